```python
import jax, jax.numpy as jnp
from jax import lax
import numpy as np

D_MODEL = 1024
BATCH = 4
SEQ = 4096
DEPTH = 4

GRID_W = 64
CTX_LEN = 256
D_MIX = D_MODEL
D_A = D_MIX // 2
A_GROUPS = 4
A_GROUP_DIM = D_A // A_GROUPS
CHUNK = 128
D_B = D_MIX - D_A
POOL_WINDOWS = (2, 4, 8, 16)
B_GROUPS = len(POOL_WINDOWS)
B_GROUP_DIM = D_B // B_GROUPS
N_HEADS_NA = 16
HEAD_DIM = D_MODEL // N_HEADS_NA
NA_ROWS = 8
NA_COLS = 16
D_FF = ((8 * D_MODEL + 3 * 256 - 1) // (3 * 256)) * 256
N_EVEN = (DEPTH + 1) // 2
N_ODD = DEPTH // 2
EPS = 1e-6
NEG_INF = -1e30
MOD_INIT = 0.5

kernel_name = "hybrid_dit_gmlp_pool_natten"


def rms_norm(x, g):
    xf = x.astype(jnp.float32)
    y = xf * lax.rsqrt(jnp.mean(xf * xf, axis=-1, keepdims=True) + EPS)
    return (y * g.astype(jnp.float32)).astype(x.dtype)


def layer_norm(x, g):
    xf = x.astype(jnp.float32)
    xc = xf - jnp.mean(xf, axis=-1, keepdims=True)
    y = xc * lax.rsqrt(jnp.mean(xc * xc, axis=-1, keepdims=True) + EPS)
    return (y * g.astype(jnp.float32)).astype(x.dtype)


def adaln(cond, w_mod, b_mod):
    m = jax.nn.silu(cond) @ w_mod + b_mod
    return jnp.split(m[..., None, :], 6, axis=-1)


def modulate(h, shift, scale):
    return h * (1 + scale) + shift


def chunk_gating(u, v, w_s, b_s):
    b, l, _ = v.shape
    vg = v.reshape(b, l // CHUNK, CHUNK, A_GROUPS, A_GROUP_DIM)
    mixed = jnp.einsum('gij,bnjgc->bnigc', w_s, vg) + b_s.T[:, :, None]
    return u * mixed.reshape(b, l, D_A)


def multiscale_pool(p, w_pool, pool_scale):
    b, l, _ = p.shape
    pf = p.astype(jnp.float32)
    cs = jnp.concatenate([jnp.zeros((b, 1, D_B), jnp.float32), jnp.cumsum(pf, axis=1)], axis=1)
    t = jnp.arange(l)
    pooled = []
    for g, w in enumerate(POOL_WINDOWS):
        lo = jnp.clip(t - w // 2, 0, l)
        hi = jnp.clip(t + w // 2, 0, l)
        cs_g = cs[..., g * B_GROUP_DIM:(g + 1) * B_GROUP_DIM]
        seg = jnp.take(cs_g, hi, axis=1) - jnp.take(cs_g, lo, axis=1)
        pooled.append(seg / (hi - lo).astype(jnp.float32)[None, :, None])
    pooled = jnp.stack(pooled, axis=2)
    diff = (pooled - pf.reshape(b, l, B_GROUPS, B_GROUP_DIM)).astype(p.dtype)
    y = jnp.einsum('blgc,gcd->blgd', diff, w_pool).reshape(b, l, D_B)
    return y * pool_scale


def ab_mixer(h, w_in, ln_v, w_s, b_s, w_pool, pool_scale, w_out):
    z = h @ w_in
    za = jax.nn.gelu(z[..., :2 * D_A])
    u, v = za[..., :D_A], layer_norm(za[..., D_A:], ln_v)
    ya = chunk_gating(u, v, w_s, b_s)
    yb = multiscale_pool(z[..., 2 * D_A:], w_pool, pool_scale)
    return jnp.concatenate([ya, yb], axis=-1) @ w_out


def na_mixer(h, hc, w_qkv, rpb, w_out, ctx_out):
    b, l, _ = h.shape
    rows = l // GRID_W
    kr = min(NA_ROWS, rows)
    scale = HEAD_DIM ** -0.5
    q, k, v = jnp.split(h @ w_qkv, 3, axis=-1)
    q = q.reshape(b, rows, GRID_W, N_HEADS_NA, HEAD_DIM)
    k = k.reshape(b, rows, GRID_W, N_HEADS_NA, HEAD_DIM)
    v = v.reshape(b, rows, GRID_W, N_HEADS_NA, HEAD_DIM)
    cl = hc.shape[1]
    qc, kc, vc = jnp.split(hc @ w_qkv, 3, axis=-1)
    kc = kc.reshape(b, cl, N_HEADS_NA, HEAD_DIM)
    vc = vc.reshape(b, cl, N_HEADS_NA, HEAD_DIM)
    r = jnp.arange(rows)
    row_start = jnp.clip(r - kr // 2, 0, rows - kr)
    row_idx = row_start[:, None] + jnp.arange(kr)
    k_rows = k[:, row_idx]
    v_rows = v[:, row_idx]
    s_nb = jnp.einsum('brqhd,brkwhd->bhrqkw', q, k_rows, preferred_element_type=jnp.float32) * scale
    col = jnp.arange(GRID_W)
    col_start = jnp.clip(col - NA_COLS // 2, 0, GRID_W - NA_COLS)
    col_mask = (col[None, :] >= col_start[:, None]) & (col[None, :] < col_start[:, None] + NA_COLS)
    dr = row_idx - r[:, None] + (NA_ROWS - 1)
    dc = jnp.clip(col[None, :] - col[:, None], -(NA_COLS - 1), NA_COLS - 1) + (NA_COLS - 1)
    bias = rpb[:, dr[:, None, :, None], dc[None, :, None, :]].astype(jnp.float32)
    s_nb = jnp.where(col_mask[None, None, None, :, None, :], s_nb + bias[None], NEG_INF)
    n_nb = kr * GRID_W
    s_nb = s_nb.reshape(b, N_HEADS_NA, rows, GRID_W, n_nb)
    s_cx = jnp.einsum('brqhd,bkhd->bhrqk', q, kc, preferred_element_type=jnp.float32) * scale
    p = jax.nn.softmax(jnp.concatenate([s_nb, s_cx], axis=-1), axis=-1)
    p_nb = p[..., :n_nb].reshape(b, N_HEADS_NA, rows, GRID_W, kr, GRID_W).astype(v.dtype)
    p_cx = p[..., n_nb:].astype(v.dtype)
    o = (jnp.einsum('bhrqkw,brkwhd->brqhd', p_nb, v_rows)
         + jnp.einsum('bhrqk,bkhd->brqhd', p_cx, vc))
    y = o.reshape(b, l, D_MODEL) @ w_out
    if ctx_out:
        qc = qc.reshape(b, cl, N_HEADS_NA, HEAD_DIM)
        sc = jnp.einsum('bqhd,bkhd->bhqk', qc, kc, preferred_element_type=jnp.float32) * scale
        pc = jax.nn.softmax(sc, axis=-1).astype(vc.dtype)
        oc = jnp.einsum('bhqk,bkhd->bqhd', pc, vc)
        yc = oc.reshape(b, cl, D_MODEL) @ w_out
    else:
        yc = None
    return y, yc


def swiglu(h, w_in, w_out):
    a, g = jnp.split(h @ w_in, 2, axis=-1)
    return (jax.nn.silu(a) * g) @ w_out


def setup_inputs(seed: int = 0) -> dict:
    key = jax.random.key(seed)
    ks = jax.random.split(key, 24)
    nrm = jax.random.normal
    f32 = jnp.float32
    return {
        "x": nrm(ks[0], (BATCH, SEQ, D_MODEL), f32),
        "c": nrm(ks[1], (BATCH, D_MODEL), f32),
        "ctx": nrm(ks[2], (BATCH, CTX_LEN, D_MODEL), f32),
        "c_ctx": nrm(ks[3], (D_MODEL,), f32),
        "w_mod": nrm(ks[4], (DEPTH, D_MODEL, 6 * D_MODEL), f32) * (MOD_INIT * D_MODEL ** -0.5),
        "b_mod": nrm(ks[5], (DEPTH, 6 * D_MODEL), f32) * 0.02,
        "norm_mix": 1.0 + 0.05 * nrm(ks[6], (DEPTH, D_MODEL), f32),
        "norm_ffn": 1.0 + 0.05 * nrm(ks[7], (DEPTH, D_MODEL), f32),
        "w_in_ab": nrm(ks[8], (N_EVEN, D_MODEL, 2 * D_A + D_B), f32) * D_MODEL ** -0.5,
        "ln_v": 1.0 + 0.05 * nrm(ks[9], (N_EVEN, D_A), f32),
        "w_spatial": nrm(ks[10], (N_EVEN, A_GROUPS, CHUNK, CHUNK), f32) * CHUNK ** -0.5,
        "b_spatial": 1.0 + 0.05 * nrm(ks[11], (N_EVEN, A_GROUPS, CHUNK), f32),
        "w_pool": nrm(ks[12], (N_EVEN, B_GROUPS, B_GROUP_DIM, B_GROUP_DIM), f32) * B_GROUP_DIM ** -0.5,
        "pool_scale": 1.0 + 0.05 * nrm(ks[13], (N_EVEN, D_B), f32),
        "w_out_ab": nrm(ks[14], (N_EVEN, D_MIX, D_MODEL), f32) * D_MIX ** -0.5,
        "w_qkv": nrm(ks[15], (N_ODD, D_MODEL, 3 * D_MODEL), f32) * D_MODEL ** -0.5,
        "rpb": nrm(ks[16], (N_ODD, N_HEADS_NA, 2 * NA_ROWS - 1, 2 * NA_COLS - 1), f32) * 0.1,
        "w_out_na": nrm(ks[17], (N_ODD, D_MODEL, D_MODEL), f32) * D_MODEL ** -0.5,
        "w_ffn_in": nrm(ks[18], (DEPTH, D_MODEL, 2 * D_FF), f32) * D_MODEL ** -0.5,
        "w_ffn_out": nrm(ks[19], (DEPTH, D_FF, D_MODEL), f32) * D_FF ** -0.5,
        "norm_final": 1.0 + 0.05 * nrm(ks[20], (D_MODEL,), f32),
    }


def reference(x, c, ctx, c_ctx, w_mod, b_mod, norm_mix, norm_ffn, w_in_ab, ln_v, w_spatial,
              b_spatial, w_pool, pool_scale, w_out_ab, w_qkv, rpb, w_out_na, w_ffn_in,
              w_ffn_out, norm_final):
    xc = ctx
    for i in range(DEPTH):
        last = i == DEPTH - 1
        odd = i % 2 == 1
        j = i // 2
        sh1, sc1, g1, sh2, sc2, g2 = adaln(c, w_mod[i], b_mod[i])
        h = modulate(rms_norm(x, norm_mix[i]), sh1, sc1)
        need_ctx = (not last) or odd
        if need_ctx:
            csh1, csc1, cg1, csh2, csc2, cg2 = adaln(c_ctx, w_mod[i], b_mod[i])
            hc = modulate(rms_norm(xc, norm_mix[i]), csh1, csc1)
        if odd:
            y, yc = na_mixer(h, hc, w_qkv[j], rpb[j], w_out_na[j], not last)
            x = x + g1 * y
            if not last:
                xc = xc + cg1 * yc
        else:
            ab = (w_in_ab[j], ln_v[j], w_spatial[j], b_spatial[j], w_pool[j], pool_scale[j], w_out_ab[j])
            x = x + g1 * ab_mixer(h, *ab)
            if not last:
                xc = xc + cg1 * ab_mixer(hc, *ab)
        x = x + g2 * swiglu(modulate(rms_norm(x, norm_ffn[i]), sh2, sc2), w_ffn_in[i], w_ffn_out[i])
        if not last:
            xc = xc + cg2 * swiglu(modulate(rms_norm(xc, norm_ffn[i]), csh2, csc2), w_ffn_in[i], w_ffn_out[i])
    return rms_norm(x, norm_final)
```

```python
import functools

import jax
import jax.numpy as jnp
from jax import lax
from jax.experimental import pallas as pl
from jax.experimental.pallas import tpu as pltpu

D_MODEL = 1024
BATCH = 4
SEQ = 4096
DEPTH = 4
GRID_W = 64
GRID_H = SEQ // GRID_W
CTX_LEN = 256
D_A = D_MODEL // 2
A_GROUPS = 4
CHUNK = 128
D_B = D_MODEL - D_A
POOL_WINDOWS = (2, 4, 8, 16)
GROUP_DIM = 128
N_HEADS = 16
HEAD_DIM = 64
NA_ROWS = 8
NA_COLS = 16
D_FF = 2816
EPS = 1e-6
NEG_INF = -1e30

N_LAT = BATCH * SEQ
N_CTX = BATCH * CTX_LEN
N_TOT = N_LAT + N_CTX
MOD_ROWS = 8
CTX_MOD_ROW = BATCH
N_MOD = 6

TM_FFN = 512
TM_AB = 256
HALO = 8
FF_CHUNK = 256
MOD_TN = 1536
HEAD_PAIR = 2 * HEAD_DIM
N_HEAD_PAIRS = N_HEADS // 2
NB_KEYS = NA_ROWS * GRID_W

F32 = jnp.float32
BF16 = jnp.bfloat16


def _const_spec(shape):
    nd = len(shape)
    return pl.BlockSpec(shape, lambda *_: (0,) * nd, pipeline_mode=pl.Buffered(1))


def _mod_spec(layer, k, tm):
    n_lat_tiles = N_LAT // tm
    tiles_per_batch = SEQ // tm

    def index_map(t, *_):
        row = jnp.where(t < n_lat_tiles, t // tiles_per_batch, CTX_MOD_ROW)
        return ((layer * MOD_ROWS + row) * N_MOD + k, 0, 0)

    return pl.BlockSpec((1, 1, D_MODEL), index_map)


def _rms(x, g):
    return x * lax.rsqrt(jnp.mean(x * x, axis=-1, keepdims=True) + EPS) * g


def _adaln_kernel(cond_ref, w_ref, b_ref, o_ref):
    c = cond_ref[...]
    s = c * jax.nn.sigmoid(c)
    o_ref[0] = jnp.dot(s, w_ref[0], preferred_element_type=F32,
                       precision=lax.Precision.HIGHEST) + b_ref[0]


def _adaln(cond, w_mod, b_mod):
    return pl.pallas_call(
        _adaln_kernel,
        grid=(DEPTH, N_MOD * D_MODEL // MOD_TN),
        in_specs=[
            pl.BlockSpec((MOD_ROWS, D_MODEL), lambda l, n: (0, 0)),
            pl.BlockSpec((1, D_MODEL, MOD_TN), lambda l, n: (l, 0, n)),
            pl.BlockSpec((1, 1, MOD_TN), lambda l, n: (l, 0, n)),
        ],
        out_specs=pl.BlockSpec((1, MOD_ROWS, MOD_TN), lambda l, n: (l, 0, n)),
        out_shape=jax.ShapeDtypeStruct((DEPTH, MOD_ROWS, N_MOD * D_MODEL), F32),
        compiler_params=pltpu.CompilerParams(
            dimension_semantics=("arbitrary", "arbitrary"), vmem_limit_bytes=32 << 20),
        name="adaln",
    )(cond, w_mod, b_mod.reshape(DEPTH, 1, N_MOD * D_MODEL))


def _ffn_kernel(x_ref, y_ref, g1_ref, sh_ref, sc_ref, g2_ref, nrm_ref, wo_ref, wi_ref,
                wout_ref, nf_ref, o_ref, hmid_ref, *, final):
    x1 = x_ref[...] + g1_ref[0] * jnp.dot(y_ref[...], wo_ref[...], preferred_element_type=F32)
    h = _rms(x1, nrm_ref[...]) * (1.0 + sc_ref[0]) + sh_ref[0]
    hb = h.astype(BF16)
    for c in range(D_FF // FF_CHUNK):
        lo = c * FF_CHUNK
        a = jnp.dot(hb, wi_ref[:, lo:lo + FF_CHUNK], preferred_element_type=F32)
        g = jnp.dot(hb, wi_ref[:, D_FF + lo:D_FF + lo + FF_CHUNK], preferred_element_type=F32)
        hmid_ref[:, lo:lo + FF_CHUNK] = (a * jax.nn.sigmoid(a) * g).astype(BF16)
    out = x1 + g2_ref[0] * jnp.dot(hmid_ref[...], wout_ref[...], preferred_element_type=F32)
    if final:
        out = _rms(out, nf_ref[...])
    o_ref[...] = out


def _ffn(xs, y, mod3, layer, norm_ffn, w_o, w_in, w_out, norm_final, *, n_rows, final):
    tm = TM_FFN
    row_spec = pl.BlockSpec((tm, D_MODEL), lambda t: (t, 0))
    return pl.pallas_call(
        functools.partial(_ffn_kernel, final=final),
        grid=(n_rows // tm,),
        in_specs=[
            row_spec, row_spec,
            _mod_spec(layer, 2, tm), _mod_spec(layer, 3, tm), _mod_spec(layer, 4, tm),
            _mod_spec(layer, 5, tm),
            _const_spec((1, D_MODEL)),
            _const_spec((D_MODEL, D_MODEL)),
            _const_spec((D_MODEL, 2 * D_FF)),
            _const_spec((D_FF, D_MODEL)),
            _const_spec((1, D_MODEL)),
        ],
        out_specs=row_spec,
        out_shape=jax.ShapeDtypeStruct((n_rows, D_MODEL), F32),
        scratch_shapes=[pltpu.VMEM((tm, D_FF), BF16)],
        compiler_params=pltpu.CompilerParams(
            dimension_semantics=("arbitrary",), vmem_limit_bytes=52 << 20),
        name="outproj_ffn",
    )(xs, y, mod3, mod3, mod3, mod3, norm_ffn, w_o, w_in, w_out, norm_final)


def _ab_kernel(x_ref, xp_ref, xn_ref, sh_ref, sc_ref, nrm_ref, win_ref, lnv_ref, ws_ref,
               bs_ref, wp_ref, ps_ref, y_ref, p_ref):
    tm = TM_AB
    t = pl.program_id(0)
    n_lat_tiles = N_LAT // tm
    is_ctx = t >= n_lat_tiles
    seq_len = jnp.where(is_ctx, CTX_LEN, SEQ)
    pos0 = jnp.where(is_ctx, 0, (t % (SEQ // tm)) * tm)

    x_all = jnp.concatenate([xp_ref[...], x_ref[...], xn_ref[...]], axis=0)
    h = _rms(x_all, nrm_ref[...]) * (1.0 + sc_ref[0]) + sh_ref[0]
    z = jnp.dot(h.astype(BF16), win_ref[...], preferred_element_type=F32)

    pos_all = pos0 - HALO + lax.broadcasted_iota(jnp.int32, (tm + 2 * HALO, D_B), 0)
    p_all = z[:, 2 * D_A:]
    p_ref[...] = jnp.where((pos_all >= 0) & (pos_all < seq_len), p_all, 0.0)

    za = jax.nn.gelu(z[HALO:HALO + tm, :2 * D_A])
    u = za[:, :D_A]
    vv = za[:, D_A:]
    vc = vv - jnp.mean(vv, axis=-1, keepdims=True)
    v = (vc * lax.rsqrt(jnp.mean(vc * vc, axis=-1, keepdims=True) + EPS) * lnv_ref[...]).astype(BF16)

    for g in range(A_GROUPS):
        cs = slice(g * GROUP_DIM, (g + 1) * GROUP_DIM)
        for n in range(tm // CHUNK):
            rs = slice(n * CHUNK, (n + 1) * CHUNK)
            mixed = jnp.dot(ws_ref[g], v[rs, cs], preferred_element_type=F32) + bs_ref[g]
            y_ref[rs, cs] = (u[rs, cs] * mixed).astype(BF16)

    pos = pos0 + lax.broadcasted_iota(jnp.int32, (tm, GROUP_DIM), 0)
    for g, w in enumerate(POOL_WINDOWS):
        half = w // 2
        cs = pl.ds(g * GROUP_DIM, GROUP_DIM)
        seg = p_ref[pl.ds(HALO - half, tm), cs]
        for d in range(-half + 1, half):
            seg = seg + p_ref[pl.ds(HALO + d, tm), cs]
        cnt = jnp.minimum(pos + half, seq_len) - jnp.maximum(pos - half, 0)
        diff = seg / cnt.astype(F32) - p_ref[pl.ds(HALO, tm), cs]
        yb = jnp.dot(diff.astype(BF16), wp_ref[g], preferred_element_type=F32)
        y_ref[:, D_A + g * GROUP_DIM:D_A + (g + 1) * GROUP_DIM] = (
            yb * ps_ref[:, g * GROUP_DIM:(g + 1) * GROUP_DIM]).astype(BF16)


def _ab_mixer(xs, mod3, layer, norm_mix, w_in, ln_v, w_s, b_s, w_pool, pool_scale):
    tm = TM_AB
    sub = tm // HALO
    last_halo_block = N_TOT // HALO - 1
    return pl.pallas_call(
        _ab_kernel,
        grid=(N_TOT // tm,),
        in_specs=[
            pl.BlockSpec((tm, D_MODEL), lambda t: (t, 0)),
            pl.BlockSpec((HALO, D_MODEL), lambda t: (jnp.maximum(t * sub - 1, 0), 0)),
            pl.BlockSpec((HALO, D_MODEL), lambda t: (jnp.minimum((t + 1) * sub, last_halo_block), 0)),
            _mod_spec(layer, 0, tm), _mod_spec(layer, 1, tm),
            _const_spec((1, D_MODEL)),
            _const_spec((D_MODEL, 2 * D_A + D_B)),
            _const_spec((1, D_A)),
            _const_spec((A_GROUPS, CHUNK, CHUNK)),
            _const_spec((A_GROUPS, CHUNK, GROUP_DIM)),
            _const_spec((A_GROUPS, GROUP_DIM, GROUP_DIM)),
            _const_spec((1, D_B)),
        ],
        out_specs=pl.BlockSpec((tm, D_MODEL), lambda t: (t, 0)),
        out_shape=jax.ShapeDtypeStruct((N_TOT, D_MODEL), BF16),
        scratch_shapes=[pltpu.VMEM((tm + 2 * HALO, D_B), F32)],
        compiler_params=pltpu.CompilerParams(
            dimension_semantics=("arbitrary",), vmem_limit_bytes=40 << 20),
        name="ab_mixer",
    )(xs, xs, xs, mod3, mod3, norm_mix, w_in, ln_v, w_s, b_s, w_pool, pool_scale)


def _qkv_kernel(x_ref, sh_ref, sc_ref, nrm_ref, w_ref, q_ref, k_ref, v_ref):
    h = _rms(x_ref[...], nrm_ref[...]) * (1.0 + sc_ref[0]) + sh_ref[0]
    hb = h.astype(BF16)
    scale = HEAD_DIM ** -0.5
    q_ref[...] = (jnp.dot(hb, w_ref[:, :D_MODEL], preferred_element_type=F32) * scale).astype(BF16)
    k_ref[...] = jnp.dot(hb, w_ref[:, D_MODEL:2 * D_MODEL], preferred_element_type=F32).astype(BF16)
    v_ref[...] = jnp.dot(hb, w_ref[:, 2 * D_MODEL:], preferred_element_type=F32).astype(BF16)


def _qkv(xs, mod3, layer, norm_mix, w_qkv):
    tm = TM_FFN
    row_spec = pl.BlockSpec((tm, D_MODEL), lambda t: (t, 0))
    out = jax.ShapeDtypeStruct((N_TOT, D_MODEL), BF16)
    return pl.pallas_call(
        _qkv_kernel,
        grid=(N_TOT // tm,),
        in_specs=[row_spec, _mod_spec(layer, 0, tm), _mod_spec(layer, 1, tm),
                  _const_spec((1, D_MODEL)), _const_spec((D_MODEL, 3 * D_MODEL))],
        out_specs=[row_spec, row_spec, row_spec],
        out_shape=[out, out, out],
        compiler_params=pltpu.CompilerParams(
            dimension_semantics=("arbitrary",), vmem_limit_bytes=40 << 20),
        name="qkv_proj",
    )(xs, mod3, mod3, norm_mix, w_qkv)


_NT_DIMS = (((1,), (1,)), ((), ()))


def _head_masks(rows):
    lane = lax.broadcasted_iota(jnp.int32, (rows, HEAD_PAIR), 1)
    return lane < HEAD_DIM


def _na_kernel(q_ref, k_ref, v_ref, kc_ref, vc_ref, bias_ref, o_ref):
    first = _head_masks(GRID_W)
    kc = kc_ref[...]
    vc = vc_ref[...]

    def row_body(r, carry):
        start = jnp.clip(r - NA_ROWS // 2, 0, GRID_H - NA_ROWS)
        dr0 = start - r + (NA_ROWS - 1)
        q_r = q_ref[pl.ds(pl.multiple_of(r * GRID_W, GRID_W), GRID_W), :]
        k_w = k_ref[pl.ds(pl.multiple_of(start * GRID_W, GRID_W), NB_KEYS), :]
        v_w = v_ref[pl.ds(pl.multiple_of(start * GRID_W, GRID_W), NB_KEYS), :]
        outs = []
        for hh in range(2):
            sel = first if hh == 0 else jnp.logical_not(first)
            qm = jnp.where(sel, q_r, jnp.zeros_like(q_r))
            s_nb = lax.dot_general(qm, k_w, _NT_DIMS, preferred_element_type=F32) + bias_ref[hh, dr0]
            s_cx = lax.dot_general(qm, kc, _NT_DIMS, preferred_element_type=F32)
            m = jnp.maximum(jnp.max(s_nb, axis=-1, keepdims=True),
                            jnp.max(s_cx, axis=-1, keepdims=True))
            p_nb = jnp.exp(s_nb - m)
            p_cx = jnp.exp(s_cx - m)
            denom = jnp.sum(p_nb, axis=-1, keepdims=True) + jnp.sum(p_cx, axis=-1, keepdims=True)
            o_h = (jnp.dot(p_nb.astype(BF16), v_w, preferred_element_type=F32)
                   + jnp.dot(p_cx.astype(BF16), vc, preferred_element_type=F32))
            outs.append(o_h / denom)
        o_ref[pl.ds(pl.multiple_of(r * GRID_W, GRID_W), GRID_W), :] = (
            jnp.where(first, outs[0], outs[1]).astype(BF16))
        return carry

    lax.fori_loop(0, GRID_H, row_body, 0)


def _na_attention(q, k, v, bias):
    lat_spec = pl.BlockSpec((SEQ, HEAD_PAIR), lambda b, hp: (b, hp))
    ctx_spec = pl.BlockSpec((CTX_LEN, HEAD_PAIR), lambda b, hp: (N_LAT // CTX_LEN + b, hp))
    return pl.pallas_call(
        _na_kernel,
        grid=(BATCH, N_HEAD_PAIRS),
        in_specs=[lat_spec, lat_spec, lat_spec, ctx_spec, ctx_spec,
                  pl.BlockSpec((2, NA_ROWS, GRID_W, NB_KEYS), lambda b, hp: (hp, 0, 0, 0))],
        out_specs=lat_spec,
        out_shape=jax.ShapeDtypeStruct((N_TOT, D_MODEL), BF16),
        compiler_params=pltpu.CompilerParams(
            dimension_semantics=("arbitrary", "arbitrary"), vmem_limit_bytes=40 << 20),
        name="na_attention",
    )(q, k, v, k, v, bias)


def _ctx_attn_kernel(q_ref, k_ref, v_ref, o_in_ref, o_ref):
    del o_in_ref
    first = _head_masks(CTX_LEN)
    q = q_ref[...]
    k = k_ref[...]
    v = v_ref[...]
    outs = []
    for hh in range(2):
        sel = first if hh == 0 else jnp.logical_not(first)
        qm = jnp.where(sel, q, jnp.zeros_like(q))
        s = lax.dot_general(qm, k, _NT_DIMS, preferred_element_type=F32)
        p = jnp.exp(s - jnp.max(s, axis=-1, keepdims=True))
        denom = jnp.sum(p, axis=-1, keepdims=True)
        outs.append(jnp.dot(p.astype(BF16), v, preferred_element_type=F32) / denom)
    o_ref[...] = jnp.where(first, outs[0], outs[1]).astype(BF16)


def _ctx_attention(q, k, v, o):
    ctx_spec = pl.BlockSpec((CTX_LEN, HEAD_PAIR), lambda b, hp: (N_LAT // CTX_LEN + b, hp))
    return pl.pallas_call(
        _ctx_attn_kernel,
        grid=(BATCH, N_HEAD_PAIRS),
        in_specs=[ctx_spec, ctx_spec, ctx_spec, pl.BlockSpec(memory_space=pl.ANY)],
        out_specs=ctx_spec,
        out_shape=jax.ShapeDtypeStruct((N_TOT, D_MODEL), BF16),
        input_output_aliases={3: 0},
        compiler_params=pltpu.CompilerParams(dimension_semantics=("arbitrary", "arbitrary")),
        name="ctx_attention",
    )(q, k, v, o)


def _bias_table(rpb):
    col = jnp.arange(GRID_W)
    col_start = jnp.clip(col - NA_COLS // 2, 0, GRID_W - NA_COLS)
    mask = (col[None, :] >= col_start[:, None]) & (col[None, :] < col_start[:, None] + NA_COLS)
    dc = jnp.clip(col[None, :] - col[:, None], -(NA_COLS - 1), NA_COLS - 1) + (NA_COLS - 1)
    dr = jnp.arange(NA_ROWS)[:, None] + jnp.arange(NA_ROWS)[None, :]
    b = rpb[:, dr[:, :, None, None], dc[None, None, :, :]]
    b = jnp.where(mask[None, None, None], b, NEG_INF)
    return b.transpose(0, 1, 3, 2, 4).reshape(N_HEADS, NA_ROWS, GRID_W, NB_KEYS)


def kernel(x, c, ctx, c_ctx, w_mod, b_mod, norm_mix, norm_ffn, w_in_ab, ln_v, w_spatial,
           b_spatial, w_pool, pool_scale, w_out_ab, w_qkv, rpb, w_out_na, w_ffn_in,
           w_ffn_out, norm_final):
    cond = jnp.zeros((MOD_ROWS, D_MODEL), F32).at[:BATCH].set(c).at[CTX_MOD_ROW].set(c_ctx)
    mod3 = _adaln(cond, w_mod, b_mod).reshape(DEPTH * MOD_ROWS * N_MOD, 1, D_MODEL)

    xs = jnp.concatenate([x.reshape(N_LAT, D_MODEL), ctx.reshape(N_CTX, D_MODEL)], axis=0)
    nf = norm_final.reshape(1, D_MODEL)
    for i in range(DEPTH):
        last = i == DEPTH - 1
        j = i // 2
        nm = norm_mix[i].reshape(1, D_MODEL)
        if i % 2 == 1:
            q, k, v = _qkv(xs, mod3, i, nm, w_qkv[j].astype(BF16))
            y = _na_attention(q, k, v, _bias_table(rpb[j]))
            if not last:
                y = _ctx_attention(q, k, v, y)
            w_o = w_out_na[j]
        else:
            b_s = jnp.broadcast_to(b_spatial[j][:, :, None], (A_GROUPS, CHUNK, GROUP_DIM))
            y = _ab_mixer(xs, mod3, i, nm, w_in_ab[j].astype(BF16), ln_v[j].reshape(1, D_A),
                          w_spatial[j].astype(BF16), b_s, w_pool[j].astype(BF16),
                          pool_scale[j].reshape(1, D_B))
            w_o = w_out_ab[j]
        xs = _ffn(xs, y, mod3, i, norm_ffn[i].reshape(1, D_MODEL), w_o.astype(BF16),
                  w_ffn_in[i].astype(BF16), w_ffn_out[i].astype(BF16), nf,
                  n_rows=N_LAT if last else N_TOT, final=last)
    return xs.reshape(BATCH, SEQ, D_MODEL)
```

```python
import functools

import jax
import jax.numpy as jnp
from jax import lax
from jax.experimental import pallas as pl
from jax.experimental.pallas import tpu as pltpu

D_MODEL = 1024
BATCH = 4
SEQ = 4096
DEPTH = 4
GRID_W = 64
GRID_H = SEQ // GRID_W
CTX_LEN = 256
D_A = D_MODEL // 2
A_GROUPS = 4
CHUNK = 128
D_B = D_MODEL - D_A
POOL_WINDOWS = (2, 4, 8, 16)
GROUP_DIM = 128
N_HEADS = 16
HEAD_DIM = 64
NA_ROWS = 8
NA_COLS = 16
D_FF = 2816
EPS = 1e-6
NEG_INF = -1e30

N_LAT = BATCH * SEQ
N_CTX = BATCH * CTX_LEN
N_TOT = N_LAT + N_CTX
MOD_ROWS = 8
CTX_MOD_ROW = BATCH
N_MOD = 6

TM_FFN = 512
TM_AB = 256
HALO = 8
FF_CHUNK = 256
MOD_TN = 1536
HEAD_PAIR = 2 * HEAD_DIM
N_HEAD_PAIRS = N_HEADS // 2
NB_KEYS = NA_ROWS * GRID_W

F32 = jnp.float32
BF16 = jnp.bfloat16


def _const_spec(shape):
    nd = len(shape)
    return pl.BlockSpec(shape, lambda *_: (0,) * nd, pipeline_mode=pl.Buffered(1))


def _mod_spec(layer, k, tm):
    n_lat_tiles = N_LAT // tm
    tiles_per_batch = SEQ // tm

    def index_map(t, *_):
        row = jnp.where(t < n_lat_tiles, t // tiles_per_batch, CTX_MOD_ROW)
        return ((layer * MOD_ROWS + row) * N_MOD + k, 0, 0)

    return pl.BlockSpec((1, 1, D_MODEL), index_map)


def _rms(x, g):
    return x * lax.rsqrt(jnp.mean(x * x, axis=-1, keepdims=True) + EPS) * g


def _adaln_kernel(cond_ref, w_ref, b_ref, o_ref):
    c = cond_ref[...]
    s = c * jax.nn.sigmoid(c)
    o_ref[0] = jnp.dot(s, w_ref[0], preferred_element_type=F32,
                       precision=lax.Precision.HIGHEST) + b_ref[0]


def _adaln(cond, w_mod, b_mod):
    return pl.pallas_call(
        _adaln_kernel,
        grid=(DEPTH, N_MOD * D_MODEL // MOD_TN),
        in_specs=[
            pl.BlockSpec((MOD_ROWS, D_MODEL), lambda l, n: (0, 0)),
            pl.BlockSpec((1, D_MODEL, MOD_TN), lambda l, n: (l, 0, n)),
            pl.BlockSpec((1, 1, MOD_TN), lambda l, n: (l, 0, n)),
        ],
        out_specs=pl.BlockSpec((1, MOD_ROWS, MOD_TN), lambda l, n: (l, 0, n)),
        out_shape=jax.ShapeDtypeStruct((DEPTH, MOD_ROWS, N_MOD * D_MODEL), F32),
        compiler_params=pltpu.CompilerParams(
            dimension_semantics=("arbitrary", "arbitrary"), vmem_limit_bytes=32 << 20),
        name="adaln",
    )(cond, w_mod, b_mod.reshape(DEPTH, 1, N_MOD * D_MODEL))


def _ffn_kernel(x_ref, y_ref, g1_ref, sh_ref, sc_ref, g2_ref, nrm_ref, wo_ref, wi_ref,
                wout_ref, nf_ref, o_ref, hmid_ref, *, final):
    x1 = x_ref[...] + g1_ref[0] * jnp.dot(y_ref[...], wo_ref[...], preferred_element_type=F32)
    h = _rms(x1, nrm_ref[...]) * (1.0 + sc_ref[0]) + sh_ref[0]
    hb = h.astype(BF16)
    for c in range(D_FF // FF_CHUNK):
        lo = c * FF_CHUNK
        a = jnp.dot(hb, wi_ref[:, lo:lo + FF_CHUNK], preferred_element_type=F32)
        g = jnp.dot(hb, wi_ref[:, D_FF + lo:D_FF + lo + FF_CHUNK], preferred_element_type=F32)
        hmid_ref[:, lo:lo + FF_CHUNK] = (a * jax.nn.sigmoid(a) * g).astype(BF16)
    out = x1 + g2_ref[0] * jnp.dot(hmid_ref[...], wout_ref[...], preferred_element_type=F32)
    if final:
        out = _rms(out, nf_ref[...])
    o_ref[...] = out


def _ffn(xs, y, mod3, layer, norm_ffn, w_o, w_in, w_out, norm_final, *, n_rows, final):
    tm = TM_FFN
    row_spec = pl.BlockSpec((tm, D_MODEL), lambda t: (t, 0))
    return pl.pallas_call(
        functools.partial(_ffn_kernel, final=final),
        grid=(n_rows // tm,),
        in_specs=[
            row_spec, row_spec,
            _mod_spec(layer, 2, tm), _mod_spec(layer, 3, tm), _mod_spec(layer, 4, tm),
            _mod_spec(layer, 5, tm),
            _const_spec((1, D_MODEL)),
            _const_spec((D_MODEL, D_MODEL)),
            _const_spec((D_MODEL, 2 * D_FF)),
            _const_spec((D_FF, D_MODEL)),
            _const_spec((1, D_MODEL)),
        ],
        out_specs=row_spec,
        out_shape=jax.ShapeDtypeStruct((n_rows, D_MODEL), F32),
        scratch_shapes=[pltpu.VMEM((tm, D_FF), BF16)],
        compiler_params=pltpu.CompilerParams(
            dimension_semantics=("arbitrary",), vmem_limit_bytes=52 << 20),
        name="outproj_ffn",
    )(xs, y, mod3, mod3, mod3, mod3, norm_ffn, w_o, w_in, w_out, norm_final)


def _ab_kernel(x_ref, xp_ref, xn_ref, sh_ref, sc_ref, nrm_ref, win_ref, lnv_ref, ws_ref,
               bs_ref, wp_ref, ps_ref, y_ref, p_ref):
    tm = TM_AB
    t = pl.program_id(0)
    n_lat_tiles = N_LAT // tm
    is_ctx = t >= n_lat_tiles
    seq_len = jnp.where(is_ctx, CTX_LEN, SEQ)
    pos0 = jnp.where(is_ctx, 0, (t % (SEQ // tm)) * tm)

    x_all = jnp.concatenate([xp_ref[...], x_ref[...], xn_ref[...]], axis=0)
    h = _rms(x_all, nrm_ref[...]) * (1.0 + sc_ref[0]) + sh_ref[0]
    z = jnp.dot(h.astype(BF16), win_ref[...], preferred_element_type=F32)

    pos_all = pos0 - HALO + lax.broadcasted_iota(jnp.int32, (tm + 2 * HALO, D_B), 0)
    p_all = z[:, 2 * D_A:]
    p_ref[...] = jnp.where((pos_all >= 0) & (pos_all < seq_len), p_all, 0.0)

    za = jax.nn.gelu(z[HALO:HALO + tm, :2 * D_A])
    u = za[:, :D_A]
    vv = za[:, D_A:]
    vc = vv - jnp.mean(vv, axis=-1, keepdims=True)
    v = (vc * lax.rsqrt(jnp.mean(vc * vc, axis=-1, keepdims=True) + EPS) * lnv_ref[...]).astype(BF16)

    for g in range(A_GROUPS):
        cs = slice(g * GROUP_DIM, (g + 1) * GROUP_DIM)
        for n in range(tm // CHUNK):
            rs = slice(n * CHUNK, (n + 1) * CHUNK)
            mixed = jnp.dot(ws_ref[g], v[rs, cs], preferred_element_type=F32) + bs_ref[g]
            y_ref[rs, cs] = (u[rs, cs] * mixed).astype(BF16)

    pos = pos0 + lax.broadcasted_iota(jnp.int32, (tm, GROUP_DIM), 0)
    for g, w in enumerate(POOL_WINDOWS):
        half = w // 2
        cs = pl.ds(g * GROUP_DIM, GROUP_DIM)
        seg = p_ref[pl.ds(HALO - half, tm), cs]
        for d in range(-half + 1, half):
            seg = seg + p_ref[pl.ds(HALO + d, tm), cs]
        cnt = jnp.minimum(pos + half, seq_len) - jnp.maximum(pos - half, 0)
        diff = seg / cnt.astype(F32) - p_ref[pl.ds(HALO, tm), cs]
        yb = jnp.dot(diff.astype(BF16), wp_ref[g], preferred_element_type=F32)
        y_ref[:, D_A + g * GROUP_DIM:D_A + (g + 1) * GROUP_DIM] = (
            yb * ps_ref[:, g * GROUP_DIM:(g + 1) * GROUP_DIM]).astype(BF16)


def _ab_mixer(xs, mod3, layer, norm_mix, w_in, ln_v, w_s, b_s, w_pool, pool_scale):
    tm = TM_AB
    sub = tm // HALO
    last_halo_block = N_TOT // HALO - 1
    return pl.pallas_call(
        _ab_kernel,
        grid=(N_TOT // tm,),
        in_specs=[
            pl.BlockSpec((tm, D_MODEL), lambda t: (t, 0)),
            pl.BlockSpec((HALO, D_MODEL), lambda t: (jnp.maximum(t * sub - 1, 0), 0)),
            pl.BlockSpec((HALO, D_MODEL), lambda t: (jnp.minimum((t + 1) * sub, last_halo_block), 0)),
            _mod_spec(layer, 0, tm), _mod_spec(layer, 1, tm),
            _const_spec((1, D_MODEL)),
            _const_spec((D_MODEL, 2 * D_A + D_B)),
            _const_spec((1, D_A)),
            _const_spec((A_GROUPS, CHUNK, CHUNK)),
            _const_spec((A_GROUPS, CHUNK, GROUP_DIM)),
            _const_spec((A_GROUPS, GROUP_DIM, GROUP_DIM)),
            _const_spec((1, D_B)),
        ],
        out_specs=pl.BlockSpec((tm, D_MODEL), lambda t: (t, 0)),
        out_shape=jax.ShapeDtypeStruct((N_TOT, D_MODEL), BF16),
        scratch_shapes=[pltpu.VMEM((tm + 2 * HALO, D_B), F32)],
        compiler_params=pltpu.CompilerParams(
            dimension_semantics=("arbitrary",), vmem_limit_bytes=40 << 20),
        name="ab_mixer",
    )(xs, xs, xs, mod3, mod3, norm_mix, w_in, ln_v, w_s, b_s, w_pool, pool_scale)


def _qkv_kernel(x_ref, sh_ref, sc_ref, nrm_ref, w_ref, q_ref, k_ref, v_ref):
    h = _rms(x_ref[...], nrm_ref[...]) * (1.0 + sc_ref[0]) + sh_ref[0]
    hb = h.astype(BF16)
    scale = HEAD_DIM ** -0.5
    q_ref[...] = (jnp.dot(hb, w_ref[:, :D_MODEL], preferred_element_type=F32) * scale).astype(BF16)
    k_ref[...] = jnp.dot(hb, w_ref[:, D_MODEL:2 * D_MODEL], preferred_element_type=F32).astype(BF16)
    v_ref[...] = jnp.dot(hb, w_ref[:, 2 * D_MODEL:], preferred_element_type=F32).astype(BF16)


def _qkv(xs, mod3, layer, norm_mix, w_qkv):
    tm = TM_FFN
    row_spec = pl.BlockSpec((tm, D_MODEL), lambda t: (t, 0))
    out = jax.ShapeDtypeStruct((N_TOT, D_MODEL), BF16)
    return pl.pallas_call(
        _qkv_kernel,
        grid=(N_TOT // tm,),
        in_specs=[row_spec, _mod_spec(layer, 0, tm), _mod_spec(layer, 1, tm),
                  _const_spec((1, D_MODEL)), _const_spec((D_MODEL, 3 * D_MODEL))],
        out_specs=[row_spec, row_spec, row_spec],
        out_shape=[out, out, out],
        compiler_params=pltpu.CompilerParams(
            dimension_semantics=("arbitrary",), vmem_limit_bytes=40 << 20),
        name="qkv_proj",
    )(xs, mod3, mod3, norm_mix, w_qkv)


_NT_DIMS = (((1,), (1,)), ((), ()))


def _head_masks(rows):
    lane = lax.broadcasted_iota(jnp.int32, (rows, HEAD_PAIR), 1)
    return lane < HEAD_DIM


def _build_bias_pairs(rpb_ref, hp, bias_ref):
    n_dc = 2 * NA_COLS - 1
    n_dr = 2 * NA_ROWS - 1
    shape = (GRID_W, HEAD_PAIR)
    c = lax.broadcasted_iota(jnp.int32, shape, 0)
    lane = lax.broadcasted_iota(jnp.int32, shape, 1)
    kc = lane % GRID_W
    half = lane // GRID_W
    col_start = jnp.clip(c - NA_COLS // 2, 0, GRID_W - NA_COLS)
    live = (kc >= col_start) & (kc < col_start + NA_COLS)
    dc = jnp.clip(kc - c, -(NA_COLS - 1), NA_COLS - 1) + (NA_COLS - 1)
    code = jnp.where(live, half * n_dc + dc, -1)

    def d_body(d, carry):
        for hh in range(2):
            base = ((2 * hp + hh) * n_dr + d) * n_dc
            t = jnp.full(shape, NEG_INF, F32)
            for off in range(2 * n_dc):
                t = jnp.where(code == off, rpb_ref[base + off], t)
            bias_ref[d, hh * GRID_W:(hh + 1) * GRID_W, :] = t
        return carry

    lax.fori_loop(0, n_dr - 1, d_body, 0)


N_KEYS = NB_KEYS + CTX_LEN
ROWS_PER_STEP = 2


def _na_kernel(rpb_ref, q_ref, k_ref, v_ref, kc_ref, vc_ref, o_ref,
               bias_ref, kbuf, vbuf, s_scr, p_scr):
    hp = pl.program_id(0)

    @pl.when(pl.program_id(1) == 0)
    def _():
        _build_bias_pairs(rpb_ref, hp, bias_ref)

    for j in range(ROWS_PER_STEP):
        kbuf[j, NB_KEYS:, :] = kc_ref[...]
        vbuf[j, NB_KEYS:, :HEAD_PAIR] = vc_ref[...]
        vbuf[j, :, HEAD_PAIR:] = jnp.ones((N_KEYS, HEAD_PAIR), BF16)

    first = _head_masks(GRID_W)

    def window_start(r):
        return jnp.clip(r - NA_ROWS // 2, 0, GRID_H - NA_ROWS)

    def rows_at(row):
        return pl.ds(pl.multiple_of(row * GRID_W, GRID_W), GRID_W)

    def window_at(row):
        return pl.ds(pl.multiple_of(row * GRID_W, GRID_W), NB_KEYS)

    def qk_stage(i):
        for j in range(ROWS_PER_STEP):
            r = i * ROWS_PER_STEP + j
            start = window_start(r)
            dr0 = start - r + (NA_ROWS - 1)
            kbuf[j, :NB_KEYS, :] = k_ref[window_at(start), :]
            q_r = q_ref[rows_at(r), :]
            zero = jnp.zeros_like(q_r)
            qs = jnp.concatenate([jnp.where(first, q_r, zero), jnp.where(first, zero, q_r)], axis=0)
            s = lax.dot_general(qs, kbuf[j], _NT_DIMS, preferred_element_type=F32)
            bias = jnp.concatenate([bias_ref[dr0 + 2 * t] for t in range(NA_ROWS // 2)], axis=1)
            s_scr[i % 2, j, :, :NB_KEYS] = s[:, :NB_KEYS] + bias
            s_scr[i % 2, j, :, NB_KEYS:] = s[:, NB_KEYS:]

    def softmax_stage(i):
        for j in range(ROWS_PER_STEP):
            s = s_scr[i % 2, j]
            p_scr[i % 2, j] = jnp.exp(s - jnp.max(s, axis=-1, keepdims=True)).astype(BF16)

    def pv_stage(i):
        for j in range(ROWS_PER_STEP):
            r = i * ROWS_PER_STEP + j
            vbuf[j, :NB_KEYS, :HEAD_PAIR] = v_ref[window_at(window_start(r)), :]
            oa = jnp.dot(p_scr[i % 2, j], vbuf[j], preferred_element_type=F32)
            o2 = oa[:, :HEAD_PAIR] / oa[:, HEAD_PAIR:]
            o_ref[rows_at(r), :] = jnp.where(first, o2[:GRID_W], o2[GRID_W:]).astype(BF16)

    n_steps = GRID_H // ROWS_PER_STEP
    qk_stage(0)
    softmax_stage(0)
    qk_stage(1)

    def step(i, carry):
        pv_stage(i - 2)
        softmax_stage(i - 1)
        qk_stage(i)
        return carry

    lax.fori_loop(2, n_steps, step, 0)
    pv_stage(n_steps - 2)
    softmax_stage(n_steps - 1)
    pv_stage(n_steps - 1)


def _na_attention(q, k, v, rpb):
    lat_spec = pl.BlockSpec((SEQ, HEAD_PAIR), lambda hp, b: (b, hp))
    ctx_spec = pl.BlockSpec((CTX_LEN, HEAD_PAIR), lambda hp, b: (N_LAT // CTX_LEN + b, hp))
    return pl.pallas_call(
        _na_kernel,
        grid=(N_HEAD_PAIRS, BATCH),
        in_specs=[pl.BlockSpec(memory_space=pltpu.SMEM),
                  lat_spec, lat_spec, lat_spec, ctx_spec, ctx_spec],
        out_specs=lat_spec,
        out_shape=jax.ShapeDtypeStruct((N_TOT, D_MODEL), BF16),
        scratch_shapes=[
            pltpu.VMEM((2 * NA_ROWS - 2, HEAD_PAIR, HEAD_PAIR), F32),
            pltpu.VMEM((ROWS_PER_STEP, N_KEYS, HEAD_PAIR), BF16),
            pltpu.VMEM((ROWS_PER_STEP, N_KEYS, 2 * HEAD_PAIR), BF16),
            pltpu.VMEM((2, ROWS_PER_STEP, HEAD_PAIR, N_KEYS), F32),
            pltpu.VMEM((2, ROWS_PER_STEP, HEAD_PAIR, N_KEYS), BF16),
        ],
        compiler_params=pltpu.CompilerParams(
            dimension_semantics=("arbitrary", "arbitrary"), vmem_limit_bytes=40 << 20),
        name="na_attention",
    )(rpb.reshape(-1), q, k, v, k, v)


def _ctx_attn_kernel(q_ref, k_ref, v_ref, o_in_ref, o_ref):
    del o_in_ref
    first = _head_masks(CTX_LEN)
    q = q_ref[...]
    k = k_ref[...]
    v = v_ref[...]
    outs = []
    for hh in range(2):
        sel = first if hh == 0 else jnp.logical_not(first)
        qm = jnp.where(sel, q, jnp.zeros_like(q))
        s = lax.dot_general(qm, k, _NT_DIMS, preferred_element_type=F32)
        p = jnp.exp(s - jnp.max(s, axis=-1, keepdims=True))
        denom = jnp.sum(p, axis=-1, keepdims=True)
        outs.append(jnp.dot(p.astype(BF16), v, preferred_element_type=F32) / denom)
    o_ref[...] = jnp.where(first, outs[0], outs[1]).astype(BF16)


def _ctx_attention(q, k, v, o):
    ctx_spec = pl.BlockSpec((CTX_LEN, HEAD_PAIR), lambda b, hp: (N_LAT // CTX_LEN + b, hp))
    return pl.pallas_call(
        _ctx_attn_kernel,
        grid=(BATCH, N_HEAD_PAIRS),
        in_specs=[ctx_spec, ctx_spec, ctx_spec, pl.BlockSpec(memory_space=pl.ANY)],
        out_specs=ctx_spec,
        out_shape=jax.ShapeDtypeStruct((N_TOT, D_MODEL), BF16),
        input_output_aliases={3: 0},
        compiler_params=pltpu.CompilerParams(dimension_semantics=("arbitrary", "arbitrary")),
        name="ctx_attention",
    )(q, k, v, o)


def kernel(x, c, ctx, c_ctx, w_mod, b_mod, norm_mix, norm_ffn, w_in_ab, ln_v, w_spatial,
           b_spatial, w_pool, pool_scale, w_out_ab, w_qkv, rpb, w_out_na, w_ffn_in,
           w_ffn_out, norm_final):
    cond = jnp.zeros((MOD_ROWS, D_MODEL), F32).at[:BATCH].set(c).at[CTX_MOD_ROW].set(c_ctx)
    mod3 = _adaln(cond, w_mod, b_mod).reshape(DEPTH * MOD_ROWS * N_MOD, 1, D_MODEL)

    xs = jnp.concatenate([x.reshape(N_LAT, D_MODEL), ctx.reshape(N_CTX, D_MODEL)], axis=0)
    nf = norm_final.reshape(1, D_MODEL)
    for i in range(DEPTH):
        last = i == DEPTH - 1
        j = i // 2
        nm = norm_mix[i].reshape(1, D_MODEL)
        if i % 2 == 1:
            q, k, v = _qkv(xs, mod3, i, nm, w_qkv[j].astype(BF16))
            y = _na_attention(q, k, v, rpb[j])
            if not last:
                y = _ctx_attention(q, k, v, y)
            w_o = w_out_na[j]
        else:
            b_s = jnp.broadcast_to(b_spatial[j][:, :, None], (A_GROUPS, CHUNK, GROUP_DIM))
            y = _ab_mixer(xs, mod3, i, nm, w_in_ab[j].astype(BF16), ln_v[j].reshape(1, D_A),
                          w_spatial[j].astype(BF16), b_s, w_pool[j].astype(BF16),
                          pool_scale[j].reshape(1, D_B))
            w_o = w_out_ab[j]
        xs = _ffn(xs, y, mod3, i, norm_ffn[i].reshape(1, D_MODEL), w_o.astype(BF16),
                  w_ffn_in[i].astype(BF16), w_ffn_out[i].astype(BF16), nf,
                  n_rows=N_LAT if last else N_TOT, final=last)
    return xs.reshape(BATCH, SEQ, D_MODEL)
```

```python
import functools

import jax
import jax.numpy as jnp
from jax import lax
from jax.experimental import pallas as pl
from jax.experimental.pallas import tpu as pltpu

D_MODEL = 1024
BATCH = 4
SEQ = 4096
DEPTH = 4
GRID_W = 64
GRID_H = SEQ // GRID_W
CTX_LEN = 256
D_A = D_MODEL // 2
A_GROUPS = 4
CHUNK = 128
D_B = D_MODEL - D_A
POOL_WINDOWS = (2, 4, 8, 16)
GROUP_DIM = 128
N_HEADS = 16
HEAD_DIM = 64
NA_ROWS = 8
NA_COLS = 16
D_FF = 2816
EPS = 1e-6
LOG2E = 1.4426950408889634
NEG_INF = -1e30

N_LAT = BATCH * SEQ
N_CTX = BATCH * CTX_LEN
N_TOT = N_LAT + N_CTX
MOD_ROWS = 8
CTX_MOD_ROW = BATCH
N_MOD = 6

TM_FFN = 512
TM_AB = 256
HALO = 8
FF_CHUNK = 256
MOD_TN = 1536
HEAD_PAIR = 2 * HEAD_DIM
N_HEAD_PAIRS = N_HEADS // 2
NB_KEYS = NA_ROWS * GRID_W

F32 = jnp.float32
BF16 = jnp.bfloat16


def _const_spec(shape):
    nd = len(shape)
    return pl.BlockSpec(shape, lambda *_: (0,) * nd, pipeline_mode=pl.Buffered(1))


def _mod_spec(layer, k, tm):
    n_lat_tiles = N_LAT // tm
    tiles_per_batch = SEQ // tm

    def index_map(t, *_):
        row = jnp.where(t < n_lat_tiles, t // tiles_per_batch, CTX_MOD_ROW)
        return ((layer * MOD_ROWS + row) * N_MOD + k, 0, 0)

    return pl.BlockSpec((1, 1, D_MODEL), index_map)


def _rms(x, g):
    return x * lax.rsqrt(jnp.mean(x * x, axis=-1, keepdims=True) + EPS) * g


def _adaln_kernel(cond_ref, w_ref, b_ref, o_ref):
    c = cond_ref[...]
    s = c * jax.nn.sigmoid(c)
    o_ref[0] = jnp.dot(s, w_ref[0], preferred_element_type=F32,
                       precision=lax.Precision.HIGHEST) + b_ref[0]


def _adaln(cond, w_mod, b_mod):
    return pl.pallas_call(
        _adaln_kernel,
        grid=(DEPTH, N_MOD * D_MODEL // MOD_TN),
        in_specs=[
            pl.BlockSpec((MOD_ROWS, D_MODEL), lambda l, n: (0, 0)),
            pl.BlockSpec((1, D_MODEL, MOD_TN), lambda l, n: (l, 0, n)),
            pl.BlockSpec((1, 1, MOD_TN), lambda l, n: (l, 0, n)),
        ],
        out_specs=pl.BlockSpec((1, MOD_ROWS, MOD_TN), lambda l, n: (l, 0, n)),
        out_shape=jax.ShapeDtypeStruct((DEPTH, MOD_ROWS, N_MOD * D_MODEL), F32),
        compiler_params=pltpu.CompilerParams(
            dimension_semantics=("arbitrary", "arbitrary"), vmem_limit_bytes=32 << 20),
        name="adaln",
    )(cond, w_mod, b_mod.reshape(DEPTH, 1, N_MOD * D_MODEL))


def _ffn_kernel(x_ref, y_ref, g1_ref, sh_ref, sc_ref, g2_ref, nrm_ref, wo_ref, wi_ref,
                wout_ref, nf_ref, o_ref, hmid_ref, *, final):
    x1 = x_ref[...] + g1_ref[0] * jnp.dot(y_ref[...], wo_ref[...], preferred_element_type=F32)
    h = _rms(x1, nrm_ref[...]) * (1.0 + sc_ref[0]) + sh_ref[0]
    hb = h.astype(BF16)
    for c in range(D_FF // FF_CHUNK):
        lo = c * FF_CHUNK
        a = jnp.dot(hb, wi_ref[:, lo:lo + FF_CHUNK], preferred_element_type=F32)
        g = jnp.dot(hb, wi_ref[:, D_FF + lo:D_FF + lo + FF_CHUNK], preferred_element_type=F32)
        hmid_ref[:, lo:lo + FF_CHUNK] = (a * jax.nn.sigmoid(a) * g).astype(BF16)
    out = x1 + g2_ref[0] * jnp.dot(hmid_ref[...], wout_ref[...], preferred_element_type=F32)
    if final:
        out = _rms(out, nf_ref[...])
    o_ref[...] = out


def _ffn(xs, y, mod3, layer, norm_ffn, w_o, w_in, w_out, norm_final, *, n_rows, final):
    tm = TM_FFN
    row_spec = pl.BlockSpec((tm, D_MODEL), lambda t: (t, 0))
    return pl.pallas_call(
        functools.partial(_ffn_kernel, final=final),
        grid=(n_rows // tm,),
        in_specs=[
            row_spec, row_spec,
            _mod_spec(layer, 2, tm), _mod_spec(layer, 3, tm), _mod_spec(layer, 4, tm),
            _mod_spec(layer, 5, tm),
            _const_spec((1, D_MODEL)),
            _const_spec((D_MODEL, D_MODEL)),
            _const_spec((D_MODEL, 2 * D_FF)),
            _const_spec((D_FF, D_MODEL)),
            _const_spec((1, D_MODEL)),
        ],
        out_specs=row_spec,
        out_shape=jax.ShapeDtypeStruct((n_rows, D_MODEL), F32),
        scratch_shapes=[pltpu.VMEM((tm, D_FF), BF16)],
        compiler_params=pltpu.CompilerParams(
            dimension_semantics=("arbitrary",), vmem_limit_bytes=52 << 20),
        name="outproj_ffn",
    )(xs, y, mod3, mod3, mod3, mod3, norm_ffn, w_o, w_in, w_out, norm_final)


def _ab_kernel(x_ref, xp_ref, xn_ref, sh_ref, sc_ref, nrm_ref, win_ref, lnv_ref, ws_ref,
               bs_ref, wp_ref, ps_ref, y_ref, p_ref):
    tm = TM_AB
    t = pl.program_id(0)
    n_lat_tiles = N_LAT // tm
    is_ctx = t >= n_lat_tiles
    seq_len = jnp.where(is_ctx, CTX_LEN, SEQ)
    pos0 = jnp.where(is_ctx, 0, (t % (SEQ // tm)) * tm)

    x_all = jnp.concatenate([xp_ref[...], x_ref[...], xn_ref[...]], axis=0)
    h = _rms(x_all, nrm_ref[...]) * (1.0 + sc_ref[0]) + sh_ref[0]
    z = jnp.dot(h.astype(BF16), win_ref[...], preferred_element_type=F32)

    pos_all = pos0 - HALO + lax.broadcasted_iota(jnp.int32, (tm + 2 * HALO, D_B), 0)
    p_all = z[:, 2 * D_A:]
    p_ref[...] = jnp.where((pos_all >= 0) & (pos_all < seq_len), p_all, 0.0)

    za = jax.nn.gelu(z[HALO:HALO + tm, :2 * D_A])
    u = za[:, :D_A]
    vv = za[:, D_A:]
    vc = vv - jnp.mean(vv, axis=-1, keepdims=True)
    v = (vc * lax.rsqrt(jnp.mean(vc * vc, axis=-1, keepdims=True) + EPS) * lnv_ref[...]).astype(BF16)

    n_blocks = tm // CHUNK
    for g in range(A_GROUPS):
        cs = slice(g * GROUP_DIM, (g + 1) * GROUP_DIM)
        v_cat = jnp.concatenate([v[n * CHUNK:(n + 1) * CHUNK, cs] for n in range(n_blocks)], axis=1)
        mixed = jnp.dot(ws_ref[g], v_cat, preferred_element_type=F32)
        for n in range(n_blocks):
            rs = slice(n * CHUNK, (n + 1) * CHUNK)
            y_ref[rs, cs] = (u[rs, cs] * (mixed[:, n * CHUNK:(n + 1) * CHUNK] + bs_ref[g])).astype(BF16)

    pos = pos0 + lax.broadcasted_iota(jnp.int32, (tm, GROUP_DIM), 0)
    for g, w in enumerate(POOL_WINDOWS):
        half = w // 2
        cs = pl.ds(g * GROUP_DIM, GROUP_DIM)
        seg = p_ref[pl.ds(HALO - half, tm), cs]
        for d in range(-half + 1, half):
            seg = seg + p_ref[pl.ds(HALO + d, tm), cs]
        cnt = jnp.minimum(pos + half, seq_len) - jnp.maximum(pos - half, 0)
        diff = seg / cnt.astype(F32) - p_ref[pl.ds(HALO, tm), cs]
        yb = jnp.dot(diff.astype(BF16), wp_ref[g], preferred_element_type=F32)
        y_ref[:, D_A + g * GROUP_DIM:D_A + (g + 1) * GROUP_DIM] = (
            yb * ps_ref[:, g * GROUP_DIM:(g + 1) * GROUP_DIM]).astype(BF16)


def _ab_mixer(xs, mod3, layer, norm_mix, w_in, ln_v, w_s, b_s, w_pool, pool_scale):
    tm = TM_AB
    sub = tm // HALO
    last_halo_block = N_TOT // HALO - 1
    return pl.pallas_call(
        _ab_kernel,
        grid=(N_TOT // tm,),
        in_specs=[
            pl.BlockSpec((tm, D_MODEL), lambda t: (t, 0)),
            pl.BlockSpec((HALO, D_MODEL), lambda t: (jnp.maximum(t * sub - 1, 0), 0)),
            pl.BlockSpec((HALO, D_MODEL), lambda t: (jnp.minimum((t + 1) * sub, last_halo_block), 0)),
            _mod_spec(layer, 0, tm), _mod_spec(layer, 1, tm),
            _const_spec((1, D_MODEL)),
            _const_spec((D_MODEL, 2 * D_A + D_B)),
            _const_spec((1, D_A)),
            _const_spec((A_GROUPS, CHUNK, CHUNK)),
            _const_spec((A_GROUPS, CHUNK, GROUP_DIM)),
            _const_spec((A_GROUPS, GROUP_DIM, GROUP_DIM)),
            _const_spec((1, D_B)),
        ],
        out_specs=pl.BlockSpec((tm, D_MODEL), lambda t: (t, 0)),
        out_shape=jax.ShapeDtypeStruct((N_TOT, D_MODEL), BF16),
        scratch_shapes=[pltpu.VMEM((tm + 2 * HALO, D_B), F32)],
        compiler_params=pltpu.CompilerParams(
            dimension_semantics=("arbitrary",), vmem_limit_bytes=40 << 20),
        name="ab_mixer",
    )(xs, xs, xs, mod3, mod3, norm_mix, w_in, ln_v, w_s, b_s, w_pool, pool_scale)


def _qkv_kernel(x_ref, sh_ref, sc_ref, nrm_ref, w_ref, q_ref, k_ref, v_ref):
    h = _rms(x_ref[...], nrm_ref[...]) * (1.0 + sc_ref[0]) + sh_ref[0]
    hb = h.astype(BF16)
    scale = HEAD_DIM ** -0.5 * LOG2E
    q_ref[...] = (jnp.dot(hb, w_ref[:, :D_MODEL], preferred_element_type=F32) * scale).astype(BF16)
    k_ref[...] = jnp.dot(hb, w_ref[:, D_MODEL:2 * D_MODEL], preferred_element_type=F32).astype(BF16)
    v_ref[...] = jnp.dot(hb, w_ref[:, 2 * D_MODEL:], preferred_element_type=F32).astype(BF16)


def _qkv(xs, mod3, layer, norm_mix, w_qkv):
    tm = TM_FFN
    row_spec = pl.BlockSpec((tm, D_MODEL), lambda t: (t, 0))
    out = jax.ShapeDtypeStruct((N_TOT, D_MODEL), BF16)
    return pl.pallas_call(
        _qkv_kernel,
        grid=(N_TOT // tm,),
        in_specs=[row_spec, _mod_spec(layer, 0, tm), _mod_spec(layer, 1, tm),
                  _const_spec((1, D_MODEL)), _const_spec((D_MODEL, 3 * D_MODEL))],
        out_specs=[row_spec, row_spec, row_spec],
        out_shape=[out, out, out],
        compiler_params=pltpu.CompilerParams(
            dimension_semantics=("arbitrary",), vmem_limit_bytes=40 << 20),
        name="qkv_proj",
    )(xs, mod3, mod3, norm_mix, w_qkv)


_NT_DIMS = (((1,), (1,)), ((), ()))


def _head_masks(rows):
    lane = lax.broadcasted_iota(jnp.int32, (rows, HEAD_PAIR), 1)
    return lane < HEAD_DIM


def _build_bias_pairs(rpb_ref, hp, bias_ref):
    n_dc = 2 * NA_COLS - 1
    n_dr = 2 * NA_ROWS - 1
    shape = (GRID_W, HEAD_PAIR)
    c = lax.broadcasted_iota(jnp.int32, shape, 0)
    lane = lax.broadcasted_iota(jnp.int32, shape, 1)
    kc = lane % GRID_W
    half = lane // GRID_W
    col_start = jnp.clip(c - NA_COLS // 2, 0, GRID_W - NA_COLS)
    live = (kc >= col_start) & (kc < col_start + NA_COLS)
    dc = jnp.clip(kc - c, -(NA_COLS - 1), NA_COLS - 1) + (NA_COLS - 1)
    code = jnp.where(live, half * n_dc + dc, -1)

    def d_body(d, carry):
        for hh in range(2):
            base = ((2 * hp + hh) * n_dr + d) * n_dc
            t = jnp.full(shape, NEG_INF, F32)
            for off in range(2 * n_dc):
                t = jnp.where(code == off, rpb_ref[base + off], t)
            bias_ref[d, hh * GRID_W:(hh + 1) * GRID_W, :] = t * LOG2E
        return carry

    lax.fori_loop(0, n_dr - 1, d_body, 0)


N_KEYS = NB_KEYS + CTX_LEN
ROWS_PER_STEP = 8


def _na_kernel(rpb_ref, q_ref, k_ref, v_ref, kc_ref, vc_ref, o_ref,
               bias_ref, kbuf, vbuf, s_scr):
    hp = pl.program_id(0)

    @pl.when(pl.program_id(1) == 0)
    def _():
        _build_bias_pairs(rpb_ref, hp, bias_ref)

    for j in range(ROWS_PER_STEP):
        kbuf[j, NB_KEYS:, :] = kc_ref[...]
        vbuf[j, NB_KEYS:, :HEAD_PAIR] = vc_ref[...]
        vbuf[j, :, HEAD_PAIR:] = jnp.ones((N_KEYS, HEAD_PAIR), BF16)

    first = _head_masks(GRID_W)

    def window_start(r):
        return jnp.clip(r - NA_ROWS // 2, 0, GRID_H - NA_ROWS)

    def rows_at(row):
        return pl.ds(pl.multiple_of(row * GRID_W, GRID_W), GRID_W)

    def window_at(row):
        return pl.ds(pl.multiple_of(row * GRID_W, GRID_W), NB_KEYS)

    def qk_stage(i):
        for j in range(ROWS_PER_STEP):
            r = i * ROWS_PER_STEP + j
            start = window_start(r)
            dr0 = start - r + (NA_ROWS - 1)
            kbuf[j, :NB_KEYS, :] = k_ref[window_at(start), :]
            q_r = q_ref[rows_at(r), :]
            zero = jnp.zeros_like(q_r)
            qs = jnp.concatenate([jnp.where(first, q_r, zero), jnp.where(first, zero, q_r)], axis=0)
            s = lax.dot_general(qs, kbuf[j], _NT_DIMS, preferred_element_type=F32)
            bias = jnp.concatenate([bias_ref[dr0 + 2 * t] for t in range(NA_ROWS // 2)], axis=1)
            s_scr[i % 2, j, :, :NB_KEYS] = s[:, :NB_KEYS] + bias
            s_scr[i % 2, j, :, NB_KEYS:] = s[:, NB_KEYS:]

    def softmax_pv_stage(i):
        for j in range(ROWS_PER_STEP):
            r = i * ROWS_PER_STEP + j
            s = s_scr[i % 2, j]
            p = jnp.exp2(s - jnp.max(s, axis=-1, keepdims=True)).astype(BF16)
            vbuf[j, :NB_KEYS, :HEAD_PAIR] = v_ref[window_at(window_start(r)), :]
            oa = jnp.dot(p, vbuf[j], preferred_element_type=F32)
            o2 = oa[:, :HEAD_PAIR] / oa[:, HEAD_PAIR:]
            o_ref[rows_at(r), :] = jnp.where(first, o2[:GRID_W], o2[GRID_W:]).astype(BF16)

    n_steps = GRID_H // ROWS_PER_STEP
    qk_stage(0)

    def step(i, carry):
        softmax_pv_stage(i - 1)
        qk_stage(i)
        return carry

    lax.fori_loop(1, n_steps, step, 0)
    softmax_pv_stage(n_steps - 1)


def _na_attention(q, k, v, rpb):
    lat_spec = pl.BlockSpec((SEQ, HEAD_PAIR), lambda hp, b: (b, hp))
    ctx_spec = pl.BlockSpec((CTX_LEN, HEAD_PAIR), lambda hp, b: (N_LAT // CTX_LEN + b, hp))
    return pl.pallas_call(
        _na_kernel,
        grid=(N_HEAD_PAIRS, BATCH),
        in_specs=[pl.BlockSpec(memory_space=pltpu.SMEM),
                  lat_spec, lat_spec, lat_spec, ctx_spec, ctx_spec],
        out_specs=lat_spec,
        out_shape=jax.ShapeDtypeStruct((N_TOT, D_MODEL), BF16),
        scratch_shapes=[
            pltpu.VMEM((2 * NA_ROWS - 2, HEAD_PAIR, HEAD_PAIR), F32),
            pltpu.VMEM((ROWS_PER_STEP, N_KEYS, HEAD_PAIR), BF16),
            pltpu.VMEM((ROWS_PER_STEP, N_KEYS, 2 * HEAD_PAIR), BF16),
            pltpu.VMEM((2, ROWS_PER_STEP, HEAD_PAIR, N_KEYS), F32),
        ],
        compiler_params=pltpu.CompilerParams(
            dimension_semantics=("arbitrary", "arbitrary"), vmem_limit_bytes=40 << 20),
        name="na_attention",
    )(rpb.reshape(-1), q, k, v, k, v)


def _ctx_attn_kernel(q_ref, k_ref, v_ref, o_in_ref, o_ref):
    del o_in_ref
    first = _head_masks(CTX_LEN)
    q = q_ref[...]
    k = k_ref[...]
    v = v_ref[...]
    outs = []
    for hh in range(2):
        sel = first if hh == 0 else jnp.logical_not(first)
        qm = jnp.where(sel, q, jnp.zeros_like(q))
        s = lax.dot_general(qm, k, _NT_DIMS, preferred_element_type=F32)
        p = jnp.exp2(s - jnp.max(s, axis=-1, keepdims=True))
        denom = jnp.sum(p, axis=-1, keepdims=True)
        outs.append(jnp.dot(p.astype(BF16), v, preferred_element_type=F32) / denom)
    o_ref[...] = jnp.where(first, outs[0], outs[1]).astype(BF16)


def _ctx_attention(q, k, v, o):
    ctx_spec = pl.BlockSpec((CTX_LEN, HEAD_PAIR), lambda b, hp: (N_LAT // CTX_LEN + b, hp))
    return pl.pallas_call(
        _ctx_attn_kernel,
        grid=(BATCH, N_HEAD_PAIRS),
        in_specs=[ctx_spec, ctx_spec, ctx_spec, pl.BlockSpec(memory_space=pl.ANY)],
        out_specs=ctx_spec,
        out_shape=jax.ShapeDtypeStruct((N_TOT, D_MODEL), BF16),
        input_output_aliases={3: 0},
        compiler_params=pltpu.CompilerParams(dimension_semantics=("arbitrary", "arbitrary")),
        name="ctx_attention",
    )(q, k, v, o)


def kernel(x, c, ctx, c_ctx, w_mod, b_mod, norm_mix, norm_ffn, w_in_ab, ln_v, w_spatial,
           b_spatial, w_pool, pool_scale, w_out_ab, w_qkv, rpb, w_out_na, w_ffn_in,
           w_ffn_out, norm_final):
    cond = jnp.zeros((MOD_ROWS, D_MODEL), F32).at[:BATCH].set(c).at[CTX_MOD_ROW].set(c_ctx)
    mod3 = _adaln(cond, w_mod, b_mod).reshape(DEPTH * MOD_ROWS * N_MOD, 1, D_MODEL)

    xs = jnp.concatenate([x.reshape(N_LAT, D_MODEL), ctx.reshape(N_CTX, D_MODEL)], axis=0)
    nf = norm_final.reshape(1, D_MODEL)
    for i in range(DEPTH):
        last = i == DEPTH - 1
        j = i // 2
        nm = norm_mix[i].reshape(1, D_MODEL)
        if i % 2 == 1:
            q, k, v = _qkv(xs, mod3, i, nm, w_qkv[j].astype(BF16))
            y = _na_attention(q, k, v, rpb[j])
            if not last:
                y = _ctx_attention(q, k, v, y)
            w_o = w_out_na[j]
        else:
            b_s = jnp.broadcast_to(b_spatial[j][:, :, None], (A_GROUPS, CHUNK, GROUP_DIM))
            y = _ab_mixer(xs, mod3, i, nm, w_in_ab[j].astype(BF16), ln_v[j].reshape(1, D_A),
                          w_spatial[j].astype(BF16), b_s, w_pool[j].astype(BF16),
                          pool_scale[j].reshape(1, D_B))
            w_o = w_out_ab[j]
        xs = _ffn(xs, y, mod3, i, norm_ffn[i].reshape(1, D_MODEL), w_o.astype(BF16),
                  w_ffn_in[i].astype(BF16), w_ffn_out[i].astype(BF16), nf,
                  n_rows=N_LAT if last else N_TOT, final=last)
    return xs.reshape(BATCH, SEQ, D_MODEL)
```

```python
import functools

import jax
import jax.numpy as jnp
from jax import lax
from jax.experimental import pallas as pl
from jax.experimental.pallas import tpu as pltpu

D_MODEL = 1024
BATCH = 4
SEQ = 4096
DEPTH = 4
GRID_W = 64
GRID_H = SEQ // GRID_W
CTX_LEN = 256
D_A = D_MODEL // 2
A_GROUPS = 4
CHUNK = 128
D_B = D_MODEL - D_A
POOL_WINDOWS = (2, 4, 8, 16)
GROUP_DIM = 128
N_HEADS = 16
HEAD_DIM = 64
NA_ROWS = 8
NA_COLS = 16
D_FF = 2816
EPS = 1e-6
LOG2E = 1.4426950408889634
NEG_INF = -1e30

N_LAT = BATCH * SEQ
N_CTX = BATCH * CTX_LEN
N_TOT = N_LAT + N_CTX
MOD_ROWS = 8
CTX_MOD_ROW = BATCH
N_MOD = 6

TM_FFN = 512
TM_AB = 256
HALO = 8
FF_CHUNK = 256
MOD_TN = 1536
HEAD_PAIR = 2 * HEAD_DIM
N_HEAD_PAIRS = N_HEADS // 2
NB_KEYS = NA_ROWS * GRID_W

F32 = jnp.float32
BF16 = jnp.bfloat16


def _const_spec(shape):
    nd = len(shape)
    return pl.BlockSpec(shape, lambda *_: (0,) * nd, pipeline_mode=pl.Buffered(1))


def _layer_spec(shape, layer):
    nd = len(shape)
    return pl.BlockSpec((None,) + tuple(shape), lambda *_: (layer,) + (0,) * nd,
                        pipeline_mode=pl.Buffered(1))


def _mod_spec(layer, k, tm):
    n_lat_tiles = N_LAT // tm
    tiles_per_batch = SEQ // tm

    def index_map(t, *_):
        row = jnp.where(t < n_lat_tiles, t // tiles_per_batch, CTX_MOD_ROW)
        return ((layer * MOD_ROWS + row) * N_MOD + k, 0, 0)

    return pl.BlockSpec((1, 1, D_MODEL), index_map)


def _rms(x, g):
    return x * lax.rsqrt(jnp.mean(x * x, axis=-1, keepdims=True) + EPS) * g


def _adaln_kernel(cond_ref, w_ref, b_ref, o_ref):
    c = cond_ref[...]
    s = c * jax.nn.sigmoid(c)
    o_ref[0] = jnp.dot(s, w_ref[0], preferred_element_type=F32,
                       precision=lax.Precision.HIGHEST) + b_ref[0]


def _adaln(cond, w_mod, b_mod):
    return pl.pallas_call(
        _adaln_kernel,
        grid=(DEPTH, N_MOD * D_MODEL // MOD_TN),
        in_specs=[
            pl.BlockSpec((MOD_ROWS, D_MODEL), lambda l, n: (0, 0)),
            pl.BlockSpec((1, D_MODEL, MOD_TN), lambda l, n: (l, 0, n)),
            pl.BlockSpec((1, 1, MOD_TN), lambda l, n: (l, 0, n)),
        ],
        out_specs=pl.BlockSpec((1, MOD_ROWS, MOD_TN), lambda l, n: (l, 0, n)),
        out_shape=jax.ShapeDtypeStruct((DEPTH, MOD_ROWS, N_MOD * D_MODEL), F32),
        compiler_params=pltpu.CompilerParams(
            dimension_semantics=("arbitrary", "arbitrary"), vmem_limit_bytes=32 << 20),
        name="adaln",
    )(cond, w_mod, b_mod.reshape(DEPTH, 1, N_MOD * D_MODEL))


def _ffn_kernel(x_ref, y_ref, g1_ref, sh_ref, sc_ref, g2_ref, nrm_ref, wo_ref, wi_ref,
                wout_ref, nf_ref, o_ref, hmid_ref, *, final):
    x1 = x_ref[...] + g1_ref[0] * jnp.dot(y_ref[...], wo_ref[...], preferred_element_type=F32)
    h = _rms(x1, nrm_ref[...]) * (1.0 + sc_ref[0]) + sh_ref[0]
    hb = h.astype(BF16)
    for c in range(D_FF // FF_CHUNK):
        lo = c * FF_CHUNK
        a = jnp.dot(hb, wi_ref[:, lo:lo + FF_CHUNK], preferred_element_type=F32)
        g = jnp.dot(hb, wi_ref[:, D_FF + lo:D_FF + lo + FF_CHUNK], preferred_element_type=F32)
        hmid_ref[:, lo:lo + FF_CHUNK] = (a * jax.nn.sigmoid(a) * g).astype(BF16)
    out = x1 + g2_ref[0] * jnp.dot(hmid_ref[...], wout_ref[...], preferred_element_type=F32)
    if final:
        out = _rms(out, nf_ref[...])
    o_ref[...] = out


def _ffn(xs, y, mod3, layer, norm_ffn, w_o, mixer_layer, w_in, w_out, norm_final, *, n_rows, final):
    tm = TM_FFN
    row_spec = pl.BlockSpec((tm, D_MODEL), lambda t: (t, 0))
    return pl.pallas_call(
        functools.partial(_ffn_kernel, final=final),
        grid=(n_rows // tm,),
        in_specs=[
            row_spec, row_spec,
            _mod_spec(layer, 2, tm), _mod_spec(layer, 3, tm), _mod_spec(layer, 4, tm),
            _mod_spec(layer, 5, tm),
            _const_spec((1, D_MODEL)),
            _layer_spec((D_MODEL, D_MODEL), mixer_layer),
            _layer_spec((D_MODEL, 2 * D_FF), layer),
            _layer_spec((D_FF, D_MODEL), layer),
            _const_spec((1, D_MODEL)),
        ],
        out_specs=row_spec,
        out_shape=jax.ShapeDtypeStruct((n_rows, D_MODEL), F32),
        scratch_shapes=[pltpu.VMEM((tm, D_FF), BF16)],
        compiler_params=pltpu.CompilerParams(
            dimension_semantics=("arbitrary",), vmem_limit_bytes=52 << 20),
        name="outproj_ffn",
    )(xs, y, mod3, mod3, mod3, mod3, norm_ffn, w_o, w_in, w_out, norm_final)


def _ab_kernel(x_ref, xp_ref, xn_ref, sh_ref, sc_ref, nrm_ref, win_ref, lnv_ref, ws_ref,
               bs_ref, wp_ref, ps_ref, y_ref, p_ref):
    tm = TM_AB
    t = pl.program_id(0)
    n_lat_tiles = N_LAT // tm
    is_ctx = t >= n_lat_tiles
    seq_len = jnp.where(is_ctx, CTX_LEN, SEQ)
    pos0 = jnp.where(is_ctx, 0, (t % (SEQ // tm)) * tm)

    x_all = jnp.concatenate([xp_ref[...], x_ref[...], xn_ref[...]], axis=0)
    h = _rms(x_all, nrm_ref[...]) * (1.0 + sc_ref[0]) + sh_ref[0]
    z = jnp.dot(h.astype(BF16), win_ref[...], preferred_element_type=F32)

    pos_all = pos0 - HALO + lax.broadcasted_iota(jnp.int32, (tm + 2 * HALO, D_B), 0)
    p_all = z[:, 2 * D_A:]
    p_ref[...] = jnp.where((pos_all >= 0) & (pos_all < seq_len), p_all, 0.0)

    za = jax.nn.gelu(z[HALO:HALO + tm, :2 * D_A])
    u = za[:, :D_A]
    vv = za[:, D_A:]
    vc = vv - jnp.mean(vv, axis=-1, keepdims=True)
    v = (vc * lax.rsqrt(jnp.mean(vc * vc, axis=-1, keepdims=True) + EPS) * lnv_ref[...]).astype(BF16)

    n_blocks = tm // CHUNK
    for g in range(A_GROUPS):
        cs = slice(g * GROUP_DIM, (g + 1) * GROUP_DIM)
        v_cat = jnp.concatenate([v[n * CHUNK:(n + 1) * CHUNK, cs] for n in range(n_blocks)], axis=1)
        mixed = jnp.dot(ws_ref[g], v_cat, preferred_element_type=F32)
        for n in range(n_blocks):
            rs = slice(n * CHUNK, (n + 1) * CHUNK)
            y_ref[rs, cs] = (u[rs, cs] * (mixed[:, n * CHUNK:(n + 1) * CHUNK] + bs_ref[g])).astype(BF16)

    pos = pos0 + lax.broadcasted_iota(jnp.int32, (tm, GROUP_DIM), 0)
    for g, w in enumerate(POOL_WINDOWS):
        half = w // 2
        cs = pl.ds(g * GROUP_DIM, GROUP_DIM)
        seg = p_ref[pl.ds(HALO - half, tm), cs]
        for d in range(-half + 1, half):
            seg = seg + p_ref[pl.ds(HALO + d, tm), cs]
        cnt = jnp.minimum(pos + half, seq_len) - jnp.maximum(pos - half, 0)
        diff = seg / cnt.astype(F32) - p_ref[pl.ds(HALO, tm), cs]
        yb = jnp.dot(diff.astype(BF16), wp_ref[g], preferred_element_type=F32)
        y_ref[:, D_A + g * GROUP_DIM:D_A + (g + 1) * GROUP_DIM] = (
            yb * ps_ref[:, g * GROUP_DIM:(g + 1) * GROUP_DIM]).astype(BF16)


def _ab_mixer(xs, mod3, layer, norm_mix, w_in, mixer_layer, ln_v, w_s, b_s, w_pool, pool_scale):
    tm = TM_AB
    sub = tm // HALO
    last_halo_block = N_TOT // HALO - 1
    return pl.pallas_call(
        _ab_kernel,
        grid=(N_TOT // tm,),
        in_specs=[
            pl.BlockSpec((tm, D_MODEL), lambda t: (t, 0)),
            pl.BlockSpec((HALO, D_MODEL), lambda t: (jnp.maximum(t * sub - 1, 0), 0)),
            pl.BlockSpec((HALO, D_MODEL), lambda t: (jnp.minimum((t + 1) * sub, last_halo_block), 0)),
            _mod_spec(layer, 0, tm), _mod_spec(layer, 1, tm),
            _const_spec((1, D_MODEL)),
            _layer_spec((D_MODEL, 2 * D_A + D_B), mixer_layer),
            _const_spec((1, D_A)),
            _const_spec((A_GROUPS, CHUNK, CHUNK)),
            _const_spec((A_GROUPS, CHUNK, GROUP_DIM)),
            _const_spec((A_GROUPS, GROUP_DIM, GROUP_DIM)),
            _const_spec((1, D_B)),
        ],
        out_specs=pl.BlockSpec((tm, D_MODEL), lambda t: (t, 0)),
        out_shape=jax.ShapeDtypeStruct((N_TOT, D_MODEL), BF16),
        scratch_shapes=[pltpu.VMEM((tm + 2 * HALO, D_B), F32)],
        compiler_params=pltpu.CompilerParams(
            dimension_semantics=("arbitrary",), vmem_limit_bytes=40 << 20),
        name="ab_mixer",
    )(xs, xs, xs, mod3, mod3, norm_mix, w_in, ln_v, w_s, b_s, w_pool, pool_scale)


def _qkv_kernel(x_ref, sh_ref, sc_ref, nrm_ref, w_ref, q_ref, k_ref, v_ref):
    h = _rms(x_ref[...], nrm_ref[...]) * (1.0 + sc_ref[0]) + sh_ref[0]
    hb = h.astype(BF16)
    scale = HEAD_DIM ** -0.5 * LOG2E
    q_ref[...] = (jnp.dot(hb, w_ref[:, :D_MODEL], preferred_element_type=F32) * scale).astype(BF16)
    k_ref[...] = jnp.dot(hb, w_ref[:, D_MODEL:2 * D_MODEL], preferred_element_type=F32).astype(BF16)
    v_ref[...] = jnp.dot(hb, w_ref[:, 2 * D_MODEL:], preferred_element_type=F32).astype(BF16)


def _qkv(xs, mod3, layer, norm_mix, w_qkv, mixer_layer):
    tm = TM_FFN
    row_spec = pl.BlockSpec((tm, D_MODEL), lambda t: (t, 0))
    out = jax.ShapeDtypeStruct((N_TOT, D_MODEL), BF16)
    return pl.pallas_call(
        _qkv_kernel,
        grid=(N_TOT // tm,),
        in_specs=[row_spec, _mod_spec(layer, 0, tm), _mod_spec(layer, 1, tm),
                  _const_spec((1, D_MODEL)), _layer_spec((D_MODEL, 3 * D_MODEL), mixer_layer)],
        out_specs=[row_spec, row_spec, row_spec],
        out_shape=[out, out, out],
        compiler_params=pltpu.CompilerParams(
            dimension_semantics=("arbitrary",), vmem_limit_bytes=40 << 20),
        name="qkv_proj",
    )(xs, mod3, mod3, norm_mix, w_qkv)


_NT_DIMS = (((1,), (1,)), ((), ()))


def _head_masks(rows):
    lane = lax.broadcasted_iota(jnp.int32, (rows, HEAD_PAIR), 1)
    return lane < HEAD_DIM


N_DR_PAIRS = 2 * NA_ROWS - 2


def _bias_rows(rpb):
    lo, hi = rpb[:, :-1], rpb[:, 1:]
    gap = jnp.zeros((N_HEADS, N_DR_PAIRS, GRID_W - 2 * NA_COLS + 1), F32)
    rows = jnp.concatenate([lo[..., NA_COLS - 1:], gap, hi, gap, lo[..., :NA_COLS - 1]], axis=-1)
    return rows[:, :, None, :]


def _build_bias_pairs(rows_ref, bias_ref):
    shape = (GRID_W, HEAD_PAIR)
    c = lax.broadcasted_iota(jnp.int32, shape, 0)
    kc = lax.broadcasted_iota(jnp.int32, shape, 1) % GRID_W
    col_start = jnp.clip(c - NA_COLS // 2, 0, GRID_W - NA_COLS)
    live = (kc >= col_start) & (kc < col_start + NA_COLS)
    for d in range(N_DR_PAIRS):
        for hh in range(2):
            t = pltpu.roll(jnp.broadcast_to(rows_ref[hh, d], shape), 0, 1, stride=1, stride_axis=0)
            bias_ref[d, hh * GRID_W:(hh + 1) * GRID_W, :] = jnp.where(live, t, NEG_INF) * LOG2E


N_KEYS = NB_KEYS + CTX_LEN
ROWS_PER_STEP = 8


def _na_kernel(rows_ref, q_ref, k_ref, v_ref, kc_ref, vc_ref, o_ref,
               bias_ref, kbuf, vbuf, s_scr):
    @pl.when(pl.program_id(1) == 0)
    def _():
        _build_bias_pairs(rows_ref, bias_ref)

    for j in range(ROWS_PER_STEP):
        kbuf[j, NB_KEYS:, :] = kc_ref[...]
        vbuf[j, NB_KEYS:, :HEAD_PAIR] = vc_ref[...]
        vbuf[j, :, HEAD_PAIR:] = jnp.ones((N_KEYS, HEAD_PAIR), BF16)

    first = _head_masks(GRID_W)

    def window_start(r):
        return jnp.clip(r - NA_ROWS // 2, 0, GRID_H - NA_ROWS)

    def rows_at(row):
        return pl.ds(pl.multiple_of(row * GRID_W, GRID_W), GRID_W)

    def window_at(row):
        return pl.ds(pl.multiple_of(row * GRID_W, GRID_W), NB_KEYS)

    def qk_stage(i):
        for j in range(ROWS_PER_STEP):
            r = i * ROWS_PER_STEP + j
            start = window_start(r)
            dr0 = start - r + (NA_ROWS - 1)
            kbuf[j, :NB_KEYS, :] = k_ref[window_at(start), :]
            q_r = q_ref[rows_at(r), :]
            zero = jnp.zeros_like(q_r)
            qs = jnp.concatenate([jnp.where(first, q_r, zero), jnp.where(first, zero, q_r)], axis=0)
            s = lax.dot_general(qs, kbuf[j], _NT_DIMS, preferred_element_type=F32)
            bias = jnp.concatenate([bias_ref[dr0 + 2 * t] for t in range(NA_ROWS // 2)], axis=1)
            s_scr[i % 2, j, :, :NB_KEYS] = s[:, :NB_KEYS] + bias
            s_scr[i % 2, j, :, NB_KEYS:] = s[:, NB_KEYS:]

    def softmax_pv_stage(i):
        for j in range(ROWS_PER_STEP):
            r = i * ROWS_PER_STEP + j
            s = s_scr[i % 2, j]
            p = jnp.exp2(s - jnp.max(s, axis=-1, keepdims=True)).astype(BF16)
            vbuf[j, :NB_KEYS, :HEAD_PAIR] = v_ref[window_at(window_start(r)), :]
            oa = jnp.dot(p, vbuf[j], preferred_element_type=F32)
            o2 = oa[:, :HEAD_PAIR] / oa[:, HEAD_PAIR:]
            o_ref[rows_at(r), :] = jnp.where(first, o2[:GRID_W], o2[GRID_W:]).astype(BF16)

    n_steps = GRID_H // ROWS_PER_STEP
    qk_stage(0)

    def step(i, carry):
        softmax_pv_stage(i - 1)
        qk_stage(i)
        return carry

    lax.fori_loop(1, n_steps, step, 0)
    softmax_pv_stage(n_steps - 1)


def _na_attention(q, k, v, rpb):
    lat_spec = pl.BlockSpec((SEQ, HEAD_PAIR), lambda hp, b: (b, hp))
    ctx_spec = pl.BlockSpec((CTX_LEN, HEAD_PAIR), lambda hp, b: (N_LAT // CTX_LEN + b, hp))
    return pl.pallas_call(
        _na_kernel,
        grid=(N_HEAD_PAIRS, BATCH),
        in_specs=[pl.BlockSpec((2, N_DR_PAIRS, 1, HEAD_PAIR), lambda hp, b: (hp, 0, 0, 0)),
                  lat_spec, lat_spec, lat_spec, ctx_spec, ctx_spec],
        out_specs=lat_spec,
        out_shape=jax.ShapeDtypeStruct((N_TOT, D_MODEL), BF16),
        scratch_shapes=[
            pltpu.VMEM((N_DR_PAIRS, HEAD_PAIR, HEAD_PAIR), F32),
            pltpu.VMEM((ROWS_PER_STEP, N_KEYS, HEAD_PAIR), BF16),
            pltpu.VMEM((ROWS_PER_STEP, N_KEYS, 2 * HEAD_PAIR), BF16),
            pltpu.VMEM((2, ROWS_PER_STEP, HEAD_PAIR, N_KEYS), F32),
        ],
        compiler_params=pltpu.CompilerParams(
            dimension_semantics=("arbitrary", "arbitrary"), vmem_limit_bytes=40 << 20),
        name="na_attention",
    )(_bias_rows(rpb), q, k, v, k, v)


def _ctx_attn_kernel(q_ref, k_ref, v_ref, o_in_ref, o_ref):
    del o_in_ref
    first = _head_masks(CTX_LEN)
    q = q_ref[...]
    k = k_ref[...]
    v = v_ref[...]
    outs = []
    for hh in range(2):
        sel = first if hh == 0 else jnp.logical_not(first)
        qm = jnp.where(sel, q, jnp.zeros_like(q))
        s = lax.dot_general(qm, k, _NT_DIMS, preferred_element_type=F32)
        p = jnp.exp2(s - jnp.max(s, axis=-1, keepdims=True))
        denom = jnp.sum(p, axis=-1, keepdims=True)
        outs.append(jnp.dot(p.astype(BF16), v, preferred_element_type=F32) / denom)
    o_ref[...] = jnp.where(first, outs[0], outs[1]).astype(BF16)


def _ctx_attention(q, k, v, o):
    ctx_spec = pl.BlockSpec((CTX_LEN, HEAD_PAIR), lambda b, hp: (N_LAT // CTX_LEN + b, hp))
    return pl.pallas_call(
        _ctx_attn_kernel,
        grid=(BATCH, N_HEAD_PAIRS),
        in_specs=[ctx_spec, ctx_spec, ctx_spec, pl.BlockSpec(memory_space=pl.ANY)],
        out_specs=ctx_spec,
        out_shape=jax.ShapeDtypeStruct((N_TOT, D_MODEL), BF16),
        input_output_aliases={3: 0},
        compiler_params=pltpu.CompilerParams(dimension_semantics=("arbitrary", "arbitrary")),
        name="ctx_attention",
    )(q, k, v, o)


def kernel(x, c, ctx, c_ctx, w_mod, b_mod, norm_mix, norm_ffn, w_in_ab, ln_v, w_spatial,
           b_spatial, w_pool, pool_scale, w_out_ab, w_qkv, rpb, w_out_na, w_ffn_in,
           w_ffn_out, norm_final):
    cond = jnp.zeros((MOD_ROWS, D_MODEL), F32).at[:BATCH].set(c).at[CTX_MOD_ROW].set(c_ctx)
    mod3 = _adaln(cond, w_mod, b_mod).reshape(DEPTH * MOD_ROWS * N_MOD, 1, D_MODEL)

    xs = jnp.concatenate([x.reshape(N_LAT, D_MODEL), ctx.reshape(N_CTX, D_MODEL)], axis=0)
    nf = norm_final.reshape(1, D_MODEL)
    w_in_ab, w_out_ab, w_qkv, w_out_na, w_ffn_in, w_ffn_out = (
        w.astype(BF16) for w in (w_in_ab, w_out_ab, w_qkv, w_out_na, w_ffn_in, w_ffn_out))
    for i in range(DEPTH):
        last = i == DEPTH - 1
        j = i // 2
        nm = norm_mix[i].reshape(1, D_MODEL)
        if i % 2 == 1:
            q, k, v = _qkv(xs, mod3, i, nm, w_qkv, j)
            y = _na_attention(q, k, v, rpb[j])
            if not last:
                y = _ctx_attention(q, k, v, y)
            w_o = w_out_na
        else:
            b_s = jnp.broadcast_to(b_spatial[j][:, :, None], (A_GROUPS, CHUNK, GROUP_DIM))
            y = _ab_mixer(xs, mod3, i, nm, w_in_ab, j, ln_v[j].reshape(1, D_A),
                          w_spatial[j].astype(BF16), b_s, w_pool[j].astype(BF16),
                          pool_scale[j].reshape(1, D_B))
            w_o = w_out_ab
        xs = _ffn(xs, y, mod3, i, norm_ffn[i].reshape(1, D_MODEL), w_o, j, w_ffn_in, w_ffn_out, nf,
                  n_rows=N_LAT if last else N_TOT, final=last)
    return xs.reshape(BATCH, SEQ, D_MODEL)
```

```python
import functools

import jax
import jax.numpy as jnp
from jax import lax
from jax.experimental import pallas as pl
from jax.experimental.pallas import tpu as pltpu

D_MODEL = 1024
BATCH = 4
SEQ = 4096
DEPTH = 4
GRID_W = 64
GRID_H = SEQ // GRID_W
CTX_LEN = 256
D_A = D_MODEL // 2
A_GROUPS = 4
CHUNK = 128
D_B = D_MODEL - D_A
POOL_WINDOWS = (2, 4, 8, 16)
GROUP_DIM = 128
N_HEADS = 16
HEAD_DIM = 64
NA_ROWS = 8
NA_COLS = 16
D_FF = 2816
EPS = 1e-6
LOG2E = 1.4426950408889634
NEG_INF = -1e30

N_LAT = BATCH * SEQ
N_CTX = BATCH * CTX_LEN
N_TOT = N_LAT + N_CTX
MOD_ROWS = 8
CTX_MOD_ROW = BATCH
N_MOD = 6

TM_FFN = 512
TM_AB = 256
HALO = 8
FF_CHUNK = 256
MOD_TN = 1536
HEAD_PAIR = 2 * HEAD_DIM
N_HEAD_PAIRS = N_HEADS // 2
NB_KEYS = NA_ROWS * GRID_W

F32 = jnp.float32
BF16 = jnp.bfloat16


def _const_spec(shape):
    nd = len(shape)
    return pl.BlockSpec(shape, lambda *_: (0,) * nd, pipeline_mode=pl.Buffered(1))


def _layer_spec(shape, layer):
    nd = len(shape)
    return pl.BlockSpec((None,) + tuple(shape), lambda *_: (layer,) + (0,) * nd,
                        pipeline_mode=pl.Buffered(1))


def _mod_spec(layer, k, tm):
    n_lat_tiles = N_LAT // tm
    tiles_per_batch = SEQ // tm

    def index_map(t, *_):
        row = jnp.where(t < n_lat_tiles, t // tiles_per_batch, CTX_MOD_ROW)
        return ((layer * MOD_ROWS + row) * N_MOD + k, 0, 0)

    return pl.BlockSpec((1, 1, D_MODEL), index_map)


def _rms(x, g):
    return x * lax.rsqrt(jnp.mean(x * x, axis=-1, keepdims=True) + EPS) * g


def _adaln_kernel(cond_ref, w_ref, b_ref, o_ref):
    c = cond_ref[...]
    s = c * jax.nn.sigmoid(c)
    o_ref[0] = jnp.dot(s, w_ref[0], preferred_element_type=F32,
                       precision=lax.Precision.HIGHEST) + b_ref[0]


def _adaln(cond, w_mod, b_mod):
    return pl.pallas_call(
        _adaln_kernel,
        grid=(DEPTH, N_MOD * D_MODEL // MOD_TN),
        in_specs=[
            pl.BlockSpec((MOD_ROWS, D_MODEL), lambda l, n: (0, 0)),
            pl.BlockSpec((1, D_MODEL, MOD_TN), lambda l, n: (l, 0, n)),
            pl.BlockSpec((1, 1, MOD_TN), lambda l, n: (l, 0, n)),
        ],
        out_specs=pl.BlockSpec((1, MOD_ROWS, MOD_TN), lambda l, n: (l, 0, n)),
        out_shape=jax.ShapeDtypeStruct((DEPTH, MOD_ROWS, N_MOD * D_MODEL), F32),
        compiler_params=pltpu.CompilerParams(
            dimension_semantics=("arbitrary", "arbitrary"), vmem_limit_bytes=32 << 20),
        name="adaln",
    )(cond, w_mod, b_mod.reshape(DEPTH, 1, N_MOD * D_MODEL))


def _ffn_kernel(x_ref, y_ref, g1_ref, sh_ref, sc_ref, g2_ref, nrm_ref, wo_ref, wi_ref,
                wout_ref, nf_ref, o_ref, hmid_ref, *, final):
    x1 = x_ref[...] + g1_ref[0] * jnp.dot(y_ref[...], wo_ref[...], preferred_element_type=F32)
    h = _rms(x1, nrm_ref[...]) * (1.0 + sc_ref[0]) + sh_ref[0]
    hb = h.astype(BF16)
    for c in range(D_FF // FF_CHUNK):
        lo = c * FF_CHUNK
        a = jnp.dot(hb, wi_ref[:, lo:lo + FF_CHUNK], preferred_element_type=F32)
        g = jnp.dot(hb, wi_ref[:, D_FF + lo:D_FF + lo + FF_CHUNK], preferred_element_type=F32)
        hmid_ref[:, lo:lo + FF_CHUNK] = (a * jax.nn.sigmoid(a) * g).astype(BF16)
    out = x1 + g2_ref[0] * jnp.dot(hmid_ref[...], wout_ref[...], preferred_element_type=F32)
    if final:
        out = _rms(out, nf_ref[...])
    o_ref[...] = out


def _ffn(xs, y, mod3, layer, norm_ffn, w_o, mixer_layer, w_in, w_out, norm_final, *, n_rows, final):
    tm = TM_FFN
    row_spec = pl.BlockSpec((tm, D_MODEL), lambda t: (t, 0))
    return pl.pallas_call(
        functools.partial(_ffn_kernel, final=final),
        grid=(n_rows // tm,),
        in_specs=[
            row_spec, row_spec,
            _mod_spec(layer, 2, tm), _mod_spec(layer, 3, tm), _mod_spec(layer, 4, tm),
            _mod_spec(layer, 5, tm),
            _const_spec((1, D_MODEL)),
            _layer_spec((D_MODEL, D_MODEL), mixer_layer),
            _layer_spec((D_MODEL, 2 * D_FF), layer),
            _layer_spec((D_FF, D_MODEL), layer),
            _const_spec((1, D_MODEL)),
        ],
        out_specs=row_spec,
        out_shape=jax.ShapeDtypeStruct((n_rows, D_MODEL), F32),
        scratch_shapes=[pltpu.VMEM((tm, D_FF), BF16)],
        compiler_params=pltpu.CompilerParams(
            dimension_semantics=("arbitrary",), vmem_limit_bytes=52 << 20),
        name="outproj_ffn",
    )(xs, y, mod3, mod3, mod3, mod3, norm_ffn, w_o, w_in, w_out, norm_final)


def _ab_kernel(x_ref, xp_ref, xn_ref, sh_ref, sc_ref, nrm_ref, win_ref, lnv_ref, ws_ref,
               bs_ref, wp_ref, ps_ref, y_ref, p_ref):
    tm = TM_AB
    t = pl.program_id(0)
    n_lat_tiles = N_LAT // tm
    is_ctx = t >= n_lat_tiles
    seq_len = jnp.where(is_ctx, CTX_LEN, SEQ)
    pos0 = jnp.where(is_ctx, 0, (t % (SEQ // tm)) * tm)

    x_all = jnp.concatenate([xp_ref[...], x_ref[...], xn_ref[...]], axis=0)
    h = _rms(x_all, nrm_ref[...]) * (1.0 + sc_ref[0]) + sh_ref[0]
    z = jnp.dot(h.astype(BF16), win_ref[...], preferred_element_type=F32)

    pos_all = pos0 - HALO + lax.broadcasted_iota(jnp.int32, (tm + 2 * HALO, D_B), 0)
    p_all = z[:, 2 * D_A:]
    p_ref[...] = jnp.where((pos_all >= 0) & (pos_all < seq_len), p_all, 0.0)

    za = jax.nn.gelu(z[HALO:HALO + tm, :2 * D_A])
    u = za[:, :D_A]
    vv = za[:, D_A:]
    vc = vv - jnp.mean(vv, axis=-1, keepdims=True)
    v = (vc * lax.rsqrt(jnp.mean(vc * vc, axis=-1, keepdims=True) + EPS) * lnv_ref[...]).astype(BF16)

    n_blocks = tm // CHUNK
    for g in range(A_GROUPS):
        cs = slice(g * GROUP_DIM, (g + 1) * GROUP_DIM)
        v_cat = jnp.concatenate([v[n * CHUNK:(n + 1) * CHUNK, cs] for n in range(n_blocks)], axis=1)
        mixed = jnp.dot(ws_ref[g], v_cat, preferred_element_type=F32)
        for n in range(n_blocks):
            rs = slice(n * CHUNK, (n + 1) * CHUNK)
            y_ref[rs, cs] = (u[rs, cs] * (mixed[:, n * CHUNK:(n + 1) * CHUNK] + bs_ref[g])).astype(BF16)

    pos = pos0 + lax.broadcasted_iota(jnp.int32, (tm, GROUP_DIM), 0)
    for g, w in enumerate(POOL_WINDOWS):
        half = w // 2
        cs = pl.ds(g * GROUP_DIM, GROUP_DIM)
        seg = p_ref[pl.ds(HALO - half, tm), cs]
        for d in range(-half + 1, half):
            seg = seg + p_ref[pl.ds(HALO + d, tm), cs]
        cnt = jnp.minimum(pos + half, seq_len) - jnp.maximum(pos - half, 0)
        diff = seg / cnt.astype(F32) - p_ref[pl.ds(HALO, tm), cs]
        yb = jnp.dot(diff.astype(BF16), wp_ref[g], preferred_element_type=F32)
        y_ref[:, D_A + g * GROUP_DIM:D_A + (g + 1) * GROUP_DIM] = (
            yb * ps_ref[:, g * GROUP_DIM:(g + 1) * GROUP_DIM]).astype(BF16)


def _ab_mixer(xs, mod3, layer, norm_mix, w_in, mixer_layer, ln_v, w_s, b_s, w_pool, pool_scale):
    tm = TM_AB
    sub = tm // HALO
    last_halo_block = N_TOT // HALO - 1
    return pl.pallas_call(
        _ab_kernel,
        grid=(N_TOT // tm,),
        in_specs=[
            pl.BlockSpec((tm, D_MODEL), lambda t: (t, 0)),
            pl.BlockSpec((HALO, D_MODEL), lambda t: (jnp.maximum(t * sub - 1, 0), 0)),
            pl.BlockSpec((HALO, D_MODEL), lambda t: (jnp.minimum((t + 1) * sub, last_halo_block), 0)),
            _mod_spec(layer, 0, tm), _mod_spec(layer, 1, tm),
            _const_spec((1, D_MODEL)),
            _layer_spec((D_MODEL, 2 * D_A + D_B), mixer_layer),
            _const_spec((1, D_A)),
            _const_spec((A_GROUPS, CHUNK, CHUNK)),
            _const_spec((A_GROUPS, CHUNK, GROUP_DIM)),
            _const_spec((A_GROUPS, GROUP_DIM, GROUP_DIM)),
            _const_spec((1, D_B)),
        ],
        out_specs=pl.BlockSpec((tm, D_MODEL), lambda t: (t, 0)),
        out_shape=jax.ShapeDtypeStruct((N_TOT, D_MODEL), BF16),
        scratch_shapes=[pltpu.VMEM((tm + 2 * HALO, D_B), F32)],
        compiler_params=pltpu.CompilerParams(
            dimension_semantics=("arbitrary",), vmem_limit_bytes=40 << 20),
        name="ab_mixer",
    )(xs, xs, xs, mod3, mod3, norm_mix, w_in, ln_v, w_s, b_s, w_pool, pool_scale)


def _qkv_kernel(x_ref, sh_ref, sc_ref, nrm_ref, w_ref, q_ref, k_ref, v_ref):
    h = _rms(x_ref[...], nrm_ref[...]) * (1.0 + sc_ref[0]) + sh_ref[0]
    hb = h.astype(BF16)
    scale = HEAD_DIM ** -0.5 * LOG2E
    q_ref[...] = (jnp.dot(hb, w_ref[:, :D_MODEL], preferred_element_type=F32) * scale).astype(BF16)
    k_ref[...] = jnp.dot(hb, w_ref[:, D_MODEL:2 * D_MODEL], preferred_element_type=F32).astype(BF16)
    v_ref[...] = jnp.dot(hb, w_ref[:, 2 * D_MODEL:], preferred_element_type=F32).astype(BF16)


def _qkv(xs, mod3, layer, norm_mix, w_qkv, mixer_layer):
    tm = TM_FFN
    row_spec = pl.BlockSpec((tm, D_MODEL), lambda t: (t, 0))
    out = jax.ShapeDtypeStruct((N_TOT, D_MODEL), BF16)
    return pl.pallas_call(
        _qkv_kernel,
        grid=(N_TOT // tm,),
        in_specs=[row_spec, _mod_spec(layer, 0, tm), _mod_spec(layer, 1, tm),
                  _const_spec((1, D_MODEL)), _layer_spec((D_MODEL, 3 * D_MODEL), mixer_layer)],
        out_specs=[row_spec, row_spec, row_spec],
        out_shape=[out, out, out],
        compiler_params=pltpu.CompilerParams(
            dimension_semantics=("arbitrary",), vmem_limit_bytes=40 << 20),
        name="qkv_proj",
    )(xs, mod3, mod3, norm_mix, w_qkv)


_NT_DIMS = (((1,), (1,)), ((), ()))


def _head_masks(rows):
    lane = lax.broadcasted_iota(jnp.int32, (rows, HEAD_PAIR), 1)
    return lane < HEAD_DIM


N_DR_PAIRS = 2 * NA_ROWS - 2


def _bias_rows(rpb):
    lo, hi = rpb[:, :-1], rpb[:, 1:]
    gap = jnp.zeros((N_HEADS, N_DR_PAIRS, GRID_W - 2 * NA_COLS + 1), F32)
    rows = jnp.concatenate([lo[..., NA_COLS - 1:], gap, hi, gap, lo[..., :NA_COLS - 1]], axis=-1)
    return rows[:, :, None, :]


def _build_bias_pairs(rows_ref, bias_ref):
    shape = (GRID_W, HEAD_PAIR)
    c = lax.broadcasted_iota(jnp.int32, shape, 0)
    kc = lax.broadcasted_iota(jnp.int32, shape, 1) % GRID_W
    col_start = jnp.clip(c - NA_COLS // 2, 0, GRID_W - NA_COLS)
    live = (kc >= col_start) & (kc < col_start + NA_COLS)
    for d in range(N_DR_PAIRS):
        for hh in range(2):
            t = pltpu.roll(jnp.broadcast_to(rows_ref[hh, d], shape), 0, 1, stride=1, stride_axis=0)
            bias_ref[d, hh * GRID_W:(hh + 1) * GRID_W, :] = jnp.where(live, t, NEG_INF) * LOG2E


N_KEYS = NB_KEYS + CTX_LEN
ROWS_PER_STEP = 8


IMAGES_PER_STEP = 2


def _na_kernel(rows_ref, q_ref, k_ref, v_ref, kc_ref, vc_ref, o_ref,
               bias_ref, vaug, s_scr):
    @pl.when(pl.program_id(1) == 0)
    def _():
        _build_bias_pairs(rows_ref, bias_ref)

    for img in range(IMAGES_PER_STEP):
        vaug[img, :SEQ, :HEAD_PAIR] = v_ref[img * SEQ:(img + 1) * SEQ, :]
        vaug[img, SEQ:, :HEAD_PAIR] = vc_ref[img * CTX_LEN:(img + 1) * CTX_LEN, :]
        vaug[img, :, HEAD_PAIR:] = jnp.ones((SEQ + CTX_LEN, HEAD_PAIR), BF16)

    first = _head_masks(GRID_W)
    steps_per_image = GRID_H // ROWS_PER_STEP

    def locate(i, j):
        img = i // steps_per_image
        r = (i % steps_per_image) * ROWS_PER_STEP + j
        return img, r, jnp.clip(r - NA_ROWS // 2, 0, GRID_H - NA_ROWS)

    def rows(row, n):
        return pl.ds(pl.multiple_of(row * GRID_W, GRID_W), n)

    def qk_stage(i):
        for j in range(ROWS_PER_STEP):
            img, r, start = locate(i, j)
            dr0 = start - r + (NA_ROWS - 1)
            q_r = q_ref[rows(img * GRID_H + r, GRID_W), :]
            zero = jnp.zeros_like(q_r)
            qs = jnp.concatenate([jnp.where(first, q_r, zero), jnp.where(first, zero, q_r)], axis=0)
            bias = jnp.concatenate([bias_ref[dr0 + 2 * t] for t in range(NA_ROWS // 2)], axis=1)
            k_w = k_ref[rows(img * GRID_H + start, NB_KEYS), :]
            kc = kc_ref[pl.ds(pl.multiple_of(img * CTX_LEN, CTX_LEN), CTX_LEN), :]
            s_scr[i % 2, j, :, :NB_KEYS] = lax.dot_general(
                qs, k_w, _NT_DIMS, preferred_element_type=F32) + bias
            s_scr[i % 2, j, :, NB_KEYS:] = lax.dot_general(qs, kc, _NT_DIMS, preferred_element_type=F32)

    def softmax_pv_stage(i):
        for j in range(ROWS_PER_STEP):
            img, r, start = locate(i, j)
            s = s_scr[i % 2, j]
            p = jnp.exp2(s - jnp.max(s, axis=-1, keepdims=True)).astype(BF16)
            oa = (jnp.dot(p[:, :NB_KEYS], vaug[img, rows(start, NB_KEYS), :], preferred_element_type=F32)
                  + jnp.dot(p[:, NB_KEYS:], vaug[img, SEQ:, :], preferred_element_type=F32))
            o2 = oa[:, :HEAD_PAIR] / oa[:, HEAD_PAIR:]
            o_ref[rows(img * GRID_H + r, GRID_W), :] = (
                jnp.where(first, o2[:GRID_W], o2[GRID_W:]).astype(BF16))

    n_steps = IMAGES_PER_STEP * steps_per_image
    qk_stage(0)

    def step(i, carry):
        softmax_pv_stage(i - 1)
        qk_stage(i)
        return carry

    lax.fori_loop(1, n_steps, step, 0)
    softmax_pv_stage(n_steps - 1)


def _na_attention(q, k, v, rpb):
    lat_rows = IMAGES_PER_STEP * SEQ
    ctx_rows = IMAGES_PER_STEP * CTX_LEN
    lat_spec = pl.BlockSpec((lat_rows, HEAD_PAIR), lambda hp, g: (g, hp))
    ctx_spec = pl.BlockSpec((ctx_rows, HEAD_PAIR), lambda hp, g: (N_LAT // ctx_rows + g, hp))
    return pl.pallas_call(
        _na_kernel,
        grid=(N_HEAD_PAIRS, BATCH // IMAGES_PER_STEP),
        in_specs=[pl.BlockSpec((2, N_DR_PAIRS, 1, HEAD_PAIR), lambda hp, g: (hp, 0, 0, 0)),
                  lat_spec, lat_spec, lat_spec, ctx_spec, ctx_spec],
        out_specs=lat_spec,
        out_shape=jax.ShapeDtypeStruct((N_TOT, D_MODEL), BF16),
        scratch_shapes=[
            pltpu.VMEM((N_DR_PAIRS, HEAD_PAIR, HEAD_PAIR), F32),
            pltpu.VMEM((IMAGES_PER_STEP, SEQ + CTX_LEN, 2 * HEAD_PAIR), BF16),
            pltpu.VMEM((2, ROWS_PER_STEP, HEAD_PAIR, N_KEYS), F32),
        ],
        compiler_params=pltpu.CompilerParams(
            dimension_semantics=("arbitrary", "arbitrary"), vmem_limit_bytes=48 << 20),
        name="na_attention",
    )(_bias_rows(rpb), q, k, v, k, v)


def _ctx_attn_kernel(q_ref, k_ref, v_ref, o_in_ref, o_ref):
    del o_in_ref
    first = _head_masks(CTX_LEN)
    q = q_ref[...]
    k = k_ref[...]
    v = v_ref[...]
    outs = []
    for hh in range(2):
        sel = first if hh == 0 else jnp.logical_not(first)
        qm = jnp.where(sel, q, jnp.zeros_like(q))
        s = lax.dot_general(qm, k, _NT_DIMS, preferred_element_type=F32)
        p = jnp.exp2(s - jnp.max(s, axis=-1, keepdims=True))
        denom = jnp.sum(p, axis=-1, keepdims=True)
        outs.append(jnp.dot(p.astype(BF16), v, preferred_element_type=F32) / denom)
    o_ref[...] = jnp.where(first, outs[0], outs[1]).astype(BF16)


def _ctx_attention(q, k, v, o):
    ctx_spec = pl.BlockSpec((CTX_LEN, HEAD_PAIR), lambda b, hp: (N_LAT // CTX_LEN + b, hp))
    return pl.pallas_call(
        _ctx_attn_kernel,
        grid=(BATCH, N_HEAD_PAIRS),
        in_specs=[ctx_spec, ctx_spec, ctx_spec, pl.BlockSpec(memory_space=pl.ANY)],
        out_specs=ctx_spec,
        out_shape=jax.ShapeDtypeStruct((N_TOT, D_MODEL), BF16),
        input_output_aliases={3: 0},
        compiler_params=pltpu.CompilerParams(dimension_semantics=("arbitrary", "arbitrary")),
        name="ctx_attention",
    )(q, k, v, o)


def kernel(x, c, ctx, c_ctx, w_mod, b_mod, norm_mix, norm_ffn, w_in_ab, ln_v, w_spatial,
           b_spatial, w_pool, pool_scale, w_out_ab, w_qkv, rpb, w_out_na, w_ffn_in,
           w_ffn_out, norm_final):
    cond = jnp.zeros((MOD_ROWS, D_MODEL), F32).at[:BATCH].set(c).at[CTX_MOD_ROW].set(c_ctx)
    mod3 = _adaln(cond, w_mod, b_mod).reshape(DEPTH * MOD_ROWS * N_MOD, 1, D_MODEL)

    xs = jnp.concatenate([x.reshape(N_LAT, D_MODEL), ctx.reshape(N_CTX, D_MODEL)], axis=0)
    nf = norm_final.reshape(1, D_MODEL)
    w_in_ab, w_out_ab, w_qkv, w_out_na, w_ffn_in, w_ffn_out = (
        w.astype(BF16) for w in (w_in_ab, w_out_ab, w_qkv, w_out_na, w_ffn_in, w_ffn_out))
    for i in range(DEPTH):
        last = i == DEPTH - 1
        j = i // 2
        nm = norm_mix[i].reshape(1, D_MODEL)
        if i % 2 == 1:
            q, k, v = _qkv(xs, mod3, i, nm, w_qkv, j)
            y = _na_attention(q, k, v, rpb[j])
            if not last:
                y = _ctx_attention(q, k, v, y)
            w_o = w_out_na
        else:
            b_s = jnp.broadcast_to(b_spatial[j][:, :, None], (A_GROUPS, CHUNK, GROUP_DIM))
            y = _ab_mixer(xs, mod3, i, nm, w_in_ab, j, ln_v[j].reshape(1, D_A),
                          w_spatial[j].astype(BF16), b_s, w_pool[j].astype(BF16),
                          pool_scale[j].reshape(1, D_B))
            w_o = w_out_ab
        xs = _ffn(xs, y, mod3, i, norm_ffn[i].reshape(1, D_MODEL), w_o, j, w_ffn_in, w_ffn_out, nf,
                  n_rows=N_LAT if last else N_TOT, final=last)
    return xs.reshape(BATCH, SEQ, D_MODEL)
```

```python
import functools
import math

import jax
import jax.numpy as jnp
from jax import lax
from jax.experimental import pallas as pl
from jax.experimental.pallas import tpu as pltpu

D_MODEL = 1024
BATCH = 4
SEQ = 4096
DEPTH = 4
GRID_W = 64
GRID_H = SEQ // GRID_W
CTX_LEN = 256
D_A = D_MODEL // 2
A_GROUPS = 4
CHUNK = 128
D_B = D_MODEL - D_A
POOL_WINDOWS = (2, 4, 8, 16)
GROUP_DIM = 128
N_HEADS = 16
HEAD_DIM = 64
NA_ROWS = 8
NA_COLS = 16
D_FF = 2816
EPS = 1e-6
LOG2E = 1.4426950408889634
NEG_INF = -1e30

N_LAT = BATCH * SEQ
N_CTX = BATCH * CTX_LEN
N_TOT = N_LAT + N_CTX
MOD_ROWS = 8
CTX_MOD_ROW = BATCH
N_MOD = 6

TM_FFN = 512
TM_AB = 256
HALO = 8
FF_CHUNK = 256
MOD_TN = 1536
HEAD_PAIR = 2 * HEAD_DIM
N_HEAD_PAIRS = N_HEADS // 2
NB_KEYS = NA_ROWS * GRID_W

F32 = jnp.float32
BF16 = jnp.bfloat16


def _const_spec(shape):
    nd = len(shape)
    return pl.BlockSpec(shape, lambda *_: (0,) * nd, pipeline_mode=pl.Buffered(1))


def _layer_spec(shape, layer):
    nd = len(shape)
    return pl.BlockSpec((None,) + tuple(shape), lambda *_: (layer,) + (0,) * nd,
                        pipeline_mode=pl.Buffered(1))


def _mod_spec(layer, k, tm):
    n_lat_tiles = N_LAT // tm
    tiles_per_batch = SEQ // tm

    def index_map(t, *_):
        row = jnp.where(t < n_lat_tiles, t // tiles_per_batch, CTX_MOD_ROW)
        return ((layer * MOD_ROWS + row) * N_MOD + k, 0, 0)

    return pl.BlockSpec((1, 1, D_MODEL), index_map)


def _stream_specs(tm, split):
    n_lat = N_LAT // tm
    ctx_base = 0 if split else n_lat
    lat = pl.BlockSpec((tm, D_MODEL), lambda t: (jnp.minimum(t, n_lat - 1), 0))
    ctx = pl.BlockSpec((tm, D_MODEL), lambda t: (ctx_base + jnp.maximum(t - n_lat, 0), 0))
    return lat, ctx


def _stream_tile(xl_ref, xc_ref, tm):
    return jnp.where(pl.program_id(0) >= N_LAT // tm, xc_ref[...], xl_ref[...])


def _rms(x, g):
    return x * lax.rsqrt(jnp.mean(x * x, axis=-1, keepdims=True) + EPS) * g


def _adaln_kernel(cond_ref, w_ref, b_ref, o_ref):
    c = cond_ref[...]
    s = c * jax.nn.sigmoid(c)
    o_ref[0] = jnp.dot(s, w_ref[0], preferred_element_type=F32,
                       precision=lax.Precision.HIGHEST) + b_ref[0]


def _adaln(cond, w_mod, b_mod):
    return pl.pallas_call(
        _adaln_kernel,
        grid=(DEPTH, N_MOD * D_MODEL // MOD_TN),
        in_specs=[
            pl.BlockSpec((MOD_ROWS, D_MODEL), lambda l, n: (0, 0)),
            pl.BlockSpec((1, D_MODEL, MOD_TN), lambda l, n: (l, 0, n)),
            pl.BlockSpec((1, 1, MOD_TN), lambda l, n: (l, 0, n)),
        ],
        out_specs=pl.BlockSpec((1, MOD_ROWS, MOD_TN), lambda l, n: (l, 0, n)),
        out_shape=jax.ShapeDtypeStruct((DEPTH, MOD_ROWS, N_MOD * D_MODEL), F32),
        compiler_params=pltpu.CompilerParams(
            dimension_semantics=("arbitrary", "arbitrary"), vmem_limit_bytes=32 << 20),
        name="adaln",
    )(cond, w_mod, b_mod.reshape(DEPTH, 1, N_MOD * D_MODEL))


def _ffn_kernel(xl_ref, xc_ref, y_ref, g1_ref, sh_ref, sc_ref, g2_ref, nrm_ref, wo_ref, wi_ref,
                wout_ref, nf_ref, o_ref, hmid_ref, *, final):
    x1 = _stream_tile(xl_ref, xc_ref, TM_FFN) + g1_ref[0] * jnp.dot(
        y_ref[...], wo_ref[...], preferred_element_type=F32)
    h = _rms(x1, nrm_ref[...]) * (1.0 + sc_ref[0]) + sh_ref[0]
    hb = h.astype(BF16)
    for c in range(D_FF // FF_CHUNK):
        lo = c * FF_CHUNK
        a = jnp.dot(hb, wi_ref[:, lo:lo + FF_CHUNK], preferred_element_type=F32)
        g = jnp.dot(hb, wi_ref[:, D_FF + lo:D_FF + lo + FF_CHUNK], preferred_element_type=F32)
        hmid_ref[:, lo:lo + FF_CHUNK] = (a * jax.nn.sigmoid(a) * g).astype(BF16)
    out = x1 + g2_ref[0] * jnp.dot(hmid_ref[...], wout_ref[...], preferred_element_type=F32)
    if final:
        out = _rms(out, nf_ref[...])
    o_ref[...] = out


def _ffn(x_lat, x_ctx, y, mod3, layer, norm_ffn, w_o, mixer_layer, w_in, w_out, norm_final, *,
         n_rows, final):
    tm = TM_FFN
    row_spec = pl.BlockSpec((tm, D_MODEL), lambda t: (t, 0))
    lat_spec, ctx_spec = _stream_specs(tm, x_ctx is not x_lat)
    return pl.pallas_call(
        functools.partial(_ffn_kernel, final=final),
        grid=(n_rows // tm,),
        in_specs=[
            lat_spec, ctx_spec, row_spec,
            _mod_spec(layer, 2, tm), _mod_spec(layer, 3, tm), _mod_spec(layer, 4, tm),
            _mod_spec(layer, 5, tm),
            _const_spec((1, D_MODEL)),
            _layer_spec((D_MODEL, D_MODEL), mixer_layer),
            _layer_spec((D_MODEL, 2 * D_FF), layer),
            _layer_spec((D_FF, D_MODEL), layer),
            _const_spec((1, D_MODEL)),
        ],
        out_specs=row_spec,
        out_shape=jax.ShapeDtypeStruct((n_rows, D_MODEL), F32),
        scratch_shapes=[pltpu.VMEM((tm, D_FF), BF16)],
        compiler_params=pltpu.CompilerParams(
            dimension_semantics=("arbitrary",), vmem_limit_bytes=52 << 20),
        name="outproj_ffn",
    )(x_lat, x_ctx, y, mod3, mod3, mod3, mod3, norm_ffn, w_o, w_in, w_out, norm_final)


def _gelu_tanh(x):
    k1 = -2.0 * math.sqrt(2.0 / math.pi) * LOG2E
    return x / (1.0 + jnp.exp2(x * (k1 + (k1 * 0.044715) * (x * x))))


def _ab_kernel(xl_ref, xc_ref, xp_ref, xn_ref, sh_ref, sc_ref, nrm_ref, win_ref, lnv_ref, ws_ref,
               bs_ref, wp_ref, ps_ref, y_ref, p_ref, a2_ref, a4_ref):
    tm = TM_AB
    t = pl.program_id(0)
    n_lat_tiles = N_LAT // tm
    is_ctx = t >= n_lat_tiles
    seq_len = jnp.where(is_ctx, CTX_LEN, SEQ)
    pos0 = jnp.where(is_ctx, 0, (t % (SEQ // tm)) * tm)

    x_all = jnp.concatenate([xp_ref[...], _stream_tile(xl_ref, xc_ref, tm), xn_ref[...]], axis=0)
    h = _rms(x_all, nrm_ref[...]) * (1.0 + sc_ref[0]) + sh_ref[0]
    z = jnp.dot(h.astype(BF16), win_ref[...], preferred_element_type=F32)

    n_p = tm + 2 * HALO
    pos_all = pos0 - HALO + lax.broadcasted_iota(jnp.int32, (n_p, D_B), 0)
    p_ref[:n_p, :] = jnp.where((pos_all >= 0) & (pos_all < seq_len), z[:, 2 * D_A:], 0.0)
    p_ref[n_p:, :] = jnp.zeros((2 * HALO, D_B), F32)

    za = _gelu_tanh(z[HALO:HALO + tm, :2 * D_A])
    u = za[:, :D_A]
    vv = za[:, D_A:]
    vc = vv - jnp.mean(vv, axis=-1, keepdims=True)
    v = (vc * lax.rsqrt(jnp.mean(vc * vc, axis=-1, keepdims=True) + EPS) * lnv_ref[...]).astype(BF16)

    n_blocks = tm // CHUNK
    for g in range(A_GROUPS):
        cs = slice(g * GROUP_DIM, (g + 1) * GROUP_DIM)
        v_cat = jnp.concatenate([v[n * CHUNK:(n + 1) * CHUNK, cs] for n in range(n_blocks)], axis=1)
        mixed = jnp.dot(ws_ref[g], v_cat, preferred_element_type=F32)
        for n in range(n_blocks):
            rs = slice(n * CHUNK, (n + 1) * CHUNK)
            y_ref[rs, cs] = (u[rs, cs] * (mixed[:, n * CHUNK:(n + 1) * CHUNK] + bs_ref[g])).astype(BF16)

    a2_ref[...] = p_ref[0:n_p + HALO, GROUP_DIM:] + p_ref[1:n_p + HALO + 1, GROUP_DIM:]
    a4_ref[...] = a2_ref[0:n_p, :] + a2_ref[2:n_p + 2, :]
    a8 = a4_ref[0:tm + HALO, 2 * GROUP_DIM:] + a4_ref[4:tm + HALO + 4, 2 * GROUP_DIM:]
    segs = (
        p_ref[HALO - 1:HALO - 1 + tm, :GROUP_DIM] + p_ref[HALO:HALO + tm, :GROUP_DIM],
        a4_ref[HALO - 2:HALO - 2 + tm, :GROUP_DIM],
        a4_ref[HALO - 4:HALO - 4 + tm, GROUP_DIM:2 * GROUP_DIM] + a4_ref[HALO:HALO + tm, GROUP_DIM:2 * GROUP_DIM],
        a8[0:tm] + a8[HALO:HALO + tm],
    )

    def window_count(first_row, half):
        pos = pos0 + first_row + lax.broadcasted_iota(jnp.int32, (HALO, GROUP_DIM), 0)
        return (jnp.minimum(pos + half, seq_len) - jnp.maximum(pos - half, 0)).astype(F32)

    for g, w in enumerate(POOL_WINDOWS):
        half = w // 2
        cs = slice(g * GROUP_DIM, (g + 1) * GROUP_DIM)
        seg = segs[g]
        pooled = jnp.concatenate([seg[:HALO] / window_count(0, half),
                                  seg[HALO:tm - HALO] * (1.0 / w),
                                  seg[tm - HALO:] / window_count(tm - HALO, half)], axis=0)
        diff = pooled - p_ref[HALO:HALO + tm, cs]
        yb = jnp.dot(diff.astype(BF16), wp_ref[g], preferred_element_type=F32)
        y_ref[:, D_A + g * GROUP_DIM:D_A + (g + 1) * GROUP_DIM] = (yb * ps_ref[:, cs]).astype(BF16)


def _ab_mixer(x_lat, x_ctx, mod3, layer, norm_mix, w_in, mixer_layer, ln_v, w_s, b_s, w_pool,
              pool_scale):
    tm = TM_AB
    sub = tm // HALO
    last_halo_block = x_lat.shape[0] // HALO - 1
    lat_spec, ctx_spec = _stream_specs(tm, x_ctx is not x_lat)
    return pl.pallas_call(
        _ab_kernel,
        grid=(N_TOT // tm,),
        in_specs=[
            lat_spec, ctx_spec,
            pl.BlockSpec((HALO, D_MODEL), lambda t: (jnp.clip(t * sub - 1, 0, last_halo_block), 0)),
            pl.BlockSpec((HALO, D_MODEL), lambda t: (jnp.minimum((t + 1) * sub, last_halo_block), 0)),
            _mod_spec(layer, 0, tm), _mod_spec(layer, 1, tm),
            _const_spec((1, D_MODEL)),
            _layer_spec((D_MODEL, 2 * D_A + D_B), mixer_layer),
            _const_spec((1, D_A)),
            _const_spec((A_GROUPS, CHUNK, CHUNK)),
            _const_spec((A_GROUPS, CHUNK, GROUP_DIM)),
            _const_spec((A_GROUPS, GROUP_DIM, GROUP_DIM)),
            _const_spec((1, D_B)),
        ],
        out_specs=pl.BlockSpec((tm, D_MODEL), lambda t: (t, 0)),
        out_shape=jax.ShapeDtypeStruct((N_TOT, D_MODEL), BF16),
        scratch_shapes=[pltpu.VMEM((tm + 4 * HALO, D_B), F32),
                        pltpu.VMEM((tm + 3 * HALO, D_B - GROUP_DIM), F32),
                        pltpu.VMEM((tm + 2 * HALO, D_B - GROUP_DIM), F32)],
        compiler_params=pltpu.CompilerParams(
            dimension_semantics=("arbitrary",), vmem_limit_bytes=40 << 20),
        name="ab_mixer",
    )(x_lat, x_ctx, x_lat, x_lat, mod3, mod3, norm_mix, w_in, ln_v, w_s, b_s, w_pool, pool_scale)


def _qkv_kernel(x_ref, sh_ref, sc_ref, nrm_ref, w_ref, q_ref, k_ref, v_ref):
    h = _rms(x_ref[...], nrm_ref[...]) * (1.0 + sc_ref[0]) + sh_ref[0]
    hb = h.astype(BF16)
    scale = HEAD_DIM ** -0.5 * LOG2E
    q_ref[...] = (jnp.dot(hb, w_ref[:, :D_MODEL], preferred_element_type=F32) * scale).astype(BF16)
    k_ref[...] = jnp.dot(hb, w_ref[:, D_MODEL:2 * D_MODEL], preferred_element_type=F32).astype(BF16)
    v_ref[...] = jnp.dot(hb, w_ref[:, 2 * D_MODEL:], preferred_element_type=F32).astype(BF16)


def _qkv(xs, mod3, layer, norm_mix, w_qkv, mixer_layer):
    tm = TM_FFN
    row_spec = pl.BlockSpec((tm, D_MODEL), lambda t: (t, 0))
    out = jax.ShapeDtypeStruct((N_TOT, D_MODEL), BF16)
    return pl.pallas_call(
        _qkv_kernel,
        grid=(N_TOT // tm,),
        in_specs=[row_spec, _mod_spec(layer, 0, tm), _mod_spec(layer, 1, tm),
                  _const_spec((1, D_MODEL)), _layer_spec((D_MODEL, 3 * D_MODEL), mixer_layer)],
        out_specs=[row_spec, row_spec, row_spec],
        out_shape=[out, out, out],
        compiler_params=pltpu.CompilerParams(
            dimension_semantics=("arbitrary",), vmem_limit_bytes=40 << 20),
        name="qkv_proj",
    )(xs, mod3, mod3, norm_mix, w_qkv)


_NT_DIMS = (((1,), (1,)), ((), ()))


def _head_masks(rows):
    lane = lax.broadcasted_iota(jnp.int32, (rows, HEAD_PAIR), 1)
    return lane < HEAD_DIM


N_DR_PAIRS = 2 * NA_ROWS - 2


def _bias_rows(rpb):
    lo, hi = rpb[:, :-1], rpb[:, 1:]
    gap = jnp.zeros((N_HEADS, N_DR_PAIRS, GRID_W - 2 * NA_COLS + 1), F32)
    rows = jnp.concatenate([lo[..., NA_COLS - 1:], gap, hi, gap, lo[..., :NA_COLS - 1]], axis=-1)
    return rows[:, :, None, :]


def _build_bias_pairs(rows_ref, bias_ref):
    shape = (GRID_W, HEAD_PAIR)
    c = lax.broadcasted_iota(jnp.int32, shape, 0)
    kc = lax.broadcasted_iota(jnp.int32, shape, 1) % GRID_W
    col_start = jnp.clip(c - NA_COLS // 2, 0, GRID_W - NA_COLS)
    live = (kc >= col_start) & (kc < col_start + NA_COLS)
    for d in range(N_DR_PAIRS):
        for hh in range(2):
            t = pltpu.roll(jnp.broadcast_to(rows_ref[hh, d], shape), 0, 1, stride=1, stride_axis=0)
            bias_ref[d, hh * GRID_W:(hh + 1) * GRID_W, :] = jnp.where(live, t, NEG_INF) * LOG2E


N_KEYS = NB_KEYS + CTX_LEN
ROWS_PER_STEP = 8


IMAGES_PER_STEP = 2


def _na_kernel(rows_ref, q_ref, k_ref, v_ref, kc_ref, vc_ref, o_ref,
               bias_ref, vaug, s_scr):
    @pl.when(pl.program_id(1) == 0)
    def _():
        _build_bias_pairs(rows_ref, bias_ref)

    for img in range(IMAGES_PER_STEP):
        vaug[img, :SEQ, :HEAD_PAIR] = v_ref[img * SEQ:(img + 1) * SEQ, :]
        vaug[img, SEQ:, :HEAD_PAIR] = vc_ref[img * CTX_LEN:(img + 1) * CTX_LEN, :]
        vaug[img, :, HEAD_PAIR:] = jnp.ones((SEQ + CTX_LEN, HEAD_PAIR), BF16)

    first = _head_masks(GRID_W)
    steps_per_image = GRID_H // ROWS_PER_STEP

    def locate(i, j):
        img = i // steps_per_image
        r = (i % steps_per_image) * ROWS_PER_STEP + j
        return img, r, jnp.clip(r - NA_ROWS // 2, 0, GRID_H - NA_ROWS)

    def rows(row, n):
        return pl.ds(pl.multiple_of(row * GRID_W, GRID_W), n)

    def qk_stage(i):
        for j in range(ROWS_PER_STEP):
            img, r, start = locate(i, j)
            dr0 = start - r + (NA_ROWS - 1)
            q_r = q_ref[rows(img * GRID_H + r, GRID_W), :]
            zero = jnp.zeros_like(q_r)
            qs = jnp.concatenate([jnp.where(first, q_r, zero), jnp.where(first, zero, q_r)], axis=0)
            bias = jnp.concatenate([bias_ref[dr0 + 2 * t] for t in range(NA_ROWS // 2)], axis=1)
            k_w = k_ref[rows(img * GRID_H + start, NB_KEYS), :]
            kc = kc_ref[pl.ds(pl.multiple_of(img * CTX_LEN, CTX_LEN), CTX_LEN), :]
            s_scr[i % 2, j, :, :NB_KEYS] = lax.dot_general(
                qs, k_w, _NT_DIMS, preferred_element_type=F32) + bias
            s_scr[i % 2, j, :, NB_KEYS:] = lax.dot_general(qs, kc, _NT_DIMS, preferred_element_type=F32)

    def softmax_pv_stage(i):
        for j in range(ROWS_PER_STEP):
            img, r, start = locate(i, j)
            s = s_scr[i % 2, j]
            p = jnp.exp2(s - jnp.max(s, axis=-1, keepdims=True)).astype(BF16)
            oa = (jnp.dot(p[:, :NB_KEYS], vaug[img, rows(start, NB_KEYS), :], preferred_element_type=F32)
                  + jnp.dot(p[:, NB_KEYS:], vaug[img, SEQ:, :], preferred_element_type=F32))
            o2 = oa[:, :HEAD_PAIR] / oa[:, HEAD_PAIR:]
            o_ref[rows(img * GRID_H + r, GRID_W), :] = (
                jnp.where(first, o2[:GRID_W], o2[GRID_W:]).astype(BF16))

    n_steps = IMAGES_PER_STEP * steps_per_image
    qk_stage(0)

    def step(i, carry):
        softmax_pv_stage(i - 1)
        qk_stage(i)
        return carry

    lax.fori_loop(1, n_steps, step, 0)
    softmax_pv_stage(n_steps - 1)


def _na_attention(q, k, v, rpb):
    lat_rows = IMAGES_PER_STEP * SEQ
    ctx_rows = IMAGES_PER_STEP * CTX_LEN
    lat_spec = pl.BlockSpec((lat_rows, HEAD_PAIR), lambda hp, g: (g, hp))
    ctx_spec = pl.BlockSpec((ctx_rows, HEAD_PAIR), lambda hp, g: (N_LAT // ctx_rows + g, hp))
    return pl.pallas_call(
        _na_kernel,
        grid=(N_HEAD_PAIRS, BATCH // IMAGES_PER_STEP),
        in_specs=[pl.BlockSpec((2, N_DR_PAIRS, 1, HEAD_PAIR), lambda hp, g: (hp, 0, 0, 0)),
                  lat_spec, lat_spec, lat_spec, ctx_spec, ctx_spec],
        out_specs=lat_spec,
        out_shape=jax.ShapeDtypeStruct((N_TOT, D_MODEL), BF16),
        scratch_shapes=[
            pltpu.VMEM((N_DR_PAIRS, HEAD_PAIR, HEAD_PAIR), F32),
            pltpu.VMEM((IMAGES_PER_STEP, SEQ + CTX_LEN, 2 * HEAD_PAIR), BF16),
            pltpu.VMEM((2, ROWS_PER_STEP, HEAD_PAIR, N_KEYS), F32),
        ],
        compiler_params=pltpu.CompilerParams(
            dimension_semantics=("arbitrary", "arbitrary"), vmem_limit_bytes=48 << 20),
        name="na_attention",
    )(_bias_rows(rpb), q, k, v, k, v)


def _ctx_attn_kernel(q_ref, k_ref, v_ref, o_in_ref, o_ref):
    del o_in_ref
    first = _head_masks(CTX_LEN)
    q = q_ref[...]
    k = k_ref[...]
    v = v_ref[...]
    outs = []
    for hh in range(2):
        sel = first if hh == 0 else jnp.logical_not(first)
        qm = jnp.where(sel, q, jnp.zeros_like(q))
        s = lax.dot_general(qm, k, _NT_DIMS, preferred_element_type=F32)
        p = jnp.exp2(s - jnp.max(s, axis=-1, keepdims=True))
        denom = jnp.sum(p, axis=-1, keepdims=True)
        outs.append(jnp.dot(p.astype(BF16), v, preferred_element_type=F32) / denom)
    o_ref[...] = jnp.where(first, outs[0], outs[1]).astype(BF16)


def _ctx_attention(q, k, v, o):
    ctx_spec = pl.BlockSpec((CTX_LEN, HEAD_PAIR), lambda b, hp: (N_LAT // CTX_LEN + b, hp))
    return pl.pallas_call(
        _ctx_attn_kernel,
        grid=(BATCH, N_HEAD_PAIRS),
        in_specs=[ctx_spec, ctx_spec, ctx_spec, pl.BlockSpec(memory_space=pl.ANY)],
        out_specs=ctx_spec,
        out_shape=jax.ShapeDtypeStruct((N_TOT, D_MODEL), BF16),
        input_output_aliases={3: 0},
        compiler_params=pltpu.CompilerParams(dimension_semantics=("arbitrary", "arbitrary")),
        name="ctx_attention",
    )(q, k, v, o)


def kernel(x, c, ctx, c_ctx, w_mod, b_mod, norm_mix, norm_ffn, w_in_ab, ln_v, w_spatial,
           b_spatial, w_pool, pool_scale, w_out_ab, w_qkv, rpb, w_out_na, w_ffn_in,
           w_ffn_out, norm_final):
    cond = jnp.zeros((MOD_ROWS, D_MODEL), F32).at[:BATCH].set(c).at[CTX_MOD_ROW].set(c_ctx)
    mod3 = _adaln(cond, w_mod, b_mod).reshape(DEPTH * MOD_ROWS * N_MOD, 1, D_MODEL)

    x_lat, x_ctx = x.reshape(N_LAT, D_MODEL), ctx.reshape(N_CTX, D_MODEL)
    nf = norm_final.reshape(1, D_MODEL)
    w_in_ab, w_out_ab, w_qkv, w_out_na, w_ffn_in, w_ffn_out = (
        w.astype(BF16) for w in (w_in_ab, w_out_ab, w_qkv, w_out_na, w_ffn_in, w_ffn_out))
    for i in range(DEPTH):
        last = i == DEPTH - 1
        j = i // 2
        nm = norm_mix[i].reshape(1, D_MODEL)
        if i % 2 == 1:
            q, k, v = _qkv(x_lat, mod3, i, nm, w_qkv, j)
            y = _na_attention(q, k, v, rpb[j])
            if not last:
                y = _ctx_attention(q, k, v, y)
            w_o = w_out_na
        else:
            b_s = jnp.broadcast_to(b_spatial[j][:, :, None], (A_GROUPS, CHUNK, GROUP_DIM))
            y = _ab_mixer(x_lat, x_ctx, mod3, i, nm, w_in_ab, j, ln_v[j].reshape(1, D_A),
                          w_spatial[j].astype(BF16), b_s, w_pool[j].astype(BF16),
                          pool_scale[j].reshape(1, D_B))
            w_o = w_out_ab
        x_lat = x_ctx = _ffn(x_lat, x_ctx, y, mod3, i, norm_ffn[i].reshape(1, D_MODEL), w_o, j,
                             w_ffn_in, w_ffn_out, nf, n_rows=N_LAT if last else N_TOT, final=last)
    return x_lat.reshape(BATCH, SEQ, D_MODEL)
```

```python
import functools
import math

import jax
import jax.numpy as jnp
from jax import lax
from jax.experimental import pallas as pl
from jax.experimental.pallas import tpu as pltpu

D_MODEL = 1024
BATCH = 4
SEQ = 4096
DEPTH = 4
GRID_W = 64
GRID_H = SEQ // GRID_W
CTX_LEN = 256
D_A = D_MODEL // 2
A_GROUPS = 4
CHUNK = 128
D_B = D_MODEL - D_A
POOL_WINDOWS = (2, 4, 8, 16)
GROUP_DIM = 128
N_HEADS = 16
HEAD_DIM = 64
NA_ROWS = 8
NA_COLS = 16
D_FF = 2816
EPS = 1e-6
LOG2E = 1.4426950408889634
NEG_INF = -1e30

N_LAT = BATCH * SEQ
N_CTX = BATCH * CTX_LEN
N_TOT = N_LAT + N_CTX
MOD_ROWS = 8
CTX_MOD_ROW = BATCH
N_MOD = 6

TM_FFN = 512
TM_AB = 256
HALO = 8
FF_CHUNK = 256
MOD_TN = 1536
HEAD_PAIR = 2 * HEAD_DIM
N_HEAD_PAIRS = N_HEADS // 2
NB_KEYS = NA_ROWS * GRID_W

F32 = jnp.float32
BF16 = jnp.bfloat16


def _const_spec(shape):
    nd = len(shape)
    return pl.BlockSpec(shape, lambda *_: (0,) * nd, pipeline_mode=pl.Buffered(1))


def _layer_spec(shape, layer):
    nd = len(shape)
    return pl.BlockSpec((None,) + tuple(shape), lambda *_: (layer,) + (0,) * nd,
                        pipeline_mode=pl.Buffered(1))


def _mod_spec(layer, k, tm):
    n_lat_tiles = N_LAT // tm
    tiles_per_batch = SEQ // tm

    def index_map(t, *_):
        row = jnp.where(t < n_lat_tiles, t // tiles_per_batch, CTX_MOD_ROW)
        return ((layer * MOD_ROWS + row) * N_MOD + k, 0, 0)

    return pl.BlockSpec((1, 1, D_MODEL), index_map)


def _stream_specs(tm, split):
    if not split:
        return (pl.BlockSpec((tm, D_MODEL), lambda t: (t, 0)),
                pl.BlockSpec((HALO, D_MODEL), lambda t: (0, 0)))
    n_lat = N_LAT // tm
    return (pl.BlockSpec((tm, D_MODEL), lambda t: (jnp.minimum(t, n_lat - 1), 0)),
            pl.BlockSpec((tm, D_MODEL), lambda t: (jnp.maximum(t - n_lat, 0), 0)))


def _stream_tile(xl_ref, xc_ref, tm, split):
    if not split:
        return xl_ref[...]
    return jnp.where(pl.program_id(0) >= N_LAT // tm, xc_ref[...], xl_ref[...])


def _rms(x, g):
    return x * lax.rsqrt(jnp.mean(x * x, axis=-1, keepdims=True) + EPS) * g


def _adaln_kernel(cond_ref, w_ref, b_ref, o_ref):
    c = cond_ref[...]
    s = c * jax.nn.sigmoid(c)
    w = w_ref[0]
    s_hi = s.astype(BF16)
    s_lo = (s - s_hi.astype(F32)).astype(BF16)
    w_hi = w.astype(BF16)
    w_lo = (w - w_hi.astype(F32)).astype(BF16)
    s_parts = jnp.concatenate([s_hi, s_lo], axis=0)
    by_hi = jnp.dot(s_parts, w_hi, preferred_element_type=F32)
    by_lo = jnp.dot(s_parts, w_lo, preferred_element_type=F32)
    o_ref[0] = (by_hi[:MOD_ROWS] + by_hi[MOD_ROWS:]) + (by_lo[:MOD_ROWS] + by_lo[MOD_ROWS:]) + b_ref[0]


def _adaln(cond, w_mod, b_mod):
    return pl.pallas_call(
        _adaln_kernel,
        grid=(DEPTH, N_MOD * D_MODEL // MOD_TN),
        in_specs=[
            pl.BlockSpec((MOD_ROWS, D_MODEL), lambda l, n: (0, 0)),
            pl.BlockSpec((1, D_MODEL, MOD_TN), lambda l, n: (l, 0, n)),
            pl.BlockSpec((1, 1, MOD_TN), lambda l, n: (l, 0, n)),
        ],
        out_specs=pl.BlockSpec((1, MOD_ROWS, MOD_TN), lambda l, n: (l, 0, n)),
        out_shape=jax.ShapeDtypeStruct((DEPTH, MOD_ROWS, N_MOD * D_MODEL), F32),
        compiler_params=pltpu.CompilerParams(
            dimension_semantics=("arbitrary", "arbitrary"), vmem_limit_bytes=32 << 20),
        name="adaln",
    )(cond, w_mod, b_mod.reshape(DEPTH, 1, N_MOD * D_MODEL))


def _ffn_kernel(xl_ref, xc_ref, y_ref, g1_ref, sh_ref, sc_ref, g2_ref, nrm_ref, wo_ref, wi_ref,
                wout_ref, nf_ref, o_ref, hmid_ref, *, split, final):
    x1 = _stream_tile(xl_ref, xc_ref, TM_FFN, split) + g1_ref[0] * jnp.dot(
        y_ref[...], wo_ref[...], preferred_element_type=F32)
    h = _rms(x1, nrm_ref[...]) * (1.0 + sc_ref[0]) + sh_ref[0]
    hb = h.astype(BF16)
    for c in range(D_FF // FF_CHUNK):
        lo = c * FF_CHUNK
        a = jnp.dot(hb, wi_ref[:, lo:lo + FF_CHUNK], preferred_element_type=F32)
        g = jnp.dot(hb, wi_ref[:, D_FF + lo:D_FF + lo + FF_CHUNK], preferred_element_type=F32)
        hmid_ref[:, lo:lo + FF_CHUNK] = (a * jax.nn.sigmoid(a) * g).astype(BF16)
    out = x1 + g2_ref[0] * jnp.dot(hmid_ref[...], wout_ref[...], preferred_element_type=F32)
    if final:
        out = _rms(out, nf_ref[...])
    o_ref[...] = out


def _ffn(x_lat, x_ctx, y, mod3, layer, norm_ffn, w_o, mixer_layer, w_in, w_out, norm_final, *,
         n_rows, final):
    tm = TM_FFN
    row_spec = pl.BlockSpec((tm, D_MODEL), lambda t: (t, 0))
    split = x_ctx is not x_lat
    lat_spec, ctx_spec = _stream_specs(tm, split)
    return pl.pallas_call(
        functools.partial(_ffn_kernel, split=split, final=final),
        grid=(n_rows // tm,),
        in_specs=[
            lat_spec, ctx_spec, row_spec,
            _mod_spec(layer, 2, tm), _mod_spec(layer, 3, tm), _mod_spec(layer, 4, tm),
            _mod_spec(layer, 5, tm),
            _const_spec((1, D_MODEL)),
            _layer_spec((D_MODEL, D_MODEL), mixer_layer),
            _layer_spec((D_MODEL, 2 * D_FF), layer),
            _layer_spec((D_FF, D_MODEL), layer),
            _const_spec((1, D_MODEL)),
        ],
        out_specs=row_spec,
        out_shape=jax.ShapeDtypeStruct((n_rows, D_MODEL), F32),
        scratch_shapes=[pltpu.VMEM((tm, D_FF), BF16)],
        compiler_params=pltpu.CompilerParams(
            dimension_semantics=("arbitrary",), vmem_limit_bytes=52 << 20),
        name="outproj_ffn",
    )(x_lat, x_ctx, y, mod3, mod3, mod3, mod3, norm_ffn, w_o, w_in, w_out, norm_final)


def _gelu_tanh(x):
    k1 = -2.0 * math.sqrt(2.0 / math.pi) * LOG2E
    return x / (1.0 + jnp.exp2(x * (k1 + (k1 * 0.044715) * (x * x))))


def _ab_kernel(xl_ref, xc_ref, xp_ref, xn_ref, sh_ref, sc_ref, nrm_ref, win_ref, lnv_ref, ws_ref,
               bs_ref, wp_ref, ps_ref, y_ref, p_ref, a2_ref, a4_ref, *, split):
    tm = TM_AB
    t = pl.program_id(0)
    n_lat_tiles = N_LAT // tm
    is_ctx = t >= n_lat_tiles
    seq_len = jnp.where(is_ctx, CTX_LEN, SEQ)
    pos0 = jnp.where(is_ctx, 0, (t % (SEQ // tm)) * tm)

    x_all = jnp.concatenate([xp_ref[...], _stream_tile(xl_ref, xc_ref, tm, split), xn_ref[...]], axis=0)
    h = _rms(x_all, nrm_ref[...]) * (1.0 + sc_ref[0]) + sh_ref[0]
    z = jnp.dot(h.astype(BF16), win_ref[...], preferred_element_type=F32)

    n_p = tm + 2 * HALO
    pos_all = pos0 - HALO + lax.broadcasted_iota(jnp.int32, (n_p, D_B), 0)
    p_ref[:n_p, :] = jnp.where((pos_all >= 0) & (pos_all < seq_len), z[:, 2 * D_A:], 0.0)
    p_ref[n_p:, :] = jnp.zeros((2 * HALO, D_B), F32)

    za = _gelu_tanh(z[HALO:HALO + tm, :2 * D_A])
    u = za[:, :D_A]
    vv = za[:, D_A:]
    vc = vv - jnp.mean(vv, axis=-1, keepdims=True)
    v = (vc * lax.rsqrt(jnp.mean(vc * vc, axis=-1, keepdims=True) + EPS) * lnv_ref[...]).astype(BF16)

    n_blocks = tm // CHUNK
    for g in range(A_GROUPS):
        cs = slice(g * GROUP_DIM, (g + 1) * GROUP_DIM)
        v_cat = jnp.concatenate([v[n * CHUNK:(n + 1) * CHUNK, cs] for n in range(n_blocks)], axis=1)
        mixed = jnp.dot(ws_ref[g], v_cat, preferred_element_type=F32)
        for n in range(n_blocks):
            rs = slice(n * CHUNK, (n + 1) * CHUNK)
            y_ref[rs, cs] = (u[rs, cs] * (mixed[:, n * CHUNK:(n + 1) * CHUNK] + bs_ref[g])).astype(BF16)

    a2_ref[...] = p_ref[0:n_p + HALO, GROUP_DIM:] + p_ref[1:n_p + HALO + 1, GROUP_DIM:]
    a4_ref[...] = a2_ref[0:n_p, :] + a2_ref[2:n_p + 2, :]
    a8 = a4_ref[0:tm + HALO, 2 * GROUP_DIM:] + a4_ref[4:tm + HALO + 4, 2 * GROUP_DIM:]
    segs = (
        p_ref[HALO - 1:HALO - 1 + tm, :GROUP_DIM] + p_ref[HALO:HALO + tm, :GROUP_DIM],
        a4_ref[HALO - 2:HALO - 2 + tm, :GROUP_DIM],
        a4_ref[HALO - 4:HALO - 4 + tm, GROUP_DIM:2 * GROUP_DIM] + a4_ref[HALO:HALO + tm, GROUP_DIM:2 * GROUP_DIM],
        a8[0:tm] + a8[HALO:HALO + tm],
    )

    def window_count(first_row, half):
        pos = pos0 + first_row + lax.broadcasted_iota(jnp.int32, (HALO, GROUP_DIM), 0)
        return (jnp.minimum(pos + half, seq_len) - jnp.maximum(pos - half, 0)).astype(F32)

    for g, w in enumerate(POOL_WINDOWS):
        half = w // 2
        cs = slice(g * GROUP_DIM, (g + 1) * GROUP_DIM)
        seg = segs[g]
        pooled = jnp.concatenate([seg[:HALO] / window_count(0, half),
                                  seg[HALO:tm - HALO] * (1.0 / w),
                                  seg[tm - HALO:] / window_count(tm - HALO, half)], axis=0)
        diff = pooled - p_ref[HALO:HALO + tm, cs]
        yb = jnp.dot(diff.astype(BF16), wp_ref[g], preferred_element_type=F32)
        y_ref[:, D_A + g * GROUP_DIM:D_A + (g + 1) * GROUP_DIM] = (yb * ps_ref[:, cs]).astype(BF16)


def _ab_mixer(x_lat, x_ctx, mod3, layer, norm_mix, w_in, mixer_layer, ln_v, w_s, b_s, w_pool,
              pool_scale):
    tm = TM_AB
    sub = tm // HALO
    last_halo_block = x_lat.shape[0] // HALO - 1
    split = x_ctx is not x_lat
    lat_spec, ctx_spec = _stream_specs(tm, split)
    return pl.pallas_call(
        functools.partial(_ab_kernel, split=split),
        grid=(N_TOT // tm,),
        in_specs=[
            lat_spec, ctx_spec,
            pl.BlockSpec((HALO, D_MODEL), lambda t: (jnp.clip(t * sub - 1, 0, last_halo_block), 0)),
            pl.BlockSpec((HALO, D_MODEL), lambda t: (jnp.minimum((t + 1) * sub, last_halo_block), 0)),
            _mod_spec(layer, 0, tm), _mod_spec(layer, 1, tm),
            _const_spec((1, D_MODEL)),
            _layer_spec((D_MODEL, 2 * D_A + D_B), mixer_layer),
            _const_spec((1, D_A)),
            _const_spec((A_GROUPS, CHUNK, CHUNK)),
            _const_spec((A_GROUPS, CHUNK, GROUP_DIM)),
            _const_spec((A_GROUPS, GROUP_DIM, GROUP_DIM)),
            _const_spec((1, D_B)),
        ],
        out_specs=pl.BlockSpec((tm, D_MODEL), lambda t: (t, 0)),
        out_shape=jax.ShapeDtypeStruct((N_TOT, D_MODEL), BF16),
        scratch_shapes=[pltpu.VMEM((tm + 4 * HALO, D_B), F32),
                        pltpu.VMEM((tm + 3 * HALO, D_B - GROUP_DIM), F32),
                        pltpu.VMEM((tm + 2 * HALO, D_B - GROUP_DIM), F32)],
        compiler_params=pltpu.CompilerParams(
            dimension_semantics=("arbitrary",), vmem_limit_bytes=40 << 20),
        name="ab_mixer",
    )(x_lat, x_ctx, x_lat, x_lat, mod3, mod3, norm_mix, w_in, ln_v, w_s, b_s, w_pool, pool_scale)


def _qkv_kernel(x_ref, sh_ref, sc_ref, nrm_ref, w_ref, q_ref, k_ref, v_ref):
    h = _rms(x_ref[...], nrm_ref[...]) * (1.0 + sc_ref[0]) + sh_ref[0]
    hb = h.astype(BF16)
    scale = HEAD_DIM ** -0.5 * LOG2E
    q_ref[...] = (jnp.dot(hb, w_ref[:, :D_MODEL], preferred_element_type=F32) * scale).astype(BF16)
    k_ref[...] = jnp.dot(hb, w_ref[:, D_MODEL:2 * D_MODEL], preferred_element_type=F32).astype(BF16)
    v_ref[...] = jnp.dot(hb, w_ref[:, 2 * D_MODEL:], preferred_element_type=F32).astype(BF16)


def _qkv(xs, mod3, layer, norm_mix, w_qkv, mixer_layer):
    tm = TM_FFN
    row_spec = pl.BlockSpec((tm, D_MODEL), lambda t: (t, 0))
    out = jax.ShapeDtypeStruct((N_TOT, D_MODEL), BF16)
    return pl.pallas_call(
        _qkv_kernel,
        grid=(N_TOT // tm,),
        in_specs=[row_spec, _mod_spec(layer, 0, tm), _mod_spec(layer, 1, tm),
                  _const_spec((1, D_MODEL)), _layer_spec((D_MODEL, 3 * D_MODEL), mixer_layer)],
        out_specs=[row_spec, row_spec, row_spec],
        out_shape=[out, out, out],
        compiler_params=pltpu.CompilerParams(
            dimension_semantics=("arbitrary",), vmem_limit_bytes=40 << 20),
        name="qkv_proj",
    )(xs, mod3, mod3, norm_mix, w_qkv)


_NT_DIMS = (((1,), (1,)), ((), ()))


def _head_masks(rows):
    lane = lax.broadcasted_iota(jnp.int32, (rows, HEAD_PAIR), 1)
    return lane < HEAD_DIM


N_DR_PAIRS = 2 * NA_ROWS - 2


def _bias_rows(rpb):
    lo, hi = rpb[:, :-1], rpb[:, 1:]
    gap = jnp.zeros((N_HEADS, N_DR_PAIRS, GRID_W - 2 * NA_COLS + 1), F32)
    rows = jnp.concatenate([lo[..., NA_COLS - 1:], gap, hi, gap, lo[..., :NA_COLS - 1]], axis=-1)
    return rows[:, :, None, :]


def _build_bias_pairs(rows_ref, bias_ref):
    shape = (GRID_W, HEAD_PAIR)
    c = lax.broadcasted_iota(jnp.int32, shape, 0)
    kc = lax.broadcasted_iota(jnp.int32, shape, 1) % GRID_W
    col_start = jnp.clip(c - NA_COLS // 2, 0, GRID_W - NA_COLS)
    live = (kc >= col_start) & (kc < col_start + NA_COLS)
    for d in range(N_DR_PAIRS):
        for hh in range(2):
            t = pltpu.roll(jnp.broadcast_to(rows_ref[hh, d], shape), 0, 1, stride=1, stride_axis=0)
            bias_ref[d, hh * GRID_W:(hh + 1) * GRID_W, :] = jnp.where(live, t, NEG_INF) * LOG2E


N_KEYS = NB_KEYS + CTX_LEN
ROWS_PER_STEP = 16


IMAGES_PER_STEP = 2


def _na_kernel(rows_ref, q_ref, k_ref, v_ref, kc_ref, vc_ref, o_ref,
               bias_ref, vaug, s_scr):
    @pl.when(pl.program_id(1) == 0)
    def _():
        _build_bias_pairs(rows_ref, bias_ref)

    for img in range(IMAGES_PER_STEP):
        vaug[img, :SEQ, :HEAD_PAIR] = v_ref[img * SEQ:(img + 1) * SEQ, :]
        vaug[img, SEQ:, :HEAD_PAIR] = vc_ref[img * CTX_LEN:(img + 1) * CTX_LEN, :]
        vaug[img, :, HEAD_PAIR:] = jnp.ones((SEQ + CTX_LEN, HEAD_PAIR), BF16)

    first = _head_masks(GRID_W)
    steps_per_image = GRID_H // ROWS_PER_STEP

    def locate(i, j):
        img = i // steps_per_image
        r = (i % steps_per_image) * ROWS_PER_STEP + j
        return img, r, jnp.clip(r - NA_ROWS // 2, 0, GRID_H - NA_ROWS)

    def rows(row, n):
        return pl.ds(pl.multiple_of(row * GRID_W, GRID_W), n)

    def qk_stage(i):
        for j in range(ROWS_PER_STEP):
            img, r, start = locate(i, j)
            dr0 = start - r + (NA_ROWS - 1)
            q_r = q_ref[rows(img * GRID_H + r, GRID_W), :]
            zero = jnp.zeros_like(q_r)
            qs = jnp.concatenate([jnp.where(first, q_r, zero), jnp.where(first, zero, q_r)], axis=0)
            bias = jnp.concatenate([bias_ref[dr0 + 2 * t] for t in range(NA_ROWS // 2)], axis=1)
            k_w = k_ref[rows(img * GRID_H + start, NB_KEYS), :]
            kc = kc_ref[pl.ds(pl.multiple_of(img * CTX_LEN, CTX_LEN), CTX_LEN), :]
            s_scr[i % 2, j, :, :NB_KEYS] = lax.dot_general(
                qs, k_w, _NT_DIMS, preferred_element_type=F32) + bias
            s_scr[i % 2, j, :, NB_KEYS:] = lax.dot_general(qs, kc, _NT_DIMS, preferred_element_type=F32)

    def softmax_pv_stage(i):
        for j in range(ROWS_PER_STEP):
            img, r, start = locate(i, j)
            s = s_scr[i % 2, j]
            p = jnp.exp2(s - jnp.max(s, axis=-1, keepdims=True)).astype(BF16)
            oa = (jnp.dot(p[:, :NB_KEYS], vaug[img, rows(start, NB_KEYS), :], preferred_element_type=F32)
                  + jnp.dot(p[:, NB_KEYS:], vaug[img, SEQ:, :], preferred_element_type=F32))
            o2 = oa[:, :HEAD_PAIR] / oa[:, HEAD_PAIR:]
            o_ref[rows(img * GRID_H + r, GRID_W), :] = (
                jnp.where(first, o2[:GRID_W], o2[GRID_W:]).astype(BF16))

    n_steps = IMAGES_PER_STEP * steps_per_image
    qk_stage(0)

    def step(i, carry):
        softmax_pv_stage(i - 1)
        qk_stage(i)
        return carry

    lax.fori_loop(1, n_steps, step, 0)
    softmax_pv_stage(n_steps - 1)


def _na_attention(q, k, v, rpb):
    lat_rows = IMAGES_PER_STEP * SEQ
    ctx_rows = IMAGES_PER_STEP * CTX_LEN
    lat_spec = pl.BlockSpec((lat_rows, HEAD_PAIR), lambda hp, g: (g, hp))
    ctx_spec = pl.BlockSpec((ctx_rows, HEAD_PAIR), lambda hp, g: (N_LAT // ctx_rows + g, hp))
    return pl.pallas_call(
        _na_kernel,
        grid=(N_HEAD_PAIRS, BATCH // IMAGES_PER_STEP),
        in_specs=[pl.BlockSpec((2, N_DR_PAIRS, 1, HEAD_PAIR), lambda hp, g: (hp, 0, 0, 0)),
                  lat_spec, lat_spec, lat_spec, ctx_spec, ctx_spec],
        out_specs=lat_spec,
        out_shape=jax.ShapeDtypeStruct((N_TOT, D_MODEL), BF16),
        scratch_shapes=[
            pltpu.VMEM((N_DR_PAIRS, HEAD_PAIR, HEAD_PAIR), F32),
            pltpu.VMEM((IMAGES_PER_STEP, SEQ + CTX_LEN, 2 * HEAD_PAIR), BF16),
            pltpu.VMEM((2, ROWS_PER_STEP, HEAD_PAIR, N_KEYS), F32),
        ],
        compiler_params=pltpu.CompilerParams(
            dimension_semantics=("arbitrary", "arbitrary"), vmem_limit_bytes=48 << 20),
        name="na_attention",
    )(_bias_rows(rpb), q, k, v, k, v)


def _ctx_attn_kernel(q_ref, k_ref, v_ref, o_in_ref, o_ref):
    del o_in_ref
    first = _head_masks(CTX_LEN)
    for hp in range(N_HEAD_PAIRS):
        cols = slice(hp * HEAD_PAIR, (hp + 1) * HEAD_PAIR)
        q = q_ref[:, cols]
        k = k_ref[:, cols]
        v = v_ref[:, cols]
        outs = []
        for hh in range(2):
            sel = first if hh == 0 else jnp.logical_not(first)
            qm = jnp.where(sel, q, jnp.zeros_like(q))
            s = lax.dot_general(qm, k, _NT_DIMS, preferred_element_type=F32)
            p = jnp.exp2(s - jnp.max(s, axis=-1, keepdims=True))
            denom = jnp.sum(p, axis=-1, keepdims=True)
            outs.append(jnp.dot(p.astype(BF16), v, preferred_element_type=F32) / denom)
        o_ref[:, cols] = jnp.where(first, outs[0], outs[1]).astype(BF16)


def _ctx_attention(q, k, v, o):
    ctx_spec = pl.BlockSpec((CTX_LEN, D_MODEL), lambda b: (N_LAT // CTX_LEN + b, 0))
    return pl.pallas_call(
        _ctx_attn_kernel,
        grid=(BATCH,),
        in_specs=[ctx_spec, ctx_spec, ctx_spec, pl.BlockSpec(memory_space=pl.ANY)],
        out_specs=ctx_spec,
        out_shape=jax.ShapeDtypeStruct((N_TOT, D_MODEL), BF16),
        input_output_aliases={3: 0},
        compiler_params=pltpu.CompilerParams(dimension_semantics=("arbitrary",)),
        name="ctx_attention",
    )(q, k, v, o)


def kernel(x, c, ctx, c_ctx, w_mod, b_mod, norm_mix, norm_ffn, w_in_ab, ln_v, w_spatial,
           b_spatial, w_pool, pool_scale, w_out_ab, w_qkv, rpb, w_out_na, w_ffn_in,
           w_ffn_out, norm_final):
    cond = jnp.zeros((MOD_ROWS, D_MODEL), F32).at[:BATCH].set(c).at[CTX_MOD_ROW].set(c_ctx)
    mod3 = _adaln(cond, w_mod, b_mod).reshape(DEPTH * MOD_ROWS * N_MOD, 1, D_MODEL)

    x_lat, x_ctx = x.reshape(N_LAT, D_MODEL), ctx.reshape(N_CTX, D_MODEL)
    nf = norm_final.reshape(1, D_MODEL)
    w_in_ab, w_out_ab, w_qkv, w_out_na, w_ffn_in, w_ffn_out = (
        w.astype(BF16) for w in (w_in_ab, w_out_ab, w_qkv, w_out_na, w_ffn_in, w_ffn_out))
    for i in range(DEPTH):
        last = i == DEPTH - 1
        j = i // 2
        nm = norm_mix[i].reshape(1, D_MODEL)
        if i % 2 == 1:
            q, k, v = _qkv(x_lat, mod3, i, nm, w_qkv, j)
            y = _na_attention(q, k, v, rpb[j])
            if not last:
                y = _ctx_attention(q, k, v, y)
            w_o = w_out_na
        else:
            b_s = jnp.broadcast_to(b_spatial[j][:, :, None], (A_GROUPS, CHUNK, GROUP_DIM))
            y = _ab_mixer(x_lat, x_ctx, mod3, i, nm, w_in_ab, j, ln_v[j].reshape(1, D_A),
                          w_spatial[j].astype(BF16), b_s, w_pool[j].astype(BF16),
                          pool_scale[j].reshape(1, D_B))
            w_o = w_out_ab
        x_lat = x_ctx = _ffn(x_lat, x_ctx, y, mod3, i, norm_ffn[i].reshape(1, D_MODEL), w_o, j,
                             w_ffn_in, w_ffn_out, nf, n_rows=N_LAT if last else N_TOT, final=last)
    return x_lat.reshape(BATCH, SEQ, D_MODEL)
```

```python
import functools
import math

import jax
import jax.numpy as jnp
from jax import lax
from jax.experimental import pallas as pl
from jax.experimental.pallas import tpu as pltpu

D_MODEL = 1024
BATCH = 4
SEQ = 4096
DEPTH = 4
GRID_W = 64
GRID_H = SEQ // GRID_W
CTX_LEN = 256
D_A = D_MODEL // 2
A_GROUPS = 4
CHUNK = 128
D_B = D_MODEL - D_A
POOL_WINDOWS = (2, 4, 8, 16)
GROUP_DIM = 128
N_HEADS = 16
HEAD_DIM = 64
NA_ROWS = 8
NA_COLS = 16
D_FF = 2816
EPS = 1e-6
LOG2E = 1.4426950408889634
NEG_INF = -1e30

N_LAT = BATCH * SEQ
N_CTX = BATCH * CTX_LEN
N_TOT = N_LAT + N_CTX
MOD_ROWS = 8
CTX_MOD_ROW = BATCH
N_MOD = 6

TM_FFN = 512
TM_AB = 256
HALO = 8
FF_CHUNK = 256
FFN_CAST_STEPS = 32
MOD_TN = 1536
HEAD_PAIR = 2 * HEAD_DIM
N_HEAD_PAIRS = N_HEADS // 2
NB_KEYS = NA_ROWS * GRID_W

F32 = jnp.float32
BF16 = jnp.bfloat16


def _const_spec(shape):
    nd = len(shape)
    return pl.BlockSpec(shape, lambda *_: (0,) * nd, pipeline_mode=pl.Buffered(1))


def _layer_spec(shape, layer):
    nd = len(shape)
    return pl.BlockSpec((None,) + tuple(shape), lambda *_: (layer,) + (0,) * nd,
                        pipeline_mode=pl.Buffered(1))


def _mod_spec(layer, k, tm):
    n_lat_tiles = N_LAT // tm
    tiles_per_batch = SEQ // tm

    def index_map(t, *_):
        row = jnp.where(t < n_lat_tiles, t // tiles_per_batch, CTX_MOD_ROW)
        return ((layer * MOD_ROWS + row) * N_MOD + k, 0, 0)

    return pl.BlockSpec((1, 1, D_MODEL), index_map)


def _stream_specs(tm, split):
    if not split:
        return (pl.BlockSpec((tm, D_MODEL), lambda t: (t, 0)),
                pl.BlockSpec((HALO, D_MODEL), lambda t: (0, 0)))
    n_lat = N_LAT // tm
    return (pl.BlockSpec((tm, D_MODEL), lambda t: (jnp.minimum(t, n_lat - 1), 0)),
            pl.BlockSpec((tm, D_MODEL), lambda t: (jnp.maximum(t - n_lat, 0), 0)))


def _stream_tile(xl_ref, xc_ref, tm, split):
    if not split:
        return xl_ref[...]
    return jnp.where(pl.program_id(0) >= N_LAT // tm, xc_ref[...], xl_ref[...])


def _rms(x, g):
    return x * lax.rsqrt(jnp.mean(x * x, axis=-1, keepdims=True) + EPS) * g


def _adaln_kernel(cond_ref, w_ref, b_ref, o_ref):
    c = cond_ref[...]
    s = c * jax.nn.sigmoid(c)
    w = w_ref[0]
    s_hi = s.astype(BF16)
    s_lo = (s - s_hi.astype(F32)).astype(BF16)
    w_hi = w.astype(BF16)
    w_lo = (w - w_hi.astype(F32)).astype(BF16)
    s_parts = jnp.concatenate([s_hi, s_lo], axis=0)
    by_hi = jnp.dot(s_parts, w_hi, preferred_element_type=F32)
    by_lo = jnp.dot(s_parts, w_lo, preferred_element_type=F32)
    o_ref[0] = (by_hi[:MOD_ROWS] + by_hi[MOD_ROWS:]) + (by_lo[:MOD_ROWS] + by_lo[MOD_ROWS:]) + b_ref[0]


def _adaln(cond, w_mod, b_mod):
    return pl.pallas_call(
        _adaln_kernel,
        grid=(DEPTH, N_MOD * D_MODEL // MOD_TN),
        in_specs=[
            pl.BlockSpec((MOD_ROWS, D_MODEL), lambda l, n: (0, 0)),
            pl.BlockSpec((1, D_MODEL, MOD_TN), lambda l, n: (l, 0, n)),
            pl.BlockSpec((1, 1, MOD_TN), lambda l, n: (l, 0, n)),
        ],
        out_specs=pl.BlockSpec((1, MOD_ROWS, MOD_TN), lambda l, n: (l, 0, n)),
        out_shape=jax.ShapeDtypeStruct((DEPTH, MOD_ROWS, N_MOD * D_MODEL), F32),
        compiler_params=pltpu.CompilerParams(
            dimension_semantics=("arbitrary", "arbitrary"), vmem_limit_bytes=32 << 20),
        name="adaln",
    )(cond, w_mod, b_mod.reshape(DEPTH, 1, N_MOD * D_MODEL))


def _ffn_kernel(*refs, split, final, cast_next):
    (xl_ref, xc_ref, y_ref, g1_ref, sh_ref, sc_ref, g2_ref, nrm_ref, wo_ref, wi_ref, wout_ref,
     nf_ref) = refs[:12]
    if cast_next:
        wi_next_ref, wout_next_ref, o_ref, wi_cast_ref, wout_cast_ref, hmid_ref = refs[12:]
        wi_cast_ref[...] = wi_next_ref[...].astype(BF16)
        wout_cast_ref[...] = wout_next_ref[...].astype(BF16)
    else:
        o_ref, hmid_ref = refs[12:]
    x1 = _stream_tile(xl_ref, xc_ref, TM_FFN, split) + g1_ref[0] * jnp.dot(
        y_ref[...], wo_ref[...], preferred_element_type=F32)
    h = _rms(x1, nrm_ref[...]) * (1.0 + sc_ref[0]) + sh_ref[0]
    hb = h.astype(BF16)
    for c in range(D_FF // FF_CHUNK):
        lo = c * FF_CHUNK
        a = jnp.dot(hb, wi_ref[:, lo:lo + FF_CHUNK], preferred_element_type=F32)
        g = jnp.dot(hb, wi_ref[:, D_FF + lo:D_FF + lo + FF_CHUNK], preferred_element_type=F32)
        hmid_ref[:, lo:lo + FF_CHUNK] = (a * jax.nn.sigmoid(a) * g).astype(BF16)
    out = x1 + g2_ref[0] * jnp.dot(hmid_ref[...], wout_ref[...], preferred_element_type=F32)
    if final:
        out = _rms(out, nf_ref[...])
    o_ref[...] = out


def _cast_chunk_spec(n_rows, n_cols, n_steps, layer):
    return pl.BlockSpec((None, n_rows // n_steps, n_cols),
                        lambda t: (layer, jnp.minimum(t, n_steps - 1), 0))


def _ffn(x_lat, x_ctx, y, mod3, layer, norm_ffn, w_o, mixer_layer, w_in, w_out, norm_final, *,
         n_rows, final, cast_next=None):
    tm = TM_FFN
    n_steps = n_rows // tm
    row_spec = pl.BlockSpec((tm, D_MODEL), lambda t: (t, 0))
    split = x_ctx is not x_lat
    lat_spec, ctx_spec = _stream_specs(tm, split)
    in_specs = [
        lat_spec, ctx_spec, row_spec,
        _mod_spec(layer, 2, tm), _mod_spec(layer, 3, tm), _mod_spec(layer, 4, tm),
        _mod_spec(layer, 5, tm),
        _const_spec((1, D_MODEL)),
        _layer_spec((D_MODEL, D_MODEL), mixer_layer),
        _layer_spec((D_MODEL, 2 * D_FF), w_in[1]),
        _layer_spec((D_FF, D_MODEL), w_out[1]),
        _const_spec((1, D_MODEL)),
    ]
    args = [x_lat, x_ctx, y, mod3, mod3, mod3, mod3, norm_ffn, w_o, w_in[0], w_out[0], norm_final]
    out_specs = [row_spec]
    out_shape = [jax.ShapeDtypeStruct((n_rows, D_MODEL), F32)]
    if cast_next:
        assert n_steps >= FFN_CAST_STEPS
        for w_next, (rows, cols), steps in zip(
                cast_next[:2], ((D_MODEL, 2 * D_FF), (D_FF, D_MODEL)),
                (FFN_CAST_STEPS, FFN_CAST_STEPS // 2)):
            in_specs.append(_cast_chunk_spec(rows, cols, steps, cast_next[2]))
            args.append(w_next)
            out_specs.append(_cast_chunk_spec(rows, cols, steps, 0))
            out_shape.append(jax.ShapeDtypeStruct((1, rows, cols), BF16))
    res = pl.pallas_call(
        functools.partial(_ffn_kernel, split=split, final=final, cast_next=bool(cast_next)),
        grid=(n_steps,),
        in_specs=in_specs,
        out_specs=out_specs,
        out_shape=out_shape,
        scratch_shapes=[pltpu.VMEM((tm, D_FF), BF16)],
        compiler_params=pltpu.CompilerParams(
            dimension_semantics=("arbitrary",), vmem_limit_bytes=56 << 20),
        name="outproj_ffn",
    )(*args)
    return res if cast_next else res[0]


def _gelu_tanh(x):
    k1 = -2.0 * math.sqrt(2.0 / math.pi) * LOG2E
    return x / (1.0 + jnp.exp2(x * (k1 + (k1 * 0.044715) * (x * x))))


def _ab_kernel(xl_ref, xc_ref, xp_ref, xn_ref, sh_ref, sc_ref, nrm_ref, win_ref, lnv_ref, ws_ref,
               bs_ref, wp_ref, ps_ref, y_ref, p_ref, a2_ref, a4_ref, *, split):
    tm = TM_AB
    t = pl.program_id(0)
    n_lat_tiles = N_LAT // tm
    is_ctx = t >= n_lat_tiles
    seq_len = jnp.where(is_ctx, CTX_LEN, SEQ)
    pos0 = jnp.where(is_ctx, 0, (t % (SEQ // tm)) * tm)

    x_all = jnp.concatenate([xp_ref[...], _stream_tile(xl_ref, xc_ref, tm, split), xn_ref[...]], axis=0)
    h = _rms(x_all, nrm_ref[...]) * (1.0 + sc_ref[0]) + sh_ref[0]
    z = jnp.dot(h.astype(BF16), win_ref[...], preferred_element_type=F32)

    n_p = tm + 2 * HALO
    pos_all = pos0 - HALO + lax.broadcasted_iota(jnp.int32, (n_p, D_B), 0)
    p_ref[:n_p, :] = jnp.where((pos_all >= 0) & (pos_all < seq_len), z[:, 2 * D_A:], 0.0)
    p_ref[n_p:, :] = jnp.zeros((2 * HALO, D_B), F32)

    za = _gelu_tanh(z[HALO:HALO + tm, :2 * D_A])
    u = za[:, :D_A]
    vv = za[:, D_A:]
    vc = vv - jnp.mean(vv, axis=-1, keepdims=True)
    v = (vc * lax.rsqrt(jnp.mean(vc * vc, axis=-1, keepdims=True) + EPS) * lnv_ref[...]).astype(BF16)

    n_blocks = tm // CHUNK
    for g in range(A_GROUPS):
        cs = slice(g * GROUP_DIM, (g + 1) * GROUP_DIM)
        v_cat = jnp.concatenate([v[n * CHUNK:(n + 1) * CHUNK, cs] for n in range(n_blocks)], axis=1)
        mixed = jnp.dot(ws_ref[g], v_cat, preferred_element_type=F32)
        for n in range(n_blocks):
            rs = slice(n * CHUNK, (n + 1) * CHUNK)
            y_ref[rs, cs] = (u[rs, cs] * (mixed[:, n * CHUNK:(n + 1) * CHUNK] + bs_ref[g])).astype(BF16)

    a2_ref[...] = p_ref[0:n_p + HALO, GROUP_DIM:] + p_ref[1:n_p + HALO + 1, GROUP_DIM:]
    a4_ref[...] = a2_ref[0:n_p, :] + a2_ref[2:n_p + 2, :]
    a8 = a4_ref[0:tm + HALO, 2 * GROUP_DIM:] + a4_ref[4:tm + HALO + 4, 2 * GROUP_DIM:]
    segs = (
        p_ref[HALO - 1:HALO - 1 + tm, :GROUP_DIM] + p_ref[HALO:HALO + tm, :GROUP_DIM],
        a4_ref[HALO - 2:HALO - 2 + tm, :GROUP_DIM],
        a4_ref[HALO - 4:HALO - 4 + tm, GROUP_DIM:2 * GROUP_DIM] + a4_ref[HALO:HALO + tm, GROUP_DIM:2 * GROUP_DIM],
        a8[0:tm] + a8[HALO:HALO + tm],
    )

    def window_count(first_row, half):
        pos = pos0 + first_row + lax.broadcasted_iota(jnp.int32, (HALO, GROUP_DIM), 0)
        return (jnp.minimum(pos + half, seq_len) - jnp.maximum(pos - half, 0)).astype(F32)

    for g, w in enumerate(POOL_WINDOWS):
        half = w // 2
        cs = slice(g * GROUP_DIM, (g + 1) * GROUP_DIM)
        seg = segs[g]
        pooled = jnp.concatenate([seg[:HALO] / window_count(0, half),
                                  seg[HALO:tm - HALO] * (1.0 / w),
                                  seg[tm - HALO:] / window_count(tm - HALO, half)], axis=0)
        diff = pooled - p_ref[HALO:HALO + tm, cs]
        yb = jnp.dot(diff.astype(BF16), wp_ref[g], preferred_element_type=F32)
        y_ref[:, D_A + g * GROUP_DIM:D_A + (g + 1) * GROUP_DIM] = (yb * ps_ref[:, cs]).astype(BF16)


def _ab_mixer(x_lat, x_ctx, mod3, layer, norm_mix, w_in, mixer_layer, ln_v, w_s, b_s, w_pool,
              pool_scale):
    tm = TM_AB
    sub = tm // HALO
    last_halo_block = x_lat.shape[0] // HALO - 1
    split = x_ctx is not x_lat
    lat_spec, ctx_spec = _stream_specs(tm, split)
    return pl.pallas_call(
        functools.partial(_ab_kernel, split=split),
        grid=(N_TOT // tm,),
        in_specs=[
            lat_spec, ctx_spec,
            pl.BlockSpec((HALO, D_MODEL), lambda t: (jnp.clip(t * sub - 1, 0, last_halo_block), 0)),
            pl.BlockSpec((HALO, D_MODEL), lambda t: (jnp.minimum((t + 1) * sub, last_halo_block), 0)),
            _mod_spec(layer, 0, tm), _mod_spec(layer, 1, tm),
            _const_spec((1, D_MODEL)),
            _layer_spec((D_MODEL, 2 * D_A + D_B), mixer_layer),
            _const_spec((1, D_A)),
            _const_spec((A_GROUPS, CHUNK, CHUNK)),
            _const_spec((A_GROUPS, CHUNK, GROUP_DIM)),
            _const_spec((A_GROUPS, GROUP_DIM, GROUP_DIM)),
            _const_spec((1, D_B)),
        ],
        out_specs=pl.BlockSpec((tm, D_MODEL), lambda t: (t, 0)),
        out_shape=jax.ShapeDtypeStruct((N_TOT, D_MODEL), BF16),
        scratch_shapes=[pltpu.VMEM((tm + 4 * HALO, D_B), F32),
                        pltpu.VMEM((tm + 3 * HALO, D_B - GROUP_DIM), F32),
                        pltpu.VMEM((tm + 2 * HALO, D_B - GROUP_DIM), F32)],
        compiler_params=pltpu.CompilerParams(
            dimension_semantics=("arbitrary",), vmem_limit_bytes=40 << 20),
        name="ab_mixer",
    )(x_lat, x_ctx, x_lat, x_lat, mod3, mod3, norm_mix, w_in, ln_v, w_s, b_s, w_pool, pool_scale)


def _qkv_kernel(x_ref, sh_ref, sc_ref, nrm_ref, w_ref, q_ref, k_ref, v_ref):
    h = _rms(x_ref[...], nrm_ref[...]) * (1.0 + sc_ref[0]) + sh_ref[0]
    hb = h.astype(BF16)
    scale = HEAD_DIM ** -0.5 * LOG2E
    q_ref[...] = (jnp.dot(hb, w_ref[:, :D_MODEL], preferred_element_type=F32) * scale).astype(BF16)
    k_ref[...] = jnp.dot(hb, w_ref[:, D_MODEL:2 * D_MODEL], preferred_element_type=F32).astype(BF16)
    v_ref[...] = jnp.dot(hb, w_ref[:, 2 * D_MODEL:], preferred_element_type=F32).astype(BF16)


def _qkv(xs, mod3, layer, norm_mix, w_qkv, mixer_layer):
    tm = TM_FFN
    row_spec = pl.BlockSpec((tm, D_MODEL), lambda t: (t, 0))
    out = jax.ShapeDtypeStruct((N_TOT, D_MODEL), BF16)
    return pl.pallas_call(
        _qkv_kernel,
        grid=(N_TOT // tm,),
        in_specs=[row_spec, _mod_spec(layer, 0, tm), _mod_spec(layer, 1, tm),
                  _const_spec((1, D_MODEL)), _layer_spec((D_MODEL, 3 * D_MODEL), mixer_layer)],
        out_specs=[row_spec, row_spec, row_spec],
        out_shape=[out, out, out],
        compiler_params=pltpu.CompilerParams(
            dimension_semantics=("arbitrary",), vmem_limit_bytes=40 << 20),
        name="qkv_proj",
    )(xs, mod3, mod3, norm_mix, w_qkv)


_NT_DIMS = (((1,), (1,)), ((), ()))


def _head_masks(rows):
    lane = lax.broadcasted_iota(jnp.int32, (rows, HEAD_PAIR), 1)
    return lane < HEAD_DIM


N_DR_PAIRS = 2 * NA_ROWS - 2


def _bias_rows(rpb):
    lo, hi = rpb[:, :-1], rpb[:, 1:]
    gap = jnp.zeros((N_HEADS, N_DR_PAIRS, GRID_W - 2 * NA_COLS + 1), F32)
    rows = jnp.concatenate([lo[..., NA_COLS - 1:], gap, hi, gap, lo[..., :NA_COLS - 1]], axis=-1)
    return rows[:, :, None, :]


def _build_bias_pairs(rows_ref, bias_ref):
    shape = (GRID_W, HEAD_PAIR)
    c = lax.broadcasted_iota(jnp.int32, shape, 0)
    kc = lax.broadcasted_iota(jnp.int32, shape, 1) % GRID_W
    col_start = jnp.clip(c - NA_COLS // 2, 0, GRID_W - NA_COLS)
    live = (kc >= col_start) & (kc < col_start + NA_COLS)
    for d in range(N_DR_PAIRS):
        for hh in range(2):
            t = pltpu.roll(jnp.broadcast_to(rows_ref[hh, d], shape), 0, 1, stride=1, stride_axis=0)
            bias_ref[d, hh * GRID_W:(hh + 1) * GRID_W, :] = jnp.where(live, t, NEG_INF) * LOG2E


N_KEYS = NB_KEYS + CTX_LEN
ROWS_PER_STEP = 16


IMAGES_PER_STEP = 2


def _na_kernel(rows_ref, q_ref, k_ref, v_ref, kc_ref, vc_ref, o_ref,
               bias_ref, vaug, s_scr):
    @pl.when(pl.program_id(1) == 0)
    def _():
        _build_bias_pairs(rows_ref, bias_ref)

    for img in range(IMAGES_PER_STEP):
        vaug[img, :SEQ, :HEAD_PAIR] = v_ref[img * SEQ:(img + 1) * SEQ, :]
        vaug[img, SEQ:, :HEAD_PAIR] = vc_ref[img * CTX_LEN:(img + 1) * CTX_LEN, :]
        vaug[img, :, HEAD_PAIR:] = jnp.ones((SEQ + CTX_LEN, HEAD_PAIR), BF16)

    first = _head_masks(GRID_W)
    steps_per_image = GRID_H // ROWS_PER_STEP

    def locate(i, j):
        img = i // steps_per_image
        r = (i % steps_per_image) * ROWS_PER_STEP + j
        return img, r, jnp.clip(r - NA_ROWS // 2, 0, GRID_H - NA_ROWS)

    def rows(row, n):
        return pl.ds(pl.multiple_of(row * GRID_W, GRID_W), n)

    def qk_stage(i):
        for j in range(ROWS_PER_STEP):
            img, r, start = locate(i, j)
            dr0 = start - r + (NA_ROWS - 1)
            q_r = q_ref[rows(img * GRID_H + r, GRID_W), :]
            zero = jnp.zeros_like(q_r)
            qs = jnp.concatenate([jnp.where(first, q_r, zero), jnp.where(first, zero, q_r)], axis=0)
            bias = jnp.concatenate([bias_ref[dr0 + 2 * t] for t in range(NA_ROWS // 2)], axis=1)
            k_w = k_ref[rows(img * GRID_H + start, NB_KEYS), :]
            kc = kc_ref[pl.ds(pl.multiple_of(img * CTX_LEN, CTX_LEN), CTX_LEN), :]
            s_scr[i % 2, j, :, :NB_KEYS] = lax.dot_general(
                qs, k_w, _NT_DIMS, preferred_element_type=F32) + bias
            s_scr[i % 2, j, :, NB_KEYS:] = lax.dot_general(qs, kc, _NT_DIMS, preferred_element_type=F32)

    def softmax_pv_stage(i):
        for j in range(ROWS_PER_STEP):
            img, r, start = locate(i, j)
            s = s_scr[i % 2, j]
            p = jnp.exp2(s - jnp.max(s, axis=-1, keepdims=True)).astype(BF16)
            oa = (jnp.dot(p[:, :NB_KEYS], vaug[img, rows(start, NB_KEYS), :], preferred_element_type=F32)
                  + jnp.dot(p[:, NB_KEYS:], vaug[img, SEQ:, :], preferred_element_type=F32))
            o2 = oa[:, :HEAD_PAIR] / oa[:, HEAD_PAIR:]
            o_ref[rows(img * GRID_H + r, GRID_W), :] = (
                jnp.where(first, o2[:GRID_W], o2[GRID_W:]).astype(BF16))

    n_steps = IMAGES_PER_STEP * steps_per_image
    qk_stage(0)

    def step(i, carry):
        softmax_pv_stage(i - 1)
        qk_stage(i)
        return carry

    lax.fori_loop(1, n_steps, step, 0)
    softmax_pv_stage(n_steps - 1)


def _na_attention(q, k, v, rpb):
    lat_rows = IMAGES_PER_STEP * SEQ
    ctx_rows = IMAGES_PER_STEP * CTX_LEN
    lat_spec = pl.BlockSpec((lat_rows, HEAD_PAIR), lambda hp, g: (g, hp))
    ctx_spec = pl.BlockSpec((ctx_rows, HEAD_PAIR), lambda hp, g: (N_LAT // ctx_rows + g, hp))
    return pl.pallas_call(
        _na_kernel,
        grid=(N_HEAD_PAIRS, BATCH // IMAGES_PER_STEP),
        in_specs=[pl.BlockSpec((2, N_DR_PAIRS, 1, HEAD_PAIR), lambda hp, g: (hp, 0, 0, 0)),
                  lat_spec, lat_spec, lat_spec, ctx_spec, ctx_spec],
        out_specs=lat_spec,
        out_shape=jax.ShapeDtypeStruct((N_TOT, D_MODEL), BF16),
        scratch_shapes=[
            pltpu.VMEM((N_DR_PAIRS, HEAD_PAIR, HEAD_PAIR), F32),
            pltpu.VMEM((IMAGES_PER_STEP, SEQ + CTX_LEN, 2 * HEAD_PAIR), BF16),
            pltpu.VMEM((2, ROWS_PER_STEP, HEAD_PAIR, N_KEYS), F32),
        ],
        compiler_params=pltpu.CompilerParams(
            dimension_semantics=("arbitrary", "arbitrary"), vmem_limit_bytes=48 << 20),
        name="na_attention",
    )(_bias_rows(rpb), q, k, v, k, v)


def _ctx_attn_kernel(q_ref, k_ref, v_ref, o_in_ref, o_ref):
    del o_in_ref
    first = _head_masks(CTX_LEN)
    for hp in range(N_HEAD_PAIRS):
        cols = slice(hp * HEAD_PAIR, (hp + 1) * HEAD_PAIR)
        q = q_ref[:, cols]
        k = k_ref[:, cols]
        v = v_ref[:, cols]
        outs = []
        for hh in range(2):
            sel = first if hh == 0 else jnp.logical_not(first)
            qm = jnp.where(sel, q, jnp.zeros_like(q))
            s = lax.dot_general(qm, k, _NT_DIMS, preferred_element_type=F32)
            p = jnp.exp2(s - jnp.max(s, axis=-1, keepdims=True))
            denom = jnp.sum(p, axis=-1, keepdims=True)
            outs.append(jnp.dot(p.astype(BF16), v, preferred_element_type=F32) / denom)
        o_ref[:, cols] = jnp.where(first, outs[0], outs[1]).astype(BF16)


def _ctx_attention(q, k, v, o):
    ctx_spec = pl.BlockSpec((CTX_LEN, D_MODEL), lambda b: (N_LAT // CTX_LEN + b, 0))
    return pl.pallas_call(
        _ctx_attn_kernel,
        grid=(BATCH,),
        in_specs=[ctx_spec, ctx_spec, ctx_spec, pl.BlockSpec(memory_space=pl.ANY)],
        out_specs=ctx_spec,
        out_shape=jax.ShapeDtypeStruct((N_TOT, D_MODEL), BF16),
        input_output_aliases={3: 0},
        compiler_params=pltpu.CompilerParams(dimension_semantics=("arbitrary",)),
        name="ctx_attention",
    )(q, k, v, o)


def kernel(x, c, ctx, c_ctx, w_mod, b_mod, norm_mix, norm_ffn, w_in_ab, ln_v, w_spatial,
           b_spatial, w_pool, pool_scale, w_out_ab, w_qkv, rpb, w_out_na, w_ffn_in,
           w_ffn_out, norm_final):
    cond = jnp.zeros((MOD_ROWS, D_MODEL), F32).at[:BATCH].set(c).at[CTX_MOD_ROW].set(c_ctx)
    mod3 = _adaln(cond, w_mod, b_mod).reshape(DEPTH * MOD_ROWS * N_MOD, 1, D_MODEL)

    x_lat, x_ctx = x.reshape(N_LAT, D_MODEL), ctx.reshape(N_CTX, D_MODEL)
    nf = norm_final.reshape(1, D_MODEL)
    w_in_ab, w_out_ab, w_qkv, w_out_na = (
        w.astype(BF16) for w in (w_in_ab, w_out_ab, w_qkv, w_out_na))
    w_in, w_out = (w_ffn_in[:1].astype(BF16), 0), (w_ffn_out[:1].astype(BF16), 0)
    for i in range(DEPTH):
        last = i == DEPTH - 1
        j = i // 2
        nm = norm_mix[i].reshape(1, D_MODEL)
        if i % 2 == 1:
            q, k, v = _qkv(x_lat, mod3, i, nm, w_qkv, j)
            y = _na_attention(q, k, v, rpb[j])
            if not last:
                y = _ctx_attention(q, k, v, y)
            w_o = w_out_na
        else:
            b_s = jnp.broadcast_to(b_spatial[j][:, :, None], (A_GROUPS, CHUNK, GROUP_DIM))
            y = _ab_mixer(x_lat, x_ctx, mod3, i, nm, w_in_ab, j, ln_v[j].reshape(1, D_A),
                          w_spatial[j].astype(BF16), b_s, w_pool[j].astype(BF16),
                          pool_scale[j].reshape(1, D_B))
            w_o = w_out_ab
        res = _ffn(x_lat, x_ctx, y, mod3, i, norm_ffn[i].reshape(1, D_MODEL), w_o, j, w_in, w_out,
                   nf, n_rows=N_LAT if last else N_TOT, final=last,
                   cast_next=None if last else (w_ffn_in, w_ffn_out, i + 1))
        if last:
            x_lat = res
        else:
            x_lat = x_ctx = res[0]
            w_in, w_out = (res[1], 0), (res[2], 0)
    return x_lat.reshape(BATCH, SEQ, D_MODEL)
```

```python
import functools
import math

import jax
import jax.numpy as jnp
from jax import lax
from jax.experimental import pallas as pl
from jax.experimental.pallas import tpu as pltpu

D_MODEL = 1024
BATCH = 4
SEQ = 4096
DEPTH = 4
GRID_W = 64
GRID_H = SEQ // GRID_W
CTX_LEN = 256
D_A = D_MODEL // 2
A_GROUPS = 4
CHUNK = 128
D_B = D_MODEL - D_A
POOL_WINDOWS = (2, 4, 8, 16)
GROUP_DIM = 128
N_HEADS = 16
HEAD_DIM = 64
NA_ROWS = 8
NA_COLS = 16
D_FF = 2816
EPS = 1e-6
LOG2E = 1.4426950408889634
NEG_INF = -1e30

N_LAT = BATCH * SEQ
N_CTX = BATCH * CTX_LEN
N_TOT = N_LAT + N_CTX
MOD_ROWS = 8
CTX_MOD_ROW = BATCH
N_MOD = 6

TM_FFN = 512
TM_QKV = 1024
TM_AB = 1024
AB_SUB = 256
HALO = 8
FF_CHUNK = 256
FFN_CAST_STEPS = 32
MOD_TN = 1536
HEAD_PAIR = 2 * HEAD_DIM
N_HEAD_PAIRS = N_HEADS // 2
NB_KEYS = NA_ROWS * GRID_W

F32 = jnp.float32
BF16 = jnp.bfloat16


def _const_spec(shape):
    nd = len(shape)
    return pl.BlockSpec(shape, lambda *_: (0,) * nd, pipeline_mode=pl.Buffered(1))


def _layer_spec(shape, layer):
    nd = len(shape)
    return pl.BlockSpec((None,) + tuple(shape), lambda *_: (layer,) + (0,) * nd,
                        pipeline_mode=pl.Buffered(1))


def _mod_spec(layer, k, tm):
    n_lat_tiles = N_LAT // tm
    tiles_per_batch = SEQ // tm

    def index_map(t, *_):
        row = jnp.where(t < n_lat_tiles, t // tiles_per_batch, CTX_MOD_ROW)
        return ((layer * MOD_ROWS + row) * N_MOD + k, 0, 0)

    return pl.BlockSpec((1, 1, D_MODEL), index_map)


def _stream_specs(tm, split):
    if not split:
        return (pl.BlockSpec((tm, D_MODEL), lambda t: (t, 0)),
                pl.BlockSpec((HALO, D_MODEL), lambda t: (0, 0)))
    n_lat = N_LAT // tm
    return (pl.BlockSpec((tm, D_MODEL), lambda t: (jnp.minimum(t, n_lat - 1), 0)),
            pl.BlockSpec((tm, D_MODEL), lambda t: (jnp.maximum(t - n_lat, 0), 0)))


def _stream_tile(xl_ref, xc_ref, tm, split):
    if not split:
        return xl_ref[...]
    return jnp.where(pl.program_id(0) >= N_LAT // tm, xc_ref[...], xl_ref[...])


def _rms(x, g):
    return x * lax.rsqrt(jnp.mean(x * x, axis=-1, keepdims=True) + EPS) * g


def _adaln_kernel(cond_ref, w_ref, b_ref, o_ref):
    c = cond_ref[...]
    s = c * jax.nn.sigmoid(c)
    w = w_ref[0]
    s_hi = s.astype(BF16)
    s_lo = (s - s_hi.astype(F32)).astype(BF16)
    w_hi = w.astype(BF16)
    w_lo = (w - w_hi.astype(F32)).astype(BF16)
    s_parts = jnp.concatenate([s_hi, s_lo], axis=0)
    by_hi = jnp.dot(s_parts, w_hi, preferred_element_type=F32)
    by_lo = jnp.dot(s_parts, w_lo, preferred_element_type=F32)
    o_ref[0] = (by_hi[:MOD_ROWS] + by_hi[MOD_ROWS:]) + (by_lo[:MOD_ROWS] + by_lo[MOD_ROWS:]) + b_ref[0]


def _adaln(cond, w_mod, b_mod):
    return pl.pallas_call(
        _adaln_kernel,
        grid=(DEPTH, N_MOD * D_MODEL // MOD_TN),
        in_specs=[
            pl.BlockSpec((MOD_ROWS, D_MODEL), lambda l, n: (0, 0)),
            pl.BlockSpec((1, D_MODEL, MOD_TN), lambda l, n: (l, 0, n)),
            pl.BlockSpec((1, 1, MOD_TN), lambda l, n: (l, 0, n)),
        ],
        out_specs=pl.BlockSpec((1, MOD_ROWS, MOD_TN), lambda l, n: (l, 0, n)),
        out_shape=jax.ShapeDtypeStruct((DEPTH, MOD_ROWS, N_MOD * D_MODEL), F32),
        compiler_params=pltpu.CompilerParams(
            dimension_semantics=("arbitrary", "arbitrary"), vmem_limit_bytes=32 << 20),
        name="adaln",
    )(cond, w_mod, b_mod.reshape(DEPTH, 1, N_MOD * D_MODEL))


def _ffn_kernel(*refs, split, final, cast_next):
    (xl_ref, xc_ref, y_ref, g1_ref, sh_ref, sc_ref, g2_ref, nrm_ref, wo_ref, wi_ref, wout_ref,
     nf_ref) = refs[:12]
    if cast_next:
        wi_next_ref, wout_next_ref, o_ref, wi_cast_ref, wout_cast_ref, hmid_ref = refs[12:]
        wi_cast_ref[...] = wi_next_ref[...].astype(BF16)
        wout_cast_ref[...] = wout_next_ref[...].astype(BF16)
    else:
        o_ref, hmid_ref = refs[12:]
    x1 = _stream_tile(xl_ref, xc_ref, TM_FFN, split) + g1_ref[0] * jnp.dot(
        y_ref[...], wo_ref[...], preferred_element_type=F32)
    h = _rms(x1, nrm_ref[...]) * (1.0 + sc_ref[0]) + sh_ref[0]
    hb = h.astype(BF16)
    for c in range(D_FF // FF_CHUNK):
        lo = c * FF_CHUNK
        a = jnp.dot(hb, wi_ref[:, lo:lo + FF_CHUNK], preferred_element_type=F32)
        g = jnp.dot(hb, wi_ref[:, D_FF + lo:D_FF + lo + FF_CHUNK], preferred_element_type=F32)
        hmid_ref[:, lo:lo + FF_CHUNK] = (a * jax.nn.sigmoid(a) * g).astype(BF16)
    out = x1 + g2_ref[0] * jnp.dot(hmid_ref[...], wout_ref[...], preferred_element_type=F32)
    if final:
        out = _rms(out, nf_ref[...])
    o_ref[...] = out


def _cast_chunk_spec(n_rows, n_cols, n_steps, layer):
    return pl.BlockSpec((None, n_rows // n_steps, n_cols),
                        lambda t: (layer, jnp.minimum(t, n_steps - 1), 0))


def _ffn(x_lat, x_ctx, y, mod3, layer, norm_ffn, w_o, mixer_layer, w_in, w_out, norm_final, *,
         n_rows, final, cast_next=None):
    tm = TM_FFN
    n_steps = n_rows // tm
    row_spec = pl.BlockSpec((tm, D_MODEL), lambda t: (t, 0))
    split = x_ctx is not x_lat
    lat_spec, ctx_spec = _stream_specs(tm, split)
    in_specs = [
        lat_spec, ctx_spec, row_spec,
        _mod_spec(layer, 2, tm), _mod_spec(layer, 3, tm), _mod_spec(layer, 4, tm),
        _mod_spec(layer, 5, tm),
        _const_spec((1, D_MODEL)),
        _layer_spec((D_MODEL, D_MODEL), mixer_layer),
        _layer_spec((D_MODEL, 2 * D_FF), w_in[1]),
        _layer_spec((D_FF, D_MODEL), w_out[1]),
        _const_spec((1, D_MODEL)),
    ]
    args = [x_lat, x_ctx, y, mod3, mod3, mod3, mod3, norm_ffn, w_o, w_in[0], w_out[0], norm_final]
    out_specs = [row_spec]
    out_shape = [jax.ShapeDtypeStruct((n_rows, D_MODEL), F32)]
    if cast_next:
        assert n_steps >= FFN_CAST_STEPS
        for w_next, (rows, cols), steps in zip(
                cast_next[:2], ((D_MODEL, 2 * D_FF), (D_FF, D_MODEL)),
                (FFN_CAST_STEPS, FFN_CAST_STEPS // 2)):
            in_specs.append(_cast_chunk_spec(rows, cols, steps, cast_next[2]))
            args.append(w_next)
            out_specs.append(_cast_chunk_spec(rows, cols, steps, 0))
            out_shape.append(jax.ShapeDtypeStruct((1, rows, cols), BF16))
    res = pl.pallas_call(
        functools.partial(_ffn_kernel, split=split, final=final, cast_next=bool(cast_next)),
        grid=(n_steps,),
        in_specs=in_specs,
        out_specs=out_specs,
        out_shape=out_shape,
        scratch_shapes=[pltpu.VMEM((tm, D_FF), BF16)],
        compiler_params=pltpu.CompilerParams(
            dimension_semantics=("arbitrary",), vmem_limit_bytes=56 << 20),
        name="outproj_ffn",
    )(*args)
    return res if cast_next else res[0]


def _gelu_tanh(x):
    k1 = -2.0 * math.sqrt(2.0 / math.pi) * LOG2E
    return x / (1.0 + jnp.exp2(x * (k1 + (k1 * 0.044715) * (x * x))))


def _ab_kernel(xl_ref, xc_ref, xp_ref, xn_ref, sh_ref, sc_ref, nrm_ref, win_ref, lnv_ref, ws_ref,
               bs_ref, wp_ref, ps_ref, y_ref, p_ref, a2_ref, a4_ref, *, split):
    x_tile = _stream_tile(xl_ref, xc_ref, TM_AB, split)
    n_sub = TM_AB // AB_SUB
    for s in range(n_sub):
        rows = slice(s * AB_SUB, (s + 1) * AB_SUB)
        prev = xp_ref[...] if s == 0 else x_tile[s * AB_SUB - HALO:s * AB_SUB]
        nxt = xn_ref[...] if s == n_sub - 1 else x_tile[(s + 1) * AB_SUB:(s + 1) * AB_SUB + HALO]
        _ab_subtile(jnp.concatenate([prev, x_tile[rows], nxt], axis=0), pl.program_id(0) * n_sub + s,
                    sh_ref, sc_ref, nrm_ref, win_ref, lnv_ref, ws_ref, bs_ref, wp_ref, ps_ref,
                    y_ref.at[rows], p_ref.at[s], a2_ref.at[s], a4_ref.at[s])


def _ab_subtile(x_all, sub_index, sh_ref, sc_ref, nrm_ref, win_ref, lnv_ref, ws_ref, bs_ref, wp_ref,
                ps_ref, y_ref, p_ref, a2_ref, a4_ref):
    tm = AB_SUB
    is_ctx = sub_index >= N_LAT // tm
    seq_len = jnp.where(is_ctx, CTX_LEN, SEQ)
    pos0 = jnp.where(is_ctx, 0, (sub_index % (SEQ // tm)) * tm)

    h = _rms(x_all, nrm_ref[...]) * (1.0 + sc_ref[0]) + sh_ref[0]
    z = jnp.dot(h.astype(BF16), win_ref[...], preferred_element_type=F32)

    n_p = tm + 2 * HALO
    pos_all = pos0 - HALO + lax.broadcasted_iota(jnp.int32, (n_p, D_B), 0)
    p_ref[:n_p, :] = jnp.where((pos_all >= 0) & (pos_all < seq_len), z[:, 2 * D_A:], 0.0)
    p_ref[n_p:, :] = jnp.zeros((2 * HALO, D_B), F32)

    za = _gelu_tanh(z[HALO:HALO + tm, :2 * D_A])
    u = za[:, :D_A]
    vv = za[:, D_A:]
    vc = vv - jnp.mean(vv, axis=-1, keepdims=True)
    v = (vc * lax.rsqrt(jnp.mean(vc * vc, axis=-1, keepdims=True) + EPS) * lnv_ref[...]).astype(BF16)

    n_blocks = tm // CHUNK
    for g in range(A_GROUPS):
        cs = slice(g * GROUP_DIM, (g + 1) * GROUP_DIM)
        v_cat = jnp.concatenate([v[n * CHUNK:(n + 1) * CHUNK, cs] for n in range(n_blocks)], axis=1)
        mixed = jnp.dot(ws_ref[g], v_cat, preferred_element_type=F32)
        for n in range(n_blocks):
            rs = slice(n * CHUNK, (n + 1) * CHUNK)
            y_ref[rs, cs] = (u[rs, cs] * (mixed[:, n * CHUNK:(n + 1) * CHUNK] + bs_ref[g])).astype(BF16)

    a2_ref[...] = p_ref[0:n_p + HALO, GROUP_DIM:] + p_ref[1:n_p + HALO + 1, GROUP_DIM:]
    a4_ref[...] = a2_ref[0:n_p, :] + a2_ref[2:n_p + 2, :]
    a8 = a4_ref[0:tm + HALO, 2 * GROUP_DIM:] + a4_ref[4:tm + HALO + 4, 2 * GROUP_DIM:]
    segs = (
        p_ref[HALO - 1:HALO - 1 + tm, :GROUP_DIM] + p_ref[HALO:HALO + tm, :GROUP_DIM],
        a4_ref[HALO - 2:HALO - 2 + tm, :GROUP_DIM],
        a4_ref[HALO - 4:HALO - 4 + tm, GROUP_DIM:2 * GROUP_DIM] + a4_ref[HALO:HALO + tm, GROUP_DIM:2 * GROUP_DIM],
        a8[0:tm] + a8[HALO:HALO + tm],
    )

    def window_count(first_row, half):
        pos = pos0 + first_row + lax.broadcasted_iota(jnp.int32, (HALO, GROUP_DIM), 0)
        return (jnp.minimum(pos + half, seq_len) - jnp.maximum(pos - half, 0)).astype(F32)

    for g, w in enumerate(POOL_WINDOWS):
        half = w // 2
        cs = slice(g * GROUP_DIM, (g + 1) * GROUP_DIM)
        seg = segs[g]
        pooled = jnp.concatenate([seg[:HALO] / window_count(0, half),
                                  seg[HALO:tm - HALO] * (1.0 / w),
                                  seg[tm - HALO:] / window_count(tm - HALO, half)], axis=0)
        diff = pooled - p_ref[HALO:HALO + tm, cs]
        yb = jnp.dot(diff.astype(BF16), wp_ref[g], preferred_element_type=F32)
        y_ref[:, D_A + g * GROUP_DIM:D_A + (g + 1) * GROUP_DIM] = (yb * ps_ref[:, cs]).astype(BF16)


def _ab_mixer(x_lat, x_ctx, mod3, layer, norm_mix, w_in, mixer_layer, ln_v, w_s, b_s, w_pool,
              pool_scale):
    tm = TM_AB
    sub = tm // HALO
    last_halo_block = x_lat.shape[0] // HALO - 1
    split = x_ctx is not x_lat
    lat_spec, ctx_spec = _stream_specs(tm, split)
    return pl.pallas_call(
        functools.partial(_ab_kernel, split=split),
        grid=(N_TOT // tm,),
        in_specs=[
            lat_spec, ctx_spec,
            pl.BlockSpec((HALO, D_MODEL), lambda t: (jnp.clip(t * sub - 1, 0, last_halo_block), 0)),
            pl.BlockSpec((HALO, D_MODEL), lambda t: (jnp.minimum((t + 1) * sub, last_halo_block), 0)),
            _mod_spec(layer, 0, tm), _mod_spec(layer, 1, tm),
            _const_spec((1, D_MODEL)),
            _layer_spec((D_MODEL, 2 * D_A + D_B), mixer_layer),
            _const_spec((1, D_A)),
            _const_spec((A_GROUPS, CHUNK, CHUNK)),
            _const_spec((A_GROUPS, CHUNK, GROUP_DIM)),
            _const_spec((A_GROUPS, GROUP_DIM, GROUP_DIM)),
            _const_spec((1, D_B)),
        ],
        out_specs=pl.BlockSpec((tm, D_MODEL), lambda t: (t, 0)),
        out_shape=jax.ShapeDtypeStruct((N_TOT, D_MODEL), BF16),
        scratch_shapes=[pltpu.VMEM((tm // AB_SUB, AB_SUB + 4 * HALO, D_B), F32),
                        pltpu.VMEM((tm // AB_SUB, AB_SUB + 3 * HALO, D_B - GROUP_DIM), F32),
                        pltpu.VMEM((tm // AB_SUB, AB_SUB + 2 * HALO, D_B - GROUP_DIM), F32)],
        compiler_params=pltpu.CompilerParams(
            dimension_semantics=("arbitrary",), vmem_limit_bytes=40 << 20),
        name="ab_mixer",
    )(x_lat, x_ctx, x_lat, x_lat, mod3, mod3, norm_mix, w_in, ln_v, w_s, b_s, w_pool, pool_scale)


def _qkv_kernel(x_ref, sh_ref, sc_ref, nrm_ref, w_ref, q_ref, k_ref, v_ref):
    h = _rms(x_ref[...], nrm_ref[...]) * (1.0 + sc_ref[0]) + sh_ref[0]
    hb = h.astype(BF16)
    scale = HEAD_DIM ** -0.5 * LOG2E
    q_ref[...] = (jnp.dot(hb, w_ref[:, :D_MODEL], preferred_element_type=F32) * scale).astype(BF16)
    k_ref[...] = jnp.dot(hb, w_ref[:, D_MODEL:2 * D_MODEL], preferred_element_type=F32).astype(BF16)
    v_ref[...] = jnp.dot(hb, w_ref[:, 2 * D_MODEL:], preferred_element_type=F32).astype(BF16)


def _qkv(xs, mod3, layer, norm_mix, w_qkv, mixer_layer):
    tm = TM_QKV
    row_spec = pl.BlockSpec((tm, D_MODEL), lambda t: (t, 0))
    out = jax.ShapeDtypeStruct((N_TOT, D_MODEL), BF16)
    return pl.pallas_call(
        _qkv_kernel,
        grid=(N_TOT // tm,),
        in_specs=[row_spec, _mod_spec(layer, 0, tm), _mod_spec(layer, 1, tm),
                  _const_spec((1, D_MODEL)), _layer_spec((D_MODEL, 3 * D_MODEL), mixer_layer)],
        out_specs=[row_spec, row_spec, row_spec],
        out_shape=[out, out, out],
        compiler_params=pltpu.CompilerParams(
            dimension_semantics=("arbitrary",), vmem_limit_bytes=40 << 20),
        name="qkv_proj",
    )(xs, mod3, mod3, norm_mix, w_qkv)


_NT_DIMS = (((1,), (1,)), ((), ()))


def _head_masks(rows):
    lane = lax.broadcasted_iota(jnp.int32, (rows, HEAD_PAIR), 1)
    return lane < HEAD_DIM


N_DR_PAIRS = 2 * NA_ROWS - 2


def _bias_rows(rpb):
    lo, hi = rpb[:, :-1], rpb[:, 1:]
    gap = jnp.zeros((N_HEADS, N_DR_PAIRS, GRID_W - 2 * NA_COLS + 1), F32)
    rows = jnp.concatenate([lo[..., NA_COLS - 1:], gap, hi, gap, lo[..., :NA_COLS - 1]], axis=-1)
    return rows[:, :, None, :]


def _build_bias_pairs(rows_ref, bias_ref):
    shape = (GRID_W, HEAD_PAIR)
    c = lax.broadcasted_iota(jnp.int32, shape, 0)
    kc = lax.broadcasted_iota(jnp.int32, shape, 1) % GRID_W
    col_start = jnp.clip(c - NA_COLS // 2, 0, GRID_W - NA_COLS)
    live = (kc >= col_start) & (kc < col_start + NA_COLS)
    for d in range(N_DR_PAIRS):
        for hh in range(2):
            t = pltpu.roll(jnp.broadcast_to(rows_ref[hh, d], shape), 0, 1, stride=1, stride_axis=0)
            bias_ref[d, hh * GRID_W:(hh + 1) * GRID_W, :] = jnp.where(live, t, NEG_INF) * LOG2E


N_KEYS = NB_KEYS + CTX_LEN
ROWS_PER_STEP = 32


IMAGES_PER_STEP = 2


def _na_kernel(rows_ref, q_ref, k_ref, v_ref, kc_ref, vc_ref, o_ref,
               bias_ref, vaug, s_scr):
    @pl.when(pl.program_id(1) == 0)
    def _():
        _build_bias_pairs(rows_ref, bias_ref)

    for img in range(IMAGES_PER_STEP):
        vaug[img, :SEQ, :HEAD_PAIR] = v_ref[img * SEQ:(img + 1) * SEQ, :]
        vaug[img, SEQ:, :HEAD_PAIR] = vc_ref[img * CTX_LEN:(img + 1) * CTX_LEN, :]
        vaug[img, :, HEAD_PAIR:] = jnp.ones((SEQ + CTX_LEN, HEAD_PAIR), BF16)

    first = _head_masks(GRID_W)
    steps_per_image = GRID_H // ROWS_PER_STEP

    def locate(i, j):
        img = i // steps_per_image
        r = (i % steps_per_image) * ROWS_PER_STEP + j
        return img, r, jnp.clip(r - NA_ROWS // 2, 0, GRID_H - NA_ROWS)

    def rows(row, n):
        return pl.ds(pl.multiple_of(row * GRID_W, GRID_W), n)

    def qk_stage(i):
        for j in range(ROWS_PER_STEP):
            img, r, start = locate(i, j)
            dr0 = start - r + (NA_ROWS - 1)
            q_r = q_ref[rows(img * GRID_H + r, GRID_W), :]
            zero = jnp.zeros_like(q_r)
            qs = jnp.concatenate([jnp.where(first, q_r, zero), jnp.where(first, zero, q_r)], axis=0)
            bias = jnp.concatenate([bias_ref[dr0 + 2 * t] for t in range(NA_ROWS // 2)], axis=1)
            k_w = k_ref[rows(img * GRID_H + start, NB_KEYS), :]
            kc = kc_ref[pl.ds(pl.multiple_of(img * CTX_LEN, CTX_LEN), CTX_LEN), :]
            s_scr[i % 2, j, :, :NB_KEYS] = lax.dot_general(
                qs, k_w, _NT_DIMS, preferred_element_type=F32) + bias
            s_scr[i % 2, j, :, NB_KEYS:] = lax.dot_general(qs, kc, _NT_DIMS, preferred_element_type=F32)

    def softmax_pv_stage(i):
        for j in range(ROWS_PER_STEP):
            img, r, start = locate(i, j)
            s = s_scr[i % 2, j]
            p = jnp.exp2(s - jnp.max(s, axis=-1, keepdims=True)).astype(BF16)
            oa = (jnp.dot(p[:, :NB_KEYS], vaug[img, rows(start, NB_KEYS), :], preferred_element_type=F32)
                  + jnp.dot(p[:, NB_KEYS:], vaug[img, SEQ:, :], preferred_element_type=F32))
            o2 = oa[:, :HEAD_PAIR] / oa[:, HEAD_PAIR:]
            o_ref[rows(img * GRID_H + r, GRID_W), :] = (
                jnp.where(first, o2[:GRID_W], o2[GRID_W:]).astype(BF16))

    n_steps = IMAGES_PER_STEP * steps_per_image
    qk_stage(0)

    def step(i, carry):
        softmax_pv_stage(i - 1)
        qk_stage(i)
        return carry

    lax.fori_loop(1, n_steps, step, 0)
    softmax_pv_stage(n_steps - 1)


def _na_attention(q, k, v, rpb):
    lat_rows = IMAGES_PER_STEP * SEQ
    ctx_rows = IMAGES_PER_STEP * CTX_LEN
    lat_spec = pl.BlockSpec((lat_rows, HEAD_PAIR), lambda hp, g: (g, hp))
    ctx_spec = pl.BlockSpec((ctx_rows, HEAD_PAIR), lambda hp, g: (N_LAT // ctx_rows + g, hp))
    return pl.pallas_call(
        _na_kernel,
        grid=(N_HEAD_PAIRS, BATCH // IMAGES_PER_STEP),
        in_specs=[pl.BlockSpec((2, N_DR_PAIRS, 1, HEAD_PAIR), lambda hp, g: (hp, 0, 0, 0)),
                  lat_spec, lat_spec, lat_spec, ctx_spec, ctx_spec],
        out_specs=lat_spec,
        out_shape=jax.ShapeDtypeStruct((N_TOT, D_MODEL), BF16),
        scratch_shapes=[
            pltpu.VMEM((N_DR_PAIRS, HEAD_PAIR, HEAD_PAIR), F32),
            pltpu.VMEM((IMAGES_PER_STEP, SEQ + CTX_LEN, 2 * HEAD_PAIR), BF16),
            pltpu.VMEM((2, ROWS_PER_STEP, HEAD_PAIR, N_KEYS), F32),
        ],
        compiler_params=pltpu.CompilerParams(
            dimension_semantics=("arbitrary", "arbitrary"), vmem_limit_bytes=48 << 20),
        name="na_attention",
    )(_bias_rows(rpb), q, k, v, k, v)


def _ctx_attn_kernel(q_ref, k_ref, v_ref, o_in_ref, o_ref):
    del o_in_ref
    first = _head_masks(CTX_LEN)
    for hp in range(N_HEAD_PAIRS):
        cols = slice(hp * HEAD_PAIR, (hp + 1) * HEAD_PAIR)
        q = q_ref[:, cols]
        k = k_ref[:, cols]
        v = v_ref[:, cols]
        outs = []
        for hh in range(2):
            sel = first if hh == 0 else jnp.logical_not(first)
            qm = jnp.where(sel, q, jnp.zeros_like(q))
            s = lax.dot_general(qm, k, _NT_DIMS, preferred_element_type=F32)
            p = jnp.exp2(s - jnp.max(s, axis=-1, keepdims=True))
            denom = jnp.sum(p, axis=-1, keepdims=True)
            outs.append(jnp.dot(p.astype(BF16), v, preferred_element_type=F32) / denom)
        o_ref[:, cols] = jnp.where(first, outs[0], outs[1]).astype(BF16)


def _ctx_attention(q, k, v, o):
    ctx_spec = pl.BlockSpec((CTX_LEN, D_MODEL), lambda b: (N_LAT // CTX_LEN + b, 0))
    return pl.pallas_call(
        _ctx_attn_kernel,
        grid=(BATCH,),
        in_specs=[ctx_spec, ctx_spec, ctx_spec, pl.BlockSpec(memory_space=pl.ANY)],
        out_specs=ctx_spec,
        out_shape=jax.ShapeDtypeStruct((N_TOT, D_MODEL), BF16),
        input_output_aliases={3: 0},
        compiler_params=pltpu.CompilerParams(dimension_semantics=("arbitrary",)),
        name="ctx_attention",
    )(q, k, v, o)


def kernel(x, c, ctx, c_ctx, w_mod, b_mod, norm_mix, norm_ffn, w_in_ab, ln_v, w_spatial,
           b_spatial, w_pool, pool_scale, w_out_ab, w_qkv, rpb, w_out_na, w_ffn_in,
           w_ffn_out, norm_final):
    cond = jnp.zeros((MOD_ROWS, D_MODEL), F32).at[:BATCH].set(c).at[CTX_MOD_ROW].set(c_ctx)
    mod3 = _adaln(cond, w_mod, b_mod).reshape(DEPTH * MOD_ROWS * N_MOD, 1, D_MODEL)

    x_lat, x_ctx = x.reshape(N_LAT, D_MODEL), ctx.reshape(N_CTX, D_MODEL)
    nf = norm_final.reshape(1, D_MODEL)
    w_in_ab, w_out_ab, w_qkv, w_out_na = (
        w.astype(BF16) for w in (w_in_ab, w_out_ab, w_qkv, w_out_na))
    w_in, w_out = (w_ffn_in[:1].astype(BF16), 0), (w_ffn_out[:1].astype(BF16), 0)
    for i in range(DEPTH):
        last = i == DEPTH - 1
        j = i // 2
        nm = norm_mix[i].reshape(1, D_MODEL)
        if i % 2 == 1:
            q, k, v = _qkv(x_lat, mod3, i, nm, w_qkv, j)
            y = _na_attention(q, k, v, rpb[j])
            if not last:
                y = _ctx_attention(q, k, v, y)
            w_o = w_out_na
        else:
            b_s = jnp.broadcast_to(b_spatial[j][:, :, None], (A_GROUPS, CHUNK, GROUP_DIM))
            y = _ab_mixer(x_lat, x_ctx, mod3, i, nm, w_in_ab, j, ln_v[j].reshape(1, D_A),
                          w_spatial[j].astype(BF16), b_s, w_pool[j].astype(BF16),
                          pool_scale[j].reshape(1, D_B))
            w_o = w_out_ab
        res = _ffn(x_lat, x_ctx, y, mod3, i, norm_ffn[i].reshape(1, D_MODEL), w_o, j, w_in, w_out,
                   nf, n_rows=N_LAT if last else N_TOT, final=last,
                   cast_next=None if last else (w_ffn_in, w_ffn_out, i + 1))
        if last:
            x_lat = res
        else:
            x_lat = x_ctx = res[0]
            w_in, w_out = (res[1], 0), (res[2], 0)
    return x_lat.reshape(BATCH, SEQ, D_MODEL)
```

```python
import functools
import math

import jax
import jax.numpy as jnp
from jax import lax
from jax.experimental import pallas as pl
from jax.experimental.pallas import tpu as pltpu

D_MODEL = 1024
BATCH = 4
SEQ = 4096
DEPTH = 4
GRID_W = 64
GRID_H = SEQ // GRID_W
CTX_LEN = 256
D_A = D_MODEL // 2
A_GROUPS = 4
CHUNK = 128
D_B = D_MODEL - D_A
POOL_WINDOWS = (2, 4, 8, 16)
GROUP_DIM = 128
N_HEADS = 16
HEAD_DIM = 64
NA_ROWS = 8
NA_COLS = 16
D_FF = 2816
EPS = 1e-6
LOG2E = 1.4426950408889634
NEG_INF = -1e30

N_LAT = BATCH * SEQ
N_CTX = BATCH * CTX_LEN
N_TOT = N_LAT + N_CTX
MOD_ROWS = 8
CTX_MOD_ROW = BATCH
N_MOD = 6

TM_FFN = 1024
TM_QKV = 1024
TM_AB = 1024
AB_SUB = 256
HALO = 8
FF_CHUNK = 256
CAST_STEPS = 16
MOD_TN = 1536
HEAD_PAIR = 2 * HEAD_DIM
N_HEAD_PAIRS = N_HEADS // 2
NB_KEYS = NA_ROWS * GRID_W

F32 = jnp.float32
BF16 = jnp.bfloat16


def _const_spec(shape):
    nd = len(shape)
    return pl.BlockSpec(shape, lambda *_: (0,) * nd, pipeline_mode=pl.Buffered(1))


def _layer_spec(shape, layer):
    nd = len(shape)
    return pl.BlockSpec((None,) + tuple(shape), lambda *_: (layer,) + (0,) * nd,
                        pipeline_mode=pl.Buffered(1))


def _mod_spec(layer, k, tm):
    n_lat_tiles = N_LAT // tm
    tiles_per_batch = SEQ // tm

    def index_map(t, *_):
        row = jnp.where(t < n_lat_tiles, t // tiles_per_batch, CTX_MOD_ROW)
        return ((layer * MOD_ROWS + row) * N_MOD + k, 0, 0)

    return pl.BlockSpec((1, 1, D_MODEL), index_map)


def _stream_specs(tm, split):
    if not split:
        return (pl.BlockSpec((tm, D_MODEL), lambda t: (t, 0)),
                pl.BlockSpec((HALO, D_MODEL), lambda t: (0, 0)))
    n_lat = N_LAT // tm
    return (pl.BlockSpec((tm, D_MODEL), lambda t: (jnp.minimum(t, n_lat - 1), 0)),
            pl.BlockSpec((tm, D_MODEL), lambda t: (jnp.maximum(t - n_lat, 0), 0)))


def _stream_tile(xl_ref, xc_ref, tm, split):
    if not split:
        return xl_ref[...]
    return jnp.where(pl.program_id(0) >= N_LAT // tm, xc_ref[...], xl_ref[...])


def _rms(x, g):
    return x * lax.rsqrt(jnp.mean(x * x, axis=-1, keepdims=True) + EPS) * g


def _adaln_kernel(cond_ref, w_ref, b_ref, o_ref):
    c = cond_ref[...]
    s = c * jax.nn.sigmoid(c)
    w = w_ref[0]
    s_hi = s.astype(BF16)
    s_lo = (s - s_hi.astype(F32)).astype(BF16)
    w_hi = w.astype(BF16)
    w_lo = (w - w_hi.astype(F32)).astype(BF16)
    s_parts = jnp.concatenate([s_hi, s_lo], axis=0)
    by_hi = jnp.dot(s_parts, w_hi, preferred_element_type=F32)
    by_lo = jnp.dot(s_parts, w_lo, preferred_element_type=F32)
    o_ref[0] = (by_hi[:MOD_ROWS] + by_hi[MOD_ROWS:]) + (by_lo[:MOD_ROWS] + by_lo[MOD_ROWS:]) + b_ref[0]


def _adaln(cond, w_mod, b_mod):
    return pl.pallas_call(
        _adaln_kernel,
        grid=(DEPTH, N_MOD * D_MODEL // MOD_TN),
        in_specs=[
            pl.BlockSpec((MOD_ROWS, D_MODEL), lambda l, n: (0, 0)),
            pl.BlockSpec((1, D_MODEL, MOD_TN), lambda l, n: (l, 0, n)),
            pl.BlockSpec((1, 1, MOD_TN), lambda l, n: (l, 0, n)),
        ],
        out_specs=pl.BlockSpec((1, MOD_ROWS, MOD_TN), lambda l, n: (l, 0, n)),
        out_shape=jax.ShapeDtypeStruct((DEPTH, MOD_ROWS, N_MOD * D_MODEL), F32),
        compiler_params=pltpu.CompilerParams(
            dimension_semantics=("arbitrary", "arbitrary"), vmem_limit_bytes=32 << 20),
        name="adaln",
    )(cond, w_mod, b_mod.reshape(DEPTH, 1, N_MOD * D_MODEL))


def _ffn_cast_io(w_ffn_in, w_ffn_out, layer):
    in_specs, out_specs, out_shape = [], [], []
    for rows, cols in ((D_MODEL, 2 * D_FF), (D_FF, D_MODEL)):
        chunk = (None, rows // CAST_STEPS, cols)
        in_specs.append(pl.BlockSpec(chunk, lambda t: (layer, jnp.minimum(t, CAST_STEPS - 1), 0)))
        out_specs.append(pl.BlockSpec(chunk, lambda t: (0, jnp.minimum(t, CAST_STEPS - 1), 0)))
        out_shape.append(jax.ShapeDtypeStruct((1, rows, cols), BF16))
    return in_specs, [w_ffn_in, w_ffn_out], out_specs, out_shape


def _cast_chunks(src_refs, dst_refs):
    for src_ref, dst_ref in zip(src_refs, dst_refs):
        dst_ref[...] = src_ref[...].astype(BF16)


def _ffn_kernel(xl_ref, xc_ref, y_ref, g1_ref, sh_ref, sc_ref, g2_ref, nrm_ref, wo_ref, wi_ref,
                wout_ref, nf_ref, o_ref, hmid_ref, *, split, final):
    x1 = _stream_tile(xl_ref, xc_ref, TM_FFN, split) + g1_ref[0] * jnp.dot(
        y_ref[...], wo_ref[...], preferred_element_type=F32)
    h = _rms(x1, nrm_ref[...]) * (1.0 + sc_ref[0]) + sh_ref[0]
    hb = h.astype(BF16)
    for c in range(D_FF // FF_CHUNK):
        lo = c * FF_CHUNK
        a = jnp.dot(hb, wi_ref[:, lo:lo + FF_CHUNK], preferred_element_type=F32)
        g = jnp.dot(hb, wi_ref[:, D_FF + lo:D_FF + lo + FF_CHUNK], preferred_element_type=F32)
        hmid_ref[:, lo:lo + FF_CHUNK] = (a * jax.nn.sigmoid(a) * g).astype(BF16)
    out = x1 + g2_ref[0] * jnp.dot(hmid_ref[...], wout_ref[...], preferred_element_type=F32)
    if final:
        out = _rms(out, nf_ref[...])
    o_ref[...] = out


def _ffn(x_lat, x_ctx, y, mod3, layer, norm_ffn, w_o, mixer_layer, w_in, w_out, norm_final, *,
         n_rows, final):
    tm = TM_FFN
    row_spec = pl.BlockSpec((tm, D_MODEL), lambda t: (t, 0))
    split = x_ctx is not x_lat
    lat_spec, ctx_spec = _stream_specs(tm, split)
    return pl.pallas_call(
        functools.partial(_ffn_kernel, split=split, final=final),
        grid=(n_rows // tm,),
        in_specs=[
            lat_spec, ctx_spec, row_spec,
            _mod_spec(layer, 2, tm), _mod_spec(layer, 3, tm), _mod_spec(layer, 4, tm),
            _mod_spec(layer, 5, tm),
            _const_spec((1, D_MODEL)),
            _layer_spec((D_MODEL, D_MODEL), mixer_layer),
            _layer_spec((D_MODEL, 2 * D_FF), 0),
            _layer_spec((D_FF, D_MODEL), 0),
            _const_spec((1, D_MODEL)),
        ],
        out_specs=row_spec,
        out_shape=jax.ShapeDtypeStruct((n_rows, D_MODEL), F32),
        scratch_shapes=[pltpu.VMEM((tm, D_FF), BF16)],
        compiler_params=pltpu.CompilerParams(
            dimension_semantics=("arbitrary",), vmem_limit_bytes=60 << 20),
        name="outproj_ffn",
    )(x_lat, x_ctx, y, mod3, mod3, mod3, mod3, norm_ffn, w_o, w_in, w_out, norm_final)


def _gelu_tanh(x):
    k1 = -2.0 * math.sqrt(2.0 / math.pi) * LOG2E
    return x / (1.0 + jnp.exp2(x * (k1 + (k1 * 0.044715) * (x * x))))


def _ab_kernel(xl_ref, xc_ref, xp_ref, xn_ref, sh_ref, sc_ref, nrm_ref, win_ref, lnv_ref, ws_ref,
               bs_ref, wp_ref, ps_ref, wi_f32_ref, wout_f32_ref, y_ref, wi_cast_ref, wout_cast_ref,
               p_ref, a2_ref, a4_ref, *, split):
    _cast_chunks((wi_f32_ref, wout_f32_ref), (wi_cast_ref, wout_cast_ref))
    x_tile = _stream_tile(xl_ref, xc_ref, TM_AB, split)
    n_sub = TM_AB // AB_SUB
    for s in range(n_sub):
        rows = slice(s * AB_SUB, (s + 1) * AB_SUB)
        prev = xp_ref[...] if s == 0 else x_tile[s * AB_SUB - HALO:s * AB_SUB]
        nxt = xn_ref[...] if s == n_sub - 1 else x_tile[(s + 1) * AB_SUB:(s + 1) * AB_SUB + HALO]
        _ab_subtile(jnp.concatenate([prev, x_tile[rows], nxt], axis=0), pl.program_id(0) * n_sub + s,
                    sh_ref, sc_ref, nrm_ref, win_ref, lnv_ref, ws_ref, bs_ref, wp_ref, ps_ref,
                    y_ref.at[rows], p_ref.at[s], a2_ref.at[s], a4_ref.at[s])


def _ab_subtile(x_all, sub_index, sh_ref, sc_ref, nrm_ref, win_ref, lnv_ref, ws_ref, bs_ref, wp_ref,
                ps_ref, y_ref, p_ref, a2_ref, a4_ref):
    tm = AB_SUB
    is_ctx = sub_index >= N_LAT // tm
    seq_len = jnp.where(is_ctx, CTX_LEN, SEQ)
    pos0 = jnp.where(is_ctx, 0, (sub_index % (SEQ // tm)) * tm)

    h = _rms(x_all, nrm_ref[...]) * (1.0 + sc_ref[0]) + sh_ref[0]
    z = jnp.dot(h.astype(BF16), win_ref[...], preferred_element_type=F32)

    n_p = tm + 2 * HALO
    pos_all = pos0 - HALO + lax.broadcasted_iota(jnp.int32, (n_p, D_B), 0)
    p_ref[:n_p, :] = jnp.where((pos_all >= 0) & (pos_all < seq_len), z[:, 2 * D_A:], 0.0)
    p_ref[n_p:, :] = jnp.zeros((2 * HALO, D_B), F32)

    za = _gelu_tanh(z[HALO:HALO + tm, :2 * D_A])
    u = za[:, :D_A]
    vv = za[:, D_A:]
    vc = vv - jnp.mean(vv, axis=-1, keepdims=True)
    v = (vc * lax.rsqrt(jnp.mean(vc * vc, axis=-1, keepdims=True) + EPS) * lnv_ref[...]).astype(BF16)

    n_blocks = tm // CHUNK
    for g in range(A_GROUPS):
        cs = slice(g * GROUP_DIM, (g + 1) * GROUP_DIM)
        v_cat = jnp.concatenate([v[n * CHUNK:(n + 1) * CHUNK, cs] for n in range(n_blocks)], axis=1)
        mixed = jnp.dot(ws_ref[g], v_cat, preferred_element_type=F32)
        for n in range(n_blocks):
            rs = slice(n * CHUNK, (n + 1) * CHUNK)
            y_ref[rs, cs] = (u[rs, cs] * (mixed[:, n * CHUNK:(n + 1) * CHUNK] + bs_ref[g])).astype(BF16)

    a2_ref[...] = p_ref[0:n_p + HALO, GROUP_DIM:] + p_ref[1:n_p + HALO + 1, GROUP_DIM:]
    a4_ref[...] = a2_ref[0:n_p, :] + a2_ref[2:n_p + 2, :]
    a8 = a4_ref[0:tm + HALO, 2 * GROUP_DIM:] + a4_ref[4:tm + HALO + 4, 2 * GROUP_DIM:]
    segs = (
        p_ref[HALO - 1:HALO - 1 + tm, :GROUP_DIM] + p_ref[HALO:HALO + tm, :GROUP_DIM],
        a4_ref[HALO - 2:HALO - 2 + tm, :GROUP_DIM],
        a4_ref[HALO - 4:HALO - 4 + tm, GROUP_DIM:2 * GROUP_DIM] + a4_ref[HALO:HALO + tm, GROUP_DIM:2 * GROUP_DIM],
        a8[0:tm] + a8[HALO:HALO + tm],
    )

    def window_count(first_row, half):
        pos = pos0 + first_row + lax.broadcasted_iota(jnp.int32, (HALO, GROUP_DIM), 0)
        return (jnp.minimum(pos + half, seq_len) - jnp.maximum(pos - half, 0)).astype(F32)

    for g, w in enumerate(POOL_WINDOWS):
        half = w // 2
        cs = slice(g * GROUP_DIM, (g + 1) * GROUP_DIM)
        seg = segs[g]
        pooled = jnp.concatenate([seg[:HALO] / window_count(0, half),
                                  seg[HALO:tm - HALO] * (1.0 / w),
                                  seg[tm - HALO:] / window_count(tm - HALO, half)], axis=0)
        diff = pooled - p_ref[HALO:HALO + tm, cs]
        yb = jnp.dot(diff.astype(BF16), wp_ref[g], preferred_element_type=F32)
        y_ref[:, D_A + g * GROUP_DIM:D_A + (g + 1) * GROUP_DIM] = (yb * ps_ref[:, cs]).astype(BF16)


def _ab_mixer(x_lat, x_ctx, mod3, layer, norm_mix, w_in, mixer_layer, ln_v, w_s, b_s, w_pool,
              pool_scale, ffn_cast):
    cast_in_specs, cast_inputs, cast_out_specs, cast_out_shape = ffn_cast
    tm = TM_AB
    sub = tm // HALO
    last_halo_block = x_lat.shape[0] // HALO - 1
    split = x_ctx is not x_lat
    lat_spec, ctx_spec = _stream_specs(tm, split)
    return pl.pallas_call(
        functools.partial(_ab_kernel, split=split),
        grid=(N_TOT // tm,),
        in_specs=[
            lat_spec, ctx_spec,
            pl.BlockSpec((HALO, D_MODEL), lambda t: (jnp.clip(t * sub - 1, 0, last_halo_block), 0)),
            pl.BlockSpec((HALO, D_MODEL), lambda t: (jnp.minimum((t + 1) * sub, last_halo_block), 0)),
            _mod_spec(layer, 0, tm), _mod_spec(layer, 1, tm),
            _const_spec((1, D_MODEL)),
            _layer_spec((D_MODEL, 2 * D_A + D_B), mixer_layer),
            _const_spec((1, D_A)),
            _const_spec((A_GROUPS, CHUNK, CHUNK)),
            _const_spec((A_GROUPS, CHUNK, GROUP_DIM)),
            _const_spec((A_GROUPS, GROUP_DIM, GROUP_DIM)),
            _const_spec((1, D_B)),
            *cast_in_specs,
        ],
        out_specs=[pl.BlockSpec((tm, D_MODEL), lambda t: (t, 0)), *cast_out_specs],
        out_shape=[jax.ShapeDtypeStruct((N_TOT, D_MODEL), BF16), *cast_out_shape],
        scratch_shapes=[pltpu.VMEM((tm // AB_SUB, AB_SUB + 4 * HALO, D_B), F32),
                        pltpu.VMEM((tm // AB_SUB, AB_SUB + 3 * HALO, D_B - GROUP_DIM), F32),
                        pltpu.VMEM((tm // AB_SUB, AB_SUB + 2 * HALO, D_B - GROUP_DIM), F32)],
        compiler_params=pltpu.CompilerParams(
            dimension_semantics=("arbitrary",), vmem_limit_bytes=48 << 20),
        name="ab_mixer",
    )(x_lat, x_ctx, x_lat, x_lat, mod3, mod3, norm_mix, w_in, ln_v, w_s, b_s, w_pool, pool_scale,
      *cast_inputs)


def _qkv_kernel(x_ref, sh_ref, sc_ref, nrm_ref, w_ref, wi_f32_ref, wout_f32_ref, q_ref, k_ref, v_ref,
                wi_cast_ref, wout_cast_ref):
    _cast_chunks((wi_f32_ref, wout_f32_ref), (wi_cast_ref, wout_cast_ref))
    h = _rms(x_ref[...], nrm_ref[...]) * (1.0 + sc_ref[0]) + sh_ref[0]
    hb = h.astype(BF16)
    scale = HEAD_DIM ** -0.5 * LOG2E
    q_ref[...] = (jnp.dot(hb, w_ref[:, :D_MODEL], preferred_element_type=F32) * scale).astype(BF16)
    k_ref[...] = jnp.dot(hb, w_ref[:, D_MODEL:2 * D_MODEL], preferred_element_type=F32).astype(BF16)
    v_ref[...] = jnp.dot(hb, w_ref[:, 2 * D_MODEL:], preferred_element_type=F32).astype(BF16)


def _qkv(xs, mod3, layer, norm_mix, w_qkv, mixer_layer, ffn_cast):
    cast_in_specs, cast_inputs, cast_out_specs, cast_out_shape = ffn_cast
    tm = TM_QKV
    row_spec = pl.BlockSpec((tm, D_MODEL), lambda t: (t, 0))
    out = jax.ShapeDtypeStruct((N_TOT, D_MODEL), BF16)
    return pl.pallas_call(
        _qkv_kernel,
        grid=(N_TOT // tm,),
        in_specs=[row_spec, _mod_spec(layer, 0, tm), _mod_spec(layer, 1, tm),
                  _const_spec((1, D_MODEL)), _layer_spec((D_MODEL, 3 * D_MODEL), mixer_layer),
                  *cast_in_specs],
        out_specs=[row_spec, row_spec, row_spec, *cast_out_specs],
        out_shape=[out, out, out, *cast_out_shape],
        compiler_params=pltpu.CompilerParams(
            dimension_semantics=("arbitrary",), vmem_limit_bytes=48 << 20),
        name="qkv_proj",
    )(xs, mod3, mod3, norm_mix, w_qkv, *cast_inputs)


_NT_DIMS = (((1,), (1,)), ((), ()))


def _head_masks(rows):
    lane = lax.broadcasted_iota(jnp.int32, (rows, HEAD_PAIR), 1)
    return lane < HEAD_DIM


N_DR_PAIRS = 2 * NA_ROWS - 2


def _bias_rows(rpb):
    lo, hi = rpb[:, :-1], rpb[:, 1:]
    gap = jnp.zeros((N_HEADS, N_DR_PAIRS, GRID_W - 2 * NA_COLS + 1), F32)
    rows = jnp.concatenate([lo[..., NA_COLS - 1:], gap, hi, gap, lo[..., :NA_COLS - 1]], axis=-1)
    return rows[:, :, None, :]


def _build_bias_pairs(rows_ref, bias_ref):
    shape = (GRID_W, HEAD_PAIR)
    c = lax.broadcasted_iota(jnp.int32, shape, 0)
    kc = lax.broadcasted_iota(jnp.int32, shape, 1) % GRID_W
    col_start = jnp.clip(c - NA_COLS // 2, 0, GRID_W - NA_COLS)
    live = (kc >= col_start) & (kc < col_start + NA_COLS)
    for d in range(N_DR_PAIRS):
        for hh in range(2):
            t = pltpu.roll(jnp.broadcast_to(rows_ref[hh, d], shape), 0, 1, stride=1, stride_axis=0)
            bias_ref[d, hh * GRID_W:(hh + 1) * GRID_W, :] = jnp.where(live, t, NEG_INF) * LOG2E


N_KEYS = NB_KEYS + CTX_LEN
ROWS_PER_STEP = 32


IMAGES_PER_STEP = 2


def _na_kernel(rows_ref, q_ref, k_ref, v_ref, kc_ref, vc_ref, o_ref,
               bias_ref, vaug, s_scr):
    @pl.when(pl.program_id(1) == 0)
    def _():
        _build_bias_pairs(rows_ref, bias_ref)

    for img in range(IMAGES_PER_STEP):
        vaug[img, :SEQ, :HEAD_PAIR] = v_ref[img * SEQ:(img + 1) * SEQ, :]
        vaug[img, SEQ:, :HEAD_PAIR] = vc_ref[img * CTX_LEN:(img + 1) * CTX_LEN, :]
        vaug[img, :, HEAD_PAIR:] = jnp.ones((SEQ + CTX_LEN, HEAD_PAIR), BF16)

    first = _head_masks(GRID_W)
    steps_per_image = GRID_H // ROWS_PER_STEP

    def locate(i, j):
        img = i // steps_per_image
        r = (i % steps_per_image) * ROWS_PER_STEP + j
        return img, r, jnp.clip(r - NA_ROWS // 2, 0, GRID_H - NA_ROWS)

    def rows(row, n):
        return pl.ds(pl.multiple_of(row * GRID_W, GRID_W), n)

    def qk_stage(i):
        for j in range(ROWS_PER_STEP):
            img, r, start = locate(i, j)
            dr0 = start - r + (NA_ROWS - 1)
            q_r = q_ref[rows(img * GRID_H + r, GRID_W), :]
            zero = jnp.zeros_like(q_r)
            qs = jnp.concatenate([jnp.where(first, q_r, zero), jnp.where(first, zero, q_r)], axis=0)
            bias = jnp.concatenate([bias_ref[dr0 + 2 * t] for t in range(NA_ROWS // 2)], axis=1)
            k_w = k_ref[rows(img * GRID_H + start, NB_KEYS), :]
            kc = kc_ref[pl.ds(pl.multiple_of(img * CTX_LEN, CTX_LEN), CTX_LEN), :]
            s_scr[i % 2, j, :, :NB_KEYS] = lax.dot_general(
                qs, k_w, _NT_DIMS, preferred_element_type=F32) + bias
            s_scr[i % 2, j, :, NB_KEYS:] = lax.dot_general(qs, kc, _NT_DIMS, preferred_element_type=F32)

    def softmax_pv_stage(i):
        for j in range(ROWS_PER_STEP):
            img, r, start = locate(i, j)
            s = s_scr[i % 2, j]
            p = jnp.exp2(s - jnp.max(s, axis=-1, keepdims=True)).astype(BF16)
            oa = (jnp.dot(p[:, :NB_KEYS], vaug[img, rows(start, NB_KEYS), :], preferred_element_type=F32)
                  + jnp.dot(p[:, NB_KEYS:], vaug[img, SEQ:, :], preferred_element_type=F32))
            o2 = oa[:, :HEAD_PAIR] / oa[:, HEAD_PAIR:]
            o_ref[rows(img * GRID_H + r, GRID_W), :] = (
                jnp.where(first, o2[:GRID_W], o2[GRID_W:]).astype(BF16))

    n_steps = IMAGES_PER_STEP * steps_per_image
    qk_stage(0)

    def step(i, carry):
        softmax_pv_stage(i - 1)
        qk_stage(i)
        return carry

    lax.fori_loop(1, n_steps, step, 0)
    softmax_pv_stage(n_steps - 1)


def _na_attention(q, k, v, rpb):
    lat_rows = IMAGES_PER_STEP * SEQ
    ctx_rows = IMAGES_PER_STEP * CTX_LEN
    lat_spec = pl.BlockSpec((lat_rows, HEAD_PAIR), lambda hp, g: (g, hp))
    ctx_spec = pl.BlockSpec((ctx_rows, HEAD_PAIR), lambda hp, g: (N_LAT // ctx_rows + g, hp))
    return pl.pallas_call(
        _na_kernel,
        grid=(N_HEAD_PAIRS, BATCH // IMAGES_PER_STEP),
        in_specs=[pl.BlockSpec((2, N_DR_PAIRS, 1, HEAD_PAIR), lambda hp, g: (hp, 0, 0, 0)),
                  lat_spec, lat_spec, lat_spec, ctx_spec, ctx_spec],
        out_specs=lat_spec,
        out_shape=jax.ShapeDtypeStruct((N_TOT, D_MODEL), BF16),
        scratch_shapes=[
            pltpu.VMEM((N_DR_PAIRS, HEAD_PAIR, HEAD_PAIR), F32),
            pltpu.VMEM((IMAGES_PER_STEP, SEQ + CTX_LEN, 2 * HEAD_PAIR), BF16),
            pltpu.VMEM((2, ROWS_PER_STEP, HEAD_PAIR, N_KEYS), F32),
        ],
        compiler_params=pltpu.CompilerParams(
            dimension_semantics=("arbitrary", "arbitrary"), vmem_limit_bytes=48 << 20),
        name="na_attention",
    )(_bias_rows(rpb), q, k, v, k, v)


def _ctx_attn_kernel(q_ref, k_ref, v_ref, o_in_ref, o_ref):
    del o_in_ref
    first = _head_masks(CTX_LEN)
    for hp in range(N_HEAD_PAIRS):
        cols = slice(hp * HEAD_PAIR, (hp + 1) * HEAD_PAIR)
        q = q_ref[:, cols]
        k = k_ref[:, cols]
        v = v_ref[:, cols]
        outs = []
        for hh in range(2):
            sel = first if hh == 0 else jnp.logical_not(first)
            qm = jnp.where(sel, q, jnp.zeros_like(q))
            s = lax.dot_general(qm, k, _NT_DIMS, preferred_element_type=F32)
            p = jnp.exp2(s - jnp.max(s, axis=-1, keepdims=True))
            denom = jnp.sum(p, axis=-1, keepdims=True)
            outs.append(jnp.dot(p.astype(BF16), v, preferred_element_type=F32) / denom)
        o_ref[:, cols] = jnp.where(first, outs[0], outs[1]).astype(BF16)


def _ctx_attention(q, k, v, o):
    ctx_spec = pl.BlockSpec((CTX_LEN, D_MODEL), lambda b: (N_LAT // CTX_LEN + b, 0))
    return pl.pallas_call(
        _ctx_attn_kernel,
        grid=(BATCH,),
        in_specs=[ctx_spec, ctx_spec, ctx_spec, pl.BlockSpec(memory_space=pl.ANY)],
        out_specs=ctx_spec,
        out_shape=jax.ShapeDtypeStruct((N_TOT, D_MODEL), BF16),
        input_output_aliases={3: 0},
        compiler_params=pltpu.CompilerParams(dimension_semantics=("arbitrary",)),
        name="ctx_attention",
    )(q, k, v, o)


def kernel(x, c, ctx, c_ctx, w_mod, b_mod, norm_mix, norm_ffn, w_in_ab, ln_v, w_spatial,
           b_spatial, w_pool, pool_scale, w_out_ab, w_qkv, rpb, w_out_na, w_ffn_in,
           w_ffn_out, norm_final):
    cond = jnp.zeros((MOD_ROWS, D_MODEL), F32).at[:BATCH].set(c).at[CTX_MOD_ROW].set(c_ctx)
    mod3 = _adaln(cond, w_mod, b_mod).reshape(DEPTH * MOD_ROWS * N_MOD, 1, D_MODEL)

    x_lat, x_ctx = x.reshape(N_LAT, D_MODEL), ctx.reshape(N_CTX, D_MODEL)
    nf = norm_final.reshape(1, D_MODEL)
    w_in_ab, w_out_ab, w_qkv, w_out_na = (
        w.astype(BF16) for w in (w_in_ab, w_out_ab, w_qkv, w_out_na))
    for i in range(DEPTH):
        last = i == DEPTH - 1
        j = i // 2
        nm = norm_mix[i].reshape(1, D_MODEL)
        ffn_cast = _ffn_cast_io(w_ffn_in, w_ffn_out, i)
        if i % 2 == 1:
            q, k, v, w_in, w_out = _qkv(x_lat, mod3, i, nm, w_qkv, j, ffn_cast)
            y = _na_attention(q, k, v, rpb[j])
            if not last:
                y = _ctx_attention(q, k, v, y)
            w_o = w_out_na
        else:
            b_s = jnp.broadcast_to(b_spatial[j][:, :, None], (A_GROUPS, CHUNK, GROUP_DIM))
            y, w_in, w_out = _ab_mixer(x_lat, x_ctx, mod3, i, nm, w_in_ab, j, ln_v[j].reshape(1, D_A),
                                       w_spatial[j].astype(BF16), b_s, w_pool[j].astype(BF16),
                                       pool_scale[j].reshape(1, D_B), ffn_cast)
            w_o = w_out_ab
        x_lat = x_ctx = _ffn(x_lat, x_ctx, y, mod3, i, norm_ffn[i].reshape(1, D_MODEL), w_o, j,
                             w_in, w_out, nf, n_rows=N_LAT if last else N_TOT, final=last)
    return x_lat.reshape(BATCH, SEQ, D_MODEL)
```

```python
import functools
import math

import jax
import jax.numpy as jnp
from jax import lax
from jax.experimental import pallas as pl
from jax.experimental.pallas import tpu as pltpu

D_MODEL = 1024
BATCH = 4
SEQ = 4096
DEPTH = 4
GRID_W = 64
GRID_H = SEQ // GRID_W
CTX_LEN = 256
D_A = D_MODEL // 2
A_GROUPS = 4
CHUNK = 128
D_B = D_MODEL - D_A
POOL_WINDOWS = (2, 4, 8, 16)
GROUP_DIM = 128
N_HEADS = 16
HEAD_DIM = 64
NA_ROWS = 8
NA_COLS = 16
D_FF = 2816
EPS = 1e-6
LOG2E = 1.4426950408889634
NEG_INF = -1e30

N_LAT = BATCH * SEQ
N_CTX = BATCH * CTX_LEN
N_TOT = N_LAT + N_CTX
MOD_ROWS = 8
CTX_MOD_ROW = BATCH
N_MOD = 6

TM_FFN = 1024
TM_QKV = 1024
TM_AB = 1024
AB_SUB = 256
HALO = 8
FF_CHUNK = 256
CAST_STEPS = 16
MOD_TN = 1536
HEAD_PAIR = 2 * HEAD_DIM
N_HEAD_PAIRS = N_HEADS // 2
NB_KEYS = NA_ROWS * GRID_W

F32 = jnp.float32
BF16 = jnp.bfloat16


def _const_spec(shape):
    nd = len(shape)
    return pl.BlockSpec(shape, lambda *_: (0,) * nd, pipeline_mode=pl.Buffered(1))


def _layer_spec(shape, layer):
    nd = len(shape)
    return pl.BlockSpec((None,) + tuple(shape), lambda *_: (layer,) + (0,) * nd,
                        pipeline_mode=pl.Buffered(1))


def _mod_spec(layer, k, tm):
    n_lat_tiles = N_LAT // tm
    tiles_per_batch = SEQ // tm

    def index_map(t, *_):
        row = jnp.where(t < n_lat_tiles, t // tiles_per_batch, CTX_MOD_ROW)
        return ((layer * MOD_ROWS + row) * N_MOD + k, 0, 0)

    return pl.BlockSpec((1, 1, D_MODEL), index_map)


def _stream_specs(tm, split):
    if not split:
        return (pl.BlockSpec((tm, D_MODEL), lambda t: (t, 0)),
                pl.BlockSpec((HALO, D_MODEL), lambda t: (0, 0)))
    n_lat = N_LAT // tm
    return (pl.BlockSpec((tm, D_MODEL), lambda t: (jnp.minimum(t, n_lat - 1), 0)),
            pl.BlockSpec((tm, D_MODEL), lambda t: (jnp.maximum(t - n_lat, 0), 0)))


def _stream_tile(xl_ref, xc_ref, tm, split):
    if not split:
        return xl_ref[...]
    return jnp.where(pl.program_id(0) >= N_LAT // tm, xc_ref[...], xl_ref[...])


def _rms(x, g):
    return x * lax.rsqrt(jnp.mean(x * x, axis=-1, keepdims=True) + EPS) * g


def _adaln_norm(x, g, scale, shift):
    return x * lax.rsqrt(jnp.mean(x * x, axis=-1, keepdims=True) + EPS) * (g * (1.0 + scale)) + shift


def _adaln_kernel(cond_ref, w_ref, b_ref, o_ref):
    c = cond_ref[...]
    s = c * jax.nn.sigmoid(c)
    w = w_ref[0]
    s_hi = s.astype(BF16)
    s_lo = (s - s_hi.astype(F32)).astype(BF16)
    w_hi = w.astype(BF16)
    w_lo = (w - w_hi.astype(F32)).astype(BF16)
    s_parts = jnp.concatenate([s_hi, s_lo], axis=0)
    by_hi = jnp.dot(s_parts, w_hi, preferred_element_type=F32)
    by_lo = jnp.dot(s_parts, w_lo, preferred_element_type=F32)
    o_ref[0] = (by_hi[:MOD_ROWS] + by_hi[MOD_ROWS:]) + (by_lo[:MOD_ROWS] + by_lo[MOD_ROWS:]) + b_ref[0]


def _adaln(cond, w_mod, b_mod):
    return pl.pallas_call(
        _adaln_kernel,
        grid=(DEPTH, N_MOD * D_MODEL // MOD_TN),
        in_specs=[
            pl.BlockSpec((MOD_ROWS, D_MODEL), lambda l, n: (0, 0)),
            pl.BlockSpec((1, D_MODEL, MOD_TN), lambda l, n: (l, 0, n)),
            pl.BlockSpec((1, 1, MOD_TN), lambda l, n: (l, 0, n)),
        ],
        out_specs=pl.BlockSpec((1, MOD_ROWS, MOD_TN), lambda l, n: (l, 0, n)),
        out_shape=jax.ShapeDtypeStruct((DEPTH, MOD_ROWS, N_MOD * D_MODEL), F32),
        compiler_params=pltpu.CompilerParams(
            dimension_semantics=("arbitrary", "arbitrary"), vmem_limit_bytes=32 << 20),
        name="adaln",
    )(cond, w_mod, b_mod.reshape(DEPTH, 1, N_MOD * D_MODEL))


N_CAST = 3


def _ffn_cast_io(w_o, mixer_layer, w_ffn_in, w_ffn_out, layer):
    jobs = ((w_o, mixer_layer, D_MODEL, D_MODEL), (w_ffn_in, layer, D_MODEL, 2 * D_FF),
            (w_ffn_out, layer, D_FF, D_MODEL))
    in_specs, out_specs, out_shape = [], [], []
    for _, index, rows, cols in jobs:
        chunk = (None, rows // CAST_STEPS, cols)
        in_specs.append(pl.BlockSpec(
            chunk, lambda t, index=index: (index, jnp.minimum(t, CAST_STEPS - 1), 0)))
        out_specs.append(pl.BlockSpec(chunk, lambda t: (0, jnp.minimum(t, CAST_STEPS - 1), 0)))
        out_shape.append(jax.ShapeDtypeStruct((1, rows, cols), BF16))
    return in_specs, [job[0] for job in jobs], out_specs, out_shape


def _cast_chunks(src_refs, dst_refs):
    for src_ref, dst_ref in zip(src_refs, dst_refs):
        dst_ref[...] = src_ref[...].astype(BF16)


def _ffn_kernel(xl_ref, xc_ref, y_ref, g1_ref, sh_ref, sc_ref, g2_ref, nrm_ref, wo_ref, wi_ref,
                wout_ref, nf_ref, o_ref, hmid_ref, *, split, final):
    x1 = _stream_tile(xl_ref, xc_ref, TM_FFN, split) + g1_ref[0] * jnp.dot(
        y_ref[...], wo_ref[...], preferred_element_type=F32)
    h = _adaln_norm(x1, nrm_ref[...], sc_ref[0], sh_ref[0])
    hb = h.astype(BF16)
    for c in range(D_FF // FF_CHUNK):
        lo = c * FF_CHUNK
        a = jnp.dot(hb, wi_ref[:, lo:lo + FF_CHUNK], preferred_element_type=F32)
        g = jnp.dot(hb, wi_ref[:, D_FF + lo:D_FF + lo + FF_CHUNK], preferred_element_type=F32)
        hmid_ref[:, lo:lo + FF_CHUNK] = (a * jax.nn.sigmoid(a) * g).astype(BF16)
    out = x1 + g2_ref[0] * jnp.dot(hmid_ref[...], wout_ref[...], preferred_element_type=F32)
    if final:
        out = _rms(out, nf_ref[...])
    o_ref[...] = out


def _ffn(x_lat, x_ctx, y, mod3, layer, norm_ffn, w_o, w_in, w_out, norm_final, *, n_rows, final):
    tm = TM_FFN
    row_spec = pl.BlockSpec((tm, D_MODEL), lambda t: (t, 0))
    split = x_ctx is not x_lat
    lat_spec, ctx_spec = _stream_specs(tm, split)
    return pl.pallas_call(
        functools.partial(_ffn_kernel, split=split, final=final),
        grid=(n_rows // tm,),
        in_specs=[
            lat_spec, ctx_spec, row_spec,
            _mod_spec(layer, 2, tm), _mod_spec(layer, 3, tm), _mod_spec(layer, 4, tm),
            _mod_spec(layer, 5, tm),
            _const_spec((1, D_MODEL)),
            _layer_spec((D_MODEL, D_MODEL), 0),
            _layer_spec((D_MODEL, 2 * D_FF), 0),
            _layer_spec((D_FF, D_MODEL), 0),
            _const_spec((1, D_MODEL)),
        ],
        out_specs=row_spec,
        out_shape=jax.ShapeDtypeStruct((n_rows, D_MODEL), F32),
        scratch_shapes=[pltpu.VMEM((tm, D_FF), BF16)],
        compiler_params=pltpu.CompilerParams(
            dimension_semantics=("arbitrary",), vmem_limit_bytes=60 << 20),
        name="outproj_ffn",
    )(x_lat, x_ctx, y, mod3, mod3, mod3, mod3, norm_ffn, w_o, w_in, w_out, norm_final)


def _gelu_tanh(x):
    k1 = -2.0 * math.sqrt(2.0 / math.pi) * LOG2E
    return x / (1.0 + jnp.exp2(x * (k1 + (k1 * 0.044715) * (x * x))))


def _ab_kernel(xl_ref, xc_ref, xp_ref, xn_ref, sh_ref, sc_ref, nrm_ref, win_ref, lnv_ref, ws_ref,
               bs_ref, wp_ref, ps_ref, *rest, split):
    y_ref, (p_ref, a2_ref, a4_ref) = rest[N_CAST], rest[2 * N_CAST + 1:]
    _cast_chunks(rest[:N_CAST], rest[N_CAST + 1:2 * N_CAST + 1])
    x_tile = _stream_tile(xl_ref, xc_ref, TM_AB, split)
    n_sub = TM_AB // AB_SUB
    for s in range(n_sub):
        rows = slice(s * AB_SUB, (s + 1) * AB_SUB)
        prev = xp_ref[...] if s == 0 else x_tile[s * AB_SUB - HALO:s * AB_SUB]
        nxt = xn_ref[...] if s == n_sub - 1 else x_tile[(s + 1) * AB_SUB:(s + 1) * AB_SUB + HALO]
        _ab_subtile(jnp.concatenate([prev, x_tile[rows], nxt], axis=0), pl.program_id(0) * n_sub + s,
                    sh_ref, sc_ref, nrm_ref, win_ref, lnv_ref, ws_ref, bs_ref, wp_ref, ps_ref,
                    y_ref.at[rows], p_ref.at[s], a2_ref.at[s], a4_ref.at[s])


def _ab_subtile(x_all, sub_index, sh_ref, sc_ref, nrm_ref, win_ref, lnv_ref, ws_ref, bs_ref, wp_ref,
                ps_ref, y_ref, p_ref, a2_ref, a4_ref):
    tm = AB_SUB
    is_ctx = sub_index >= N_LAT // tm
    seq_len = jnp.where(is_ctx, CTX_LEN, SEQ)
    pos0 = jnp.where(is_ctx, 0, (sub_index % (SEQ // tm)) * tm)

    h = _adaln_norm(x_all, nrm_ref[...], sc_ref[0], sh_ref[0])
    z = jnp.dot(h.astype(BF16), win_ref[...], preferred_element_type=F32)

    n_p = tm + 2 * HALO
    pos_all = pos0 - HALO + lax.broadcasted_iota(jnp.int32, (n_p, D_B), 0)
    p_ref[:n_p, :] = jnp.where((pos_all >= 0) & (pos_all < seq_len), z[:, 2 * D_A:], 0.0)
    p_ref[n_p:, :] = jnp.zeros((2 * HALO, D_B), F32)

    za = _gelu_tanh(z[HALO:HALO + tm, :2 * D_A])
    u = za[:, :D_A]
    vv = za[:, D_A:]
    vc = vv - jnp.mean(vv, axis=-1, keepdims=True)
    v = (vc * lax.rsqrt(jnp.mean(vc * vc, axis=-1, keepdims=True) + EPS) * lnv_ref[...]).astype(BF16)

    n_blocks = tm // CHUNK
    for g in range(A_GROUPS):
        cs = slice(g * GROUP_DIM, (g + 1) * GROUP_DIM)
        v_cat = jnp.concatenate([v[n * CHUNK:(n + 1) * CHUNK, cs] for n in range(n_blocks)], axis=1)
        mixed = jnp.dot(ws_ref[g], v_cat, preferred_element_type=F32)
        for n in range(n_blocks):
            rs = slice(n * CHUNK, (n + 1) * CHUNK)
            y_ref[rs, cs] = (u[rs, cs] * (mixed[:, n * CHUNK:(n + 1) * CHUNK] + bs_ref[g])).astype(BF16)

    a2_ref[...] = p_ref[0:n_p + HALO, GROUP_DIM:] + p_ref[1:n_p + HALO + 1, GROUP_DIM:]
    a4_ref[...] = a2_ref[0:n_p, :] + a2_ref[2:n_p + 2, :]
    a8 = a4_ref[0:tm + HALO, 2 * GROUP_DIM:] + a4_ref[4:tm + HALO + 4, 2 * GROUP_DIM:]
    segs = (
        p_ref[HALO - 1:HALO - 1 + tm, :GROUP_DIM] + p_ref[HALO:HALO + tm, :GROUP_DIM],
        a4_ref[HALO - 2:HALO - 2 + tm, :GROUP_DIM],
        a4_ref[HALO - 4:HALO - 4 + tm, GROUP_DIM:2 * GROUP_DIM] + a4_ref[HALO:HALO + tm, GROUP_DIM:2 * GROUP_DIM],
        a8[0:tm] + a8[HALO:HALO + tm],
    )

    def window_count(first_row, half):
        pos = pos0 + first_row + lax.broadcasted_iota(jnp.int32, (HALO, GROUP_DIM), 0)
        return (jnp.minimum(pos + half, seq_len) - jnp.maximum(pos - half, 0)).astype(F32)

    for g, w in enumerate(POOL_WINDOWS):
        half = w // 2
        cs = slice(g * GROUP_DIM, (g + 1) * GROUP_DIM)
        seg = segs[g]
        pooled = jnp.concatenate([seg[:HALO] / window_count(0, half),
                                  seg[HALO:tm - HALO] * (1.0 / w),
                                  seg[tm - HALO:] / window_count(tm - HALO, half)], axis=0)
        diff = pooled - p_ref[HALO:HALO + tm, cs]
        yb = jnp.dot(diff.astype(BF16), wp_ref[g], preferred_element_type=F32)
        y_ref[:, D_A + g * GROUP_DIM:D_A + (g + 1) * GROUP_DIM] = (yb * ps_ref[:, cs]).astype(BF16)


def _ab_mixer(x_lat, x_ctx, mod3, layer, norm_mix, w_in, mixer_layer, ln_v, w_s, b_s, w_pool,
              pool_scale, ffn_cast):
    cast_in_specs, cast_inputs, cast_out_specs, cast_out_shape = ffn_cast
    tm = TM_AB
    sub = tm // HALO
    last_halo_block = x_lat.shape[0] // HALO - 1
    split = x_ctx is not x_lat
    lat_spec, ctx_spec = _stream_specs(tm, split)
    return pl.pallas_call(
        functools.partial(_ab_kernel, split=split),
        grid=(N_TOT // tm,),
        in_specs=[
            lat_spec, ctx_spec,
            pl.BlockSpec((HALO, D_MODEL), lambda t: (jnp.clip(t * sub - 1, 0, last_halo_block), 0)),
            pl.BlockSpec((HALO, D_MODEL), lambda t: (jnp.minimum((t + 1) * sub, last_halo_block), 0)),
            _mod_spec(layer, 0, tm), _mod_spec(layer, 1, tm),
            _const_spec((1, D_MODEL)),
            _layer_spec((D_MODEL, 2 * D_A + D_B), mixer_layer),
            _const_spec((1, D_A)),
            _const_spec((A_GROUPS, CHUNK, CHUNK)),
            _const_spec((A_GROUPS, CHUNK, GROUP_DIM)),
            _const_spec((A_GROUPS, GROUP_DIM, GROUP_DIM)),
            _const_spec((1, D_B)),
            *cast_in_specs,
        ],
        out_specs=[pl.BlockSpec((tm, D_MODEL), lambda t: (t, 0)), *cast_out_specs],
        out_shape=[jax.ShapeDtypeStruct((N_TOT, D_MODEL), BF16), *cast_out_shape],
        scratch_shapes=[pltpu.VMEM((tm // AB_SUB, AB_SUB + 4 * HALO, D_B), F32),
                        pltpu.VMEM((tm // AB_SUB, AB_SUB + 3 * HALO, D_B - GROUP_DIM), F32),
                        pltpu.VMEM((tm // AB_SUB, AB_SUB + 2 * HALO, D_B - GROUP_DIM), F32)],
        compiler_params=pltpu.CompilerParams(
            dimension_semantics=("arbitrary",), vmem_limit_bytes=48 << 20),
        name="ab_mixer",
    )(x_lat, x_ctx, x_lat, x_lat, mod3, mod3, norm_mix, w_in, ln_v, w_s, b_s, w_pool, pool_scale,
      *cast_inputs)


def _qkv_kernel(x_ref, sh_ref, sc_ref, nrm_ref, w_ref, *rest):
    q_ref, k_ref, v_ref = rest[N_CAST:N_CAST + 3]
    _cast_chunks(rest[:N_CAST], rest[N_CAST + 3:])
    h = _adaln_norm(x_ref[...], nrm_ref[...], sc_ref[0], sh_ref[0])
    hb = h.astype(BF16)
    scale = HEAD_DIM ** -0.5 * LOG2E
    q_ref[...] = (jnp.dot(hb, w_ref[:, :D_MODEL], preferred_element_type=F32) * scale).astype(BF16)
    k_ref[...] = jnp.dot(hb, w_ref[:, D_MODEL:2 * D_MODEL], preferred_element_type=F32).astype(BF16)
    v_ref[...] = jnp.dot(hb, w_ref[:, 2 * D_MODEL:], preferred_element_type=F32).astype(BF16)


def _qkv(xs, mod3, layer, norm_mix, w_qkv, mixer_layer, ffn_cast):
    cast_in_specs, cast_inputs, cast_out_specs, cast_out_shape = ffn_cast
    tm = TM_QKV
    row_spec = pl.BlockSpec((tm, D_MODEL), lambda t: (t, 0))
    out = jax.ShapeDtypeStruct((N_TOT, D_MODEL), BF16)
    return pl.pallas_call(
        _qkv_kernel,
        grid=(N_TOT // tm,),
        in_specs=[row_spec, _mod_spec(layer, 0, tm), _mod_spec(layer, 1, tm),
                  _const_spec((1, D_MODEL)), _layer_spec((D_MODEL, 3 * D_MODEL), mixer_layer),
                  *cast_in_specs],
        out_specs=[row_spec, row_spec, row_spec, *cast_out_specs],
        out_shape=[out, out, out, *cast_out_shape],
        compiler_params=pltpu.CompilerParams(
            dimension_semantics=("arbitrary",), vmem_limit_bytes=48 << 20),
        name="qkv_proj",
    )(xs, mod3, mod3, norm_mix, w_qkv, *cast_inputs)


_NT_DIMS = (((1,), (1,)), ((), ()))


def _head_masks(rows):
    lane = lax.broadcasted_iota(jnp.int32, (rows, HEAD_PAIR), 1)
    return lane < HEAD_DIM


N_DR_PAIRS = 2 * NA_ROWS - 2


def _bias_rows(rpb):
    lo, hi = rpb[:, :-1], rpb[:, 1:]
    gap = jnp.zeros((N_HEADS, N_DR_PAIRS, GRID_W - 2 * NA_COLS + 1), F32)
    rows = jnp.concatenate([lo[..., NA_COLS - 1:], gap, hi, gap, lo[..., :NA_COLS - 1]], axis=-1)
    return rows[:, :, None, :]


def _build_bias_pairs(rows_ref, bias_ref):
    shape = (GRID_W, HEAD_PAIR)
    c = lax.broadcasted_iota(jnp.int32, shape, 0)
    kc = lax.broadcasted_iota(jnp.int32, shape, 1) % GRID_W
    col_start = jnp.clip(c - NA_COLS // 2, 0, GRID_W - NA_COLS)
    live = (kc >= col_start) & (kc < col_start + NA_COLS)
    for d in range(N_DR_PAIRS):
        for hh in range(2):
            t = pltpu.roll(jnp.broadcast_to(rows_ref[hh, d], shape), 0, 1, stride=1, stride_axis=0)
            bias_ref[d, hh * GRID_W:(hh + 1) * GRID_W, :] = jnp.where(live, t, NEG_INF) * LOG2E


N_KEYS = NB_KEYS + CTX_LEN
ROWS_PER_STEP = 32


IMAGES_PER_STEP = 2


def _na_kernel(rows_ref, q_ref, k_ref, v_ref, kc_ref, vc_ref, o_ref,
               bias_ref, vaug, s_scr):
    @pl.when(pl.program_id(1) == 0)
    def _():
        _build_bias_pairs(rows_ref, bias_ref)

    for img in range(IMAGES_PER_STEP):
        vaug[img, :SEQ, :HEAD_PAIR] = v_ref[img * SEQ:(img + 1) * SEQ, :]
        vaug[img, SEQ:, :HEAD_PAIR] = vc_ref[img * CTX_LEN:(img + 1) * CTX_LEN, :]
        vaug[img, :, HEAD_PAIR:] = jnp.ones((SEQ + CTX_LEN, HEAD_PAIR), BF16)

    first = _head_masks(GRID_W)
    steps_per_image = GRID_H // ROWS_PER_STEP

    def locate(i, j):
        img = i // steps_per_image
        r = (i % steps_per_image) * ROWS_PER_STEP + j
        return img, r, jnp.clip(r - NA_ROWS // 2, 0, GRID_H - NA_ROWS)

    def rows(row, n):
        return pl.ds(pl.multiple_of(row * GRID_W, GRID_W), n)

    def qk_stage(i):
        for j in range(ROWS_PER_STEP):
            img, r, start = locate(i, j)
            dr0 = start - r + (NA_ROWS - 1)
            q_r = q_ref[rows(img * GRID_H + r, GRID_W), :]
            zero = jnp.zeros_like(q_r)
            qs = jnp.concatenate([jnp.where(first, q_r, zero), jnp.where(first, zero, q_r)], axis=0)
            bias = jnp.concatenate([bias_ref[dr0 + 2 * t] for t in range(NA_ROWS // 2)], axis=1)
            k_w = k_ref[rows(img * GRID_H + start, NB_KEYS), :]
            kc = kc_ref[pl.ds(pl.multiple_of(img * CTX_LEN, CTX_LEN), CTX_LEN), :]
            s_scr[i % 2, j, :, :NB_KEYS] = lax.dot_general(
                qs, k_w, _NT_DIMS, preferred_element_type=F32) + bias
            s_scr[i % 2, j, :, NB_KEYS:] = lax.dot_general(qs, kc, _NT_DIMS, preferred_element_type=F32)

    def softmax_pv_stage(i):
        for j in range(ROWS_PER_STEP):
            img, r, start = locate(i, j)
            s = s_scr[i % 2, j]
            p = jnp.exp2(s - jnp.max(s, axis=-1, keepdims=True)).astype(BF16)
            oa = (jnp.dot(p[:, :NB_KEYS], vaug[img, rows(start, NB_KEYS), :], preferred_element_type=F32)
                  + jnp.dot(p[:, NB_KEYS:], vaug[img, SEQ:, :], preferred_element_type=F32))
            o2 = oa[:, :HEAD_PAIR] / oa[:, HEAD_PAIR:]
            o_ref[rows(img * GRID_H + r, GRID_W), :] = (
                jnp.where(first, o2[:GRID_W], o2[GRID_W:]).astype(BF16))

    n_steps = IMAGES_PER_STEP * steps_per_image
    qk_stage(0)

    def step(i, carry):
        softmax_pv_stage(i - 1)
        qk_stage(i)
        return carry

    lax.fori_loop(1, n_steps, step, 0)
    softmax_pv_stage(n_steps - 1)


def _na_attention(q, k, v, rpb):
    lat_rows = IMAGES_PER_STEP * SEQ
    ctx_rows = IMAGES_PER_STEP * CTX_LEN
    lat_spec = pl.BlockSpec((lat_rows, HEAD_PAIR), lambda hp, g: (g, hp))
    ctx_spec = pl.BlockSpec((ctx_rows, HEAD_PAIR), lambda hp, g: (N_LAT // ctx_rows + g, hp))
    return pl.pallas_call(
        _na_kernel,
        grid=(N_HEAD_PAIRS, BATCH // IMAGES_PER_STEP),
        in_specs=[pl.BlockSpec((2, N_DR_PAIRS, 1, HEAD_PAIR), lambda hp, g: (hp, 0, 0, 0)),
                  lat_spec, lat_spec, lat_spec, ctx_spec, ctx_spec],
        out_specs=lat_spec,
        out_shape=jax.ShapeDtypeStruct((N_TOT, D_MODEL), BF16),
        scratch_shapes=[
            pltpu.VMEM((N_DR_PAIRS, HEAD_PAIR, HEAD_PAIR), F32),
            pltpu.VMEM((IMAGES_PER_STEP, SEQ + CTX_LEN, 2 * HEAD_PAIR), BF16),
            pltpu.VMEM((2, ROWS_PER_STEP, HEAD_PAIR, N_KEYS), F32),
        ],
        compiler_params=pltpu.CompilerParams(
            dimension_semantics=("arbitrary", "arbitrary"), vmem_limit_bytes=48 << 20),
        name="na_attention",
    )(_bias_rows(rpb), q, k, v, k, v)


def _ctx_attn_kernel(q_ref, k_ref, v_ref, o_in_ref, o_ref):
    del o_in_ref
    first = _head_masks(CTX_LEN)
    for hp in range(N_HEAD_PAIRS):
        cols = slice(hp * HEAD_PAIR, (hp + 1) * HEAD_PAIR)
        q = q_ref[:, cols]
        k = k_ref[:, cols]
        v = v_ref[:, cols]
        outs = []
        for hh in range(2):
            sel = first if hh == 0 else jnp.logical_not(first)
            qm = jnp.where(sel, q, jnp.zeros_like(q))
            s = lax.dot_general(qm, k, _NT_DIMS, preferred_element_type=F32)
            p = jnp.exp2(s - jnp.max(s, axis=-1, keepdims=True))
            denom = jnp.sum(p, axis=-1, keepdims=True)
            outs.append(jnp.dot(p.astype(BF16), v, preferred_element_type=F32) / denom)
        o_ref[:, cols] = jnp.where(first, outs[0], outs[1]).astype(BF16)


def _ctx_attention(q, k, v, o):
    ctx_spec = pl.BlockSpec((CTX_LEN, D_MODEL), lambda b: (N_LAT // CTX_LEN + b, 0))
    return pl.pallas_call(
        _ctx_attn_kernel,
        grid=(BATCH,),
        in_specs=[ctx_spec, ctx_spec, ctx_spec, pl.BlockSpec(memory_space=pl.ANY)],
        out_specs=ctx_spec,
        out_shape=jax.ShapeDtypeStruct((N_TOT, D_MODEL), BF16),
        input_output_aliases={3: 0},
        compiler_params=pltpu.CompilerParams(dimension_semantics=("arbitrary",)),
        name="ctx_attention",
    )(q, k, v, o)


def kernel(x, c, ctx, c_ctx, w_mod, b_mod, norm_mix, norm_ffn, w_in_ab, ln_v, w_spatial,
           b_spatial, w_pool, pool_scale, w_out_ab, w_qkv, rpb, w_out_na, w_ffn_in,
           w_ffn_out, norm_final):
    cond = jnp.zeros((MOD_ROWS, D_MODEL), F32).at[:BATCH].set(c).at[CTX_MOD_ROW].set(c_ctx)
    mod3 = _adaln(cond, w_mod, b_mod).reshape(DEPTH * MOD_ROWS * N_MOD, 1, D_MODEL)

    x_lat, x_ctx = x.reshape(N_LAT, D_MODEL), ctx.reshape(N_CTX, D_MODEL)
    nf = norm_final.reshape(1, D_MODEL)
    w_in_ab, w_qkv = w_in_ab.astype(BF16), w_qkv.astype(BF16)
    for i in range(DEPTH):
        last = i == DEPTH - 1
        j = i // 2
        nm = norm_mix[i].reshape(1, D_MODEL)
        ffn_cast = _ffn_cast_io(w_out_na if i % 2 == 1 else w_out_ab, j, w_ffn_in, w_ffn_out, i)
        if i % 2 == 1:
            q, k, v, w_o, w_in, w_out = _qkv(x_lat, mod3, i, nm, w_qkv, j, ffn_cast)
            y = _na_attention(q, k, v, rpb[j])
            if not last:
                y = _ctx_attention(q, k, v, y)
        else:
            b_s = jnp.broadcast_to(b_spatial[j][:, :, None], (A_GROUPS, CHUNK, GROUP_DIM))
            y, w_o, w_in, w_out = _ab_mixer(
                x_lat, x_ctx, mod3, i, nm, w_in_ab, j, ln_v[j].reshape(1, D_A),
                w_spatial[j].astype(BF16), b_s, w_pool[j].astype(BF16),
                pool_scale[j].reshape(1, D_B), ffn_cast)
        x_lat = x_ctx = _ffn(x_lat, x_ctx, y, mod3, i, norm_ffn[i].reshape(1, D_MODEL), w_o,
                             w_in, w_out, nf, n_rows=N_LAT if last else N_TOT, final=last)
    return x_lat.reshape(BATCH, SEQ, D_MODEL)
```

```python
import functools
import math

import jax
import jax.numpy as jnp
from jax import lax
from jax.experimental import pallas as pl
from jax.experimental.pallas import tpu as pltpu

D_MODEL = 1024
BATCH = 4
SEQ = 4096
DEPTH = 4
GRID_W = 64
GRID_H = SEQ // GRID_W
CTX_LEN = 256
D_A = D_MODEL // 2
A_GROUPS = 4
CHUNK = 128
D_B = D_MODEL - D_A
POOL_WINDOWS = (2, 4, 8, 16)
GROUP_DIM = 128
N_HEADS = 16
HEAD_DIM = 64
NA_ROWS = 8
NA_COLS = 16
D_FF = 2816
EPS = 1e-6
LOG2E = 1.4426950408889634
NEG_INF = -1e30

N_LAT = BATCH * SEQ
N_CTX = BATCH * CTX_LEN
N_TOT = N_LAT + N_CTX
MOD_ROWS = 8
CTX_MOD_ROW = BATCH
N_MOD = 6

TM_FFN = 1024
TM_QKV = 1024
TM_AB = 1024
AB_SUB = 256
HALO = 8
FF_CHUNK = 256
CAST_STEPS = 16
MOD_TN = 1536
HEAD_PAIR = 2 * HEAD_DIM
N_HEAD_PAIRS = N_HEADS // 2
NB_KEYS = NA_ROWS * GRID_W

F32 = jnp.float32
BF16 = jnp.bfloat16


def _const_spec(shape):
    nd = len(shape)
    return pl.BlockSpec(shape, lambda *_: (0,) * nd, pipeline_mode=pl.Buffered(1))


def _layer_spec(shape, layer):
    nd = len(shape)
    return pl.BlockSpec((None,) + tuple(shape), lambda *_: (layer,) + (0,) * nd,
                        pipeline_mode=pl.Buffered(1))


def _mod_spec(layer, k, tm):
    n_lat_tiles = N_LAT // tm
    tiles_per_batch = SEQ // tm

    def index_map(t, *_):
        row = jnp.where(t < n_lat_tiles, t // tiles_per_batch, CTX_MOD_ROW)
        return ((layer * MOD_ROWS + row) * N_MOD + k, 0, 0)

    return pl.BlockSpec((1, 1, D_MODEL), index_map)


def _stream_specs(tm, split):
    if not split:
        return (pl.BlockSpec((tm, D_MODEL), lambda t: (t, 0)),
                pl.BlockSpec((HALO, D_MODEL), lambda t: (0, 0)))
    n_lat = N_LAT // tm
    return (pl.BlockSpec((tm, D_MODEL), lambda t: (jnp.minimum(t, n_lat - 1), 0)),
            pl.BlockSpec((tm, D_MODEL), lambda t: (jnp.maximum(t - n_lat, 0), 0)))


def _stream_tile(xl_ref, xc_ref, tm, split):
    if not split:
        return xl_ref[...]
    return jnp.where(pl.program_id(0) >= N_LAT // tm, xc_ref[...], xl_ref[...])


def _rms(x, g):
    return x * lax.rsqrt(jnp.mean(x * x, axis=-1, keepdims=True) + EPS) * g


def _adaln_norm(x, g, scale, shift):
    return x * lax.rsqrt(jnp.mean(x * x, axis=-1, keepdims=True) + EPS) * (g * (1.0 + scale)) + shift


def _adaln_kernel(cond_ref, w_ref, b_ref, o_ref):
    c = cond_ref[...]
    s = c * jax.nn.sigmoid(c)
    w = w_ref[0]
    s_hi = s.astype(BF16)
    s_lo = (s - s_hi.astype(F32)).astype(BF16)
    w_hi = w.astype(BF16)
    w_lo = (w - w_hi.astype(F32)).astype(BF16)
    s_parts = jnp.concatenate([s_hi, s_lo], axis=0)
    by_hi = jnp.dot(s_parts, w_hi, preferred_element_type=F32)
    by_lo = jnp.dot(s_parts, w_lo, preferred_element_type=F32)
    o_ref[0] = (by_hi[:MOD_ROWS] + by_hi[MOD_ROWS:]) + (by_lo[:MOD_ROWS] + by_lo[MOD_ROWS:]) + b_ref[0]


def _adaln(cond, w_mod, b_mod):
    return pl.pallas_call(
        _adaln_kernel,
        grid=(DEPTH, N_MOD * D_MODEL // MOD_TN),
        in_specs=[
            pl.BlockSpec((MOD_ROWS, D_MODEL), lambda l, n: (0, 0)),
            pl.BlockSpec((1, D_MODEL, MOD_TN), lambda l, n: (l, 0, n)),
            pl.BlockSpec((1, 1, MOD_TN), lambda l, n: (l, 0, n)),
        ],
        out_specs=pl.BlockSpec((1, MOD_ROWS, MOD_TN), lambda l, n: (l, 0, n)),
        out_shape=jax.ShapeDtypeStruct((DEPTH, MOD_ROWS, N_MOD * D_MODEL), F32),
        compiler_params=pltpu.CompilerParams(
            dimension_semantics=("arbitrary", "arbitrary"), vmem_limit_bytes=32 << 20),
        name="adaln",
    )(cond, w_mod, b_mod.reshape(DEPTH, 1, N_MOD * D_MODEL))


N_CAST = 3


def _ffn_cast_io(w_o, mixer_layer, w_ffn_in, w_ffn_out, layer):
    jobs = ((w_o, mixer_layer, D_MODEL, D_MODEL), (w_ffn_in, layer, D_MODEL, 2 * D_FF),
            (w_ffn_out, layer, D_FF, D_MODEL))
    in_specs, out_specs, out_shape = [], [], []
    for _, index, rows, cols in jobs:
        chunk = (None, rows // CAST_STEPS, cols)
        in_specs.append(pl.BlockSpec(
            chunk, lambda t, index=index: (index, jnp.minimum(t, CAST_STEPS - 1), 0)))
        out_specs.append(pl.BlockSpec(chunk, lambda t: (0, jnp.minimum(t, CAST_STEPS - 1), 0)))
        out_shape.append(jax.ShapeDtypeStruct((1, rows, cols), BF16))
    return in_specs, [job[0] for job in jobs], out_specs, out_shape


def _cast_chunks(src_refs, dst_refs):
    for src_ref, dst_ref in zip(src_refs, dst_refs):
        dst_ref[...] = src_ref[...].astype(BF16)


def _ffn_kernel(xl_ref, xc_ref, y_ref, g1_ref, sh_ref, sc_ref, g2_ref, nrm_ref, wo_ref, wi_ref,
                wout_ref, nf_ref, o_ref, hmid_ref, *, split, final):
    x1 = _stream_tile(xl_ref, xc_ref, TM_FFN, split) + g1_ref[0] * jnp.dot(
        y_ref[...], wo_ref[...], preferred_element_type=F32)
    h = _adaln_norm(x1, nrm_ref[...], sc_ref[0], sh_ref[0])
    hb = h.astype(BF16)
    for c in range(D_FF // FF_CHUNK):
        lo = c * FF_CHUNK
        a = jnp.dot(hb, wi_ref[:, lo:lo + FF_CHUNK], preferred_element_type=F32)
        g = jnp.dot(hb, wi_ref[:, D_FF + lo:D_FF + lo + FF_CHUNK], preferred_element_type=F32)
        hmid_ref[:, lo:lo + FF_CHUNK] = (a * jax.nn.sigmoid(a) * g).astype(BF16)
    out = x1 + g2_ref[0] * jnp.dot(hmid_ref[...], wout_ref[...], preferred_element_type=F32)
    if final:
        out = _rms(out, nf_ref[...])
    o_ref[...] = out


def _ffn(x_lat, x_ctx, y, mod3, layer, norm_ffn, w_o, w_in, w_out, norm_final, *, n_rows, final):
    tm = TM_FFN
    row_spec = pl.BlockSpec((tm, D_MODEL), lambda t: (t, 0))
    split = x_ctx is not x_lat
    lat_spec, ctx_spec = _stream_specs(tm, split)
    return pl.pallas_call(
        functools.partial(_ffn_kernel, split=split, final=final),
        grid=(n_rows // tm,),
        in_specs=[
            lat_spec, ctx_spec, row_spec,
            _mod_spec(layer, 2, tm), _mod_spec(layer, 3, tm), _mod_spec(layer, 4, tm),
            _mod_spec(layer, 5, tm),
            _const_spec((1, D_MODEL)),
            _layer_spec((D_MODEL, D_MODEL), 0),
            _layer_spec((D_MODEL, 2 * D_FF), 0),
            _layer_spec((D_FF, D_MODEL), 0),
            _const_spec((1, D_MODEL)),
        ],
        out_specs=row_spec,
        out_shape=jax.ShapeDtypeStruct((n_rows, D_MODEL), F32),
        scratch_shapes=[pltpu.VMEM((tm, D_FF), BF16)],
        compiler_params=pltpu.CompilerParams(
            dimension_semantics=("arbitrary",), vmem_limit_bytes=60 << 20),
        name="outproj_ffn",
    )(x_lat, x_ctx, y, mod3, mod3, mod3, mod3, norm_ffn, w_o, w_in, w_out, norm_final)


def _gelu_tanh(x):
    k1 = -2.0 * math.sqrt(2.0 / math.pi) * LOG2E
    return x / (1.0 + jnp.exp2(x * (k1 + (k1 * 0.044715) * (x * x))))


def _ab_kernel(xl_ref, xc_ref, xp_ref, xn_ref, sh_ref, sc_ref, nrm_ref, win_ref, lnv_ref, ws_ref,
               bs_ref, wp_ref, ps_ref, *rest, split):
    y_ref, (p_ref, a2_ref, a4_ref) = rest[N_CAST], rest[2 * N_CAST + 1:]
    _cast_chunks(rest[:N_CAST], rest[N_CAST + 1:2 * N_CAST + 1])
    x_tile = _stream_tile(xl_ref, xc_ref, TM_AB, split)
    n_sub = TM_AB // AB_SUB
    for s in range(n_sub):
        rows = slice(s * AB_SUB, (s + 1) * AB_SUB)
        prev = xp_ref[...] if s == 0 else x_tile[s * AB_SUB - HALO:s * AB_SUB]
        nxt = xn_ref[...] if s == n_sub - 1 else x_tile[(s + 1) * AB_SUB:(s + 1) * AB_SUB + HALO]
        _ab_subtile(jnp.concatenate([prev, x_tile[rows], nxt], axis=0), pl.program_id(0) * n_sub + s,
                    sh_ref, sc_ref, nrm_ref, win_ref, lnv_ref, ws_ref, bs_ref, wp_ref, ps_ref,
                    y_ref.at[rows], p_ref.at[s], a2_ref.at[s], a4_ref.at[s])


def _ab_subtile(x_all, sub_index, sh_ref, sc_ref, nrm_ref, win_ref, lnv_ref, ws_ref, bs_ref, wp_ref,
                ps_ref, y_ref, p_ref, a2_ref, a4_ref):
    tm = AB_SUB
    is_ctx = sub_index >= N_LAT // tm
    seq_len = jnp.where(is_ctx, CTX_LEN, SEQ)
    pos0 = jnp.where(is_ctx, 0, (sub_index % (SEQ // tm)) * tm)

    h = _adaln_norm(x_all, nrm_ref[...], sc_ref[0], sh_ref[0])
    z = jnp.dot(h.astype(BF16), win_ref[...], preferred_element_type=F32)

    n_p = tm + 2 * HALO
    pos_all = pos0 - HALO + lax.broadcasted_iota(jnp.int32, (n_p, D_B), 0)
    p_ref[:n_p, :] = jnp.where((pos_all >= 0) & (pos_all < seq_len), z[:, 2 * D_A:], 0.0)
    p_ref[n_p:, :] = jnp.zeros((2 * HALO, D_B), F32)

    za = _gelu_tanh(z[HALO:HALO + tm, :2 * D_A])
    u = za[:, :D_A]
    vv = za[:, D_A:]
    vc = vv - jnp.mean(vv, axis=-1, keepdims=True)
    v = (vc * lax.rsqrt(jnp.mean(vc * vc, axis=-1, keepdims=True) + EPS) * lnv_ref[...]).astype(BF16)

    n_blocks = tm // CHUNK
    for g in range(A_GROUPS):
        cs = slice(g * GROUP_DIM, (g + 1) * GROUP_DIM)
        v_cat = jnp.concatenate([v[n * CHUNK:(n + 1) * CHUNK, cs] for n in range(n_blocks)], axis=1)
        mixed = jnp.dot(ws_ref[g], v_cat, preferred_element_type=F32)
        for n in range(n_blocks):
            rs = slice(n * CHUNK, (n + 1) * CHUNK)
            y_ref[rs, cs] = (u[rs, cs] * (mixed[:, n * CHUNK:(n + 1) * CHUNK] + bs_ref[g])).astype(BF16)

    a2_ref[...] = p_ref[0:n_p + HALO, GROUP_DIM:] + p_ref[1:n_p + HALO + 1, GROUP_DIM:]
    a4_ref[...] = a2_ref[0:n_p, :] + a2_ref[2:n_p + 2, :]
    a8 = a4_ref[0:tm + HALO, 2 * GROUP_DIM:] + a4_ref[4:tm + HALO + 4, 2 * GROUP_DIM:]
    segs = (
        p_ref[HALO - 1:HALO - 1 + tm, :GROUP_DIM] + p_ref[HALO:HALO + tm, :GROUP_DIM],
        a4_ref[HALO - 2:HALO - 2 + tm, :GROUP_DIM],
        a4_ref[HALO - 4:HALO - 4 + tm, GROUP_DIM:2 * GROUP_DIM] + a4_ref[HALO:HALO + tm, GROUP_DIM:2 * GROUP_DIM],
        a8[0:tm] + a8[HALO:HALO + tm],
    )

    def window_count(first_row, half):
        pos = pos0 + first_row + lax.broadcasted_iota(jnp.int32, (HALO, GROUP_DIM), 0)
        return (jnp.minimum(pos + half, seq_len) - jnp.maximum(pos - half, 0)).astype(F32)

    for g, w in enumerate(POOL_WINDOWS):
        half = w // 2
        cs = slice(g * GROUP_DIM, (g + 1) * GROUP_DIM)
        seg = segs[g]
        pooled = jnp.concatenate([seg[:HALO] / window_count(0, half),
                                  seg[HALO:tm - HALO] * (1.0 / w),
                                  seg[tm - HALO:] / window_count(tm - HALO, half)], axis=0)
        diff = pooled - p_ref[HALO:HALO + tm, cs]
        yb = jnp.dot(diff.astype(BF16), wp_ref[g], preferred_element_type=F32)
        y_ref[:, D_A + g * GROUP_DIM:D_A + (g + 1) * GROUP_DIM] = (yb * ps_ref[:, cs]).astype(BF16)


def _ab_mixer(x_lat, x_ctx, mod3, layer, norm_mix, w_in, mixer_layer, ln_v, w_s, b_s, w_pool,
              pool_scale, ffn_cast):
    cast_in_specs, cast_inputs, cast_out_specs, cast_out_shape = ffn_cast
    tm = TM_AB
    sub = tm // HALO
    last_halo_block = x_lat.shape[0] // HALO - 1
    split = x_ctx is not x_lat
    lat_spec, ctx_spec = _stream_specs(tm, split)
    return pl.pallas_call(
        functools.partial(_ab_kernel, split=split),
        grid=(N_TOT // tm,),
        in_specs=[
            lat_spec, ctx_spec,
            pl.BlockSpec((HALO, D_MODEL), lambda t: (jnp.clip(t * sub - 1, 0, last_halo_block), 0)),
            pl.BlockSpec((HALO, D_MODEL), lambda t: (jnp.minimum((t + 1) * sub, last_halo_block), 0)),
            _mod_spec(layer, 0, tm), _mod_spec(layer, 1, tm),
            _const_spec((1, D_MODEL)),
            _layer_spec((D_MODEL, 2 * D_A + D_B), mixer_layer),
            _const_spec((1, D_A)),
            _const_spec((A_GROUPS, CHUNK, CHUNK)),
            _const_spec((A_GROUPS, CHUNK, GROUP_DIM)),
            _const_spec((A_GROUPS, GROUP_DIM, GROUP_DIM)),
            _const_spec((1, D_B)),
            *cast_in_specs,
        ],
        out_specs=[pl.BlockSpec((tm, D_MODEL), lambda t: (t, 0)), *cast_out_specs],
        out_shape=[jax.ShapeDtypeStruct((N_TOT, D_MODEL), BF16), *cast_out_shape],
        scratch_shapes=[pltpu.VMEM((tm // AB_SUB, AB_SUB + 4 * HALO, D_B), F32),
                        pltpu.VMEM((tm // AB_SUB, AB_SUB + 3 * HALO, D_B - GROUP_DIM), F32),
                        pltpu.VMEM((tm // AB_SUB, AB_SUB + 2 * HALO, D_B - GROUP_DIM), F32)],
        compiler_params=pltpu.CompilerParams(
            dimension_semantics=("arbitrary",), vmem_limit_bytes=48 << 20),
        name="ab_mixer",
    )(x_lat, x_ctx, x_lat, x_lat, mod3, mod3, norm_mix, w_in, ln_v, w_s, b_s, w_pool, pool_scale,
      *cast_inputs)


def _qkv_kernel(x_ref, sh_ref, sc_ref, nrm_ref, w_ref, *rest):
    q_ref, k_ref, v_ref = rest[N_CAST:N_CAST + 3]
    _cast_chunks(rest[:N_CAST], rest[N_CAST + 3:])
    h = _adaln_norm(x_ref[...], nrm_ref[...], sc_ref[0], sh_ref[0])
    hb = h.astype(BF16)
    scale = HEAD_DIM ** -0.5 * LOG2E
    q_ref[...] = (jnp.dot(hb, w_ref[:, :D_MODEL], preferred_element_type=F32) * scale).astype(BF16)
    k_ref[...] = jnp.dot(hb, w_ref[:, D_MODEL:2 * D_MODEL], preferred_element_type=F32).astype(BF16)
    v_ref[...] = jnp.dot(hb, w_ref[:, 2 * D_MODEL:], preferred_element_type=F32).astype(BF16)


def _qkv(xs, mod3, layer, norm_mix, w_qkv, mixer_layer, ffn_cast):
    cast_in_specs, cast_inputs, cast_out_specs, cast_out_shape = ffn_cast
    tm = TM_QKV
    row_spec = pl.BlockSpec((tm, D_MODEL), lambda t: (t, 0))
    out = jax.ShapeDtypeStruct((N_TOT, D_MODEL), BF16)
    return pl.pallas_call(
        _qkv_kernel,
        grid=(N_TOT // tm,),
        in_specs=[row_spec, _mod_spec(layer, 0, tm), _mod_spec(layer, 1, tm),
                  _const_spec((1, D_MODEL)), _layer_spec((D_MODEL, 3 * D_MODEL), mixer_layer),
                  *cast_in_specs],
        out_specs=[row_spec, row_spec, row_spec, *cast_out_specs],
        out_shape=[out, out, out, *cast_out_shape],
        compiler_params=pltpu.CompilerParams(
            dimension_semantics=("arbitrary",), vmem_limit_bytes=48 << 20),
        name="qkv_proj",
    )(xs, mod3, mod3, norm_mix, w_qkv, *cast_inputs)


_NT_DIMS = (((1,), (1,)), ((), ()))


def _head_masks(rows):
    lane = lax.broadcasted_iota(jnp.int32, (rows, HEAD_PAIR), 1)
    return lane < HEAD_DIM


N_DR_PAIRS = 2 * NA_ROWS - 2


def _bias_rows(rpb):
    lo, hi = rpb[:, :-1], rpb[:, 1:]
    gap = jnp.zeros((N_HEADS, N_DR_PAIRS, GRID_W - 2 * NA_COLS + 1), F32)
    rows = jnp.concatenate([lo[..., NA_COLS - 1:], gap, hi, gap, lo[..., :NA_COLS - 1]], axis=-1)
    return rows[:, :, None, :]


def _build_bias_pairs(rows_ref, bias_ref):
    shape = (GRID_W, HEAD_PAIR)
    c = lax.broadcasted_iota(jnp.int32, shape, 0)
    kc = lax.broadcasted_iota(jnp.int32, shape, 1) % GRID_W
    col_start = jnp.clip(c - NA_COLS // 2, 0, GRID_W - NA_COLS)
    live = (kc >= col_start) & (kc < col_start + NA_COLS)
    for d in range(N_DR_PAIRS):
        for hh in range(2):
            t = pltpu.roll(jnp.broadcast_to(rows_ref[hh, d], shape), 0, 1, stride=1, stride_axis=0)
            bias_ref[d, hh * GRID_W:(hh + 1) * GRID_W, :] = jnp.where(live, t, NEG_INF) * LOG2E


N_KEYS = NB_KEYS + CTX_LEN
ROWS_PER_STEP = 32


IMAGES_PER_STEP = 2


def _na_kernel(rows_ref, q_ref, k_ref, v_ref, kc_ref, vc_ref, o_ref,
               bias_ref, vaug, s_scr):
    @pl.when(pl.program_id(1) == 0)
    def _():
        _build_bias_pairs(rows_ref, bias_ref)

    for img in range(IMAGES_PER_STEP):
        vaug[img, :SEQ, :HEAD_PAIR] = v_ref[img * SEQ:(img + 1) * SEQ, :]
        vaug[img, SEQ:, :HEAD_PAIR] = vc_ref[img * CTX_LEN:(img + 1) * CTX_LEN, :]
        vaug[img, :, HEAD_PAIR:] = jnp.ones((SEQ + CTX_LEN, HEAD_PAIR), BF16)

    first = _head_masks(GRID_W)
    steps_per_image = GRID_H // ROWS_PER_STEP

    def locate(i, j):
        img = i // steps_per_image
        r = (i % steps_per_image) * ROWS_PER_STEP + j
        return img, r, jnp.clip(r - NA_ROWS // 2, 0, GRID_H - NA_ROWS)

    def rows(row, n):
        return pl.ds(pl.multiple_of(row * GRID_W, GRID_W), n)

    def qk_stage(i):
        for j in range(ROWS_PER_STEP):
            img, r, start = locate(i, j)
            dr0 = start - r + (NA_ROWS - 1)
            q_r = q_ref[rows(img * GRID_H + r, GRID_W), :]
            zero = jnp.zeros_like(q_r)
            qs = jnp.concatenate([jnp.where(first, q_r, zero), jnp.where(first, zero, q_r)], axis=0)
            bias = jnp.concatenate([bias_ref[dr0 + 2 * t] for t in range(NA_ROWS // 2)], axis=1)
            k_w = k_ref[rows(img * GRID_H + start, NB_KEYS), :]
            kc = kc_ref[pl.ds(pl.multiple_of(img * CTX_LEN, CTX_LEN), CTX_LEN), :]
            s_scr[i % 2, j, :, :NB_KEYS] = lax.dot_general(
                qs, k_w, _NT_DIMS, preferred_element_type=F32) + bias
            s_scr[i % 2, j, :, NB_KEYS:] = lax.dot_general(qs, kc, _NT_DIMS, preferred_element_type=F32)

    def softmax_pv_stage(i):
        for j in range(ROWS_PER_STEP):
            img, r, start = locate(i, j)
            s = s_scr[i % 2, j]
            p = jnp.exp2((s - jnp.max(s, axis=-1, keepdims=True)).astype(BF16))
            oa = (jnp.dot(p[:, :NB_KEYS], vaug[img, rows(start, NB_KEYS), :], preferred_element_type=F32)
                  + jnp.dot(p[:, NB_KEYS:], vaug[img, SEQ:, :], preferred_element_type=F32))
            o2 = oa[:, :HEAD_PAIR] / oa[:, HEAD_PAIR:]
            o_ref[rows(img * GRID_H + r, GRID_W), :] = (
                jnp.where(first, o2[:GRID_W], o2[GRID_W:]).astype(BF16))

    n_steps = IMAGES_PER_STEP * steps_per_image
    qk_stage(0)

    def step(i, carry):
        softmax_pv_stage(i - 1)
        qk_stage(i)
        return carry

    lax.fori_loop(1, n_steps, step, 0)
    softmax_pv_stage(n_steps - 1)


def _na_attention(q, k, v, rpb):
    lat_rows = IMAGES_PER_STEP * SEQ
    ctx_rows = IMAGES_PER_STEP * CTX_LEN
    lat_spec = pl.BlockSpec((lat_rows, HEAD_PAIR), lambda hp, g: (g, hp))
    ctx_spec = pl.BlockSpec((ctx_rows, HEAD_PAIR), lambda hp, g: (N_LAT // ctx_rows + g, hp))
    return pl.pallas_call(
        _na_kernel,
        grid=(N_HEAD_PAIRS, BATCH // IMAGES_PER_STEP),
        in_specs=[pl.BlockSpec((2, N_DR_PAIRS, 1, HEAD_PAIR), lambda hp, g: (hp, 0, 0, 0)),
                  lat_spec, lat_spec, lat_spec, ctx_spec, ctx_spec],
        out_specs=lat_spec,
        out_shape=jax.ShapeDtypeStruct((N_TOT, D_MODEL), BF16),
        scratch_shapes=[
            pltpu.VMEM((N_DR_PAIRS, HEAD_PAIR, HEAD_PAIR), F32),
            pltpu.VMEM((IMAGES_PER_STEP, SEQ + CTX_LEN, 2 * HEAD_PAIR), BF16),
            pltpu.VMEM((2, ROWS_PER_STEP, HEAD_PAIR, N_KEYS), F32),
        ],
        compiler_params=pltpu.CompilerParams(
            dimension_semantics=("arbitrary", "arbitrary"), vmem_limit_bytes=48 << 20),
        name="na_attention",
    )(_bias_rows(rpb), q, k, v, k, v)


def _ctx_attn_kernel(q_ref, k_ref, v_ref, o_in_ref, o_ref):
    del o_in_ref
    first = _head_masks(CTX_LEN)
    for hp in range(N_HEAD_PAIRS):
        cols = slice(hp * HEAD_PAIR, (hp + 1) * HEAD_PAIR)
        q = q_ref[:, cols]
        k = k_ref[:, cols]
        v = v_ref[:, cols]
        outs = []
        for hh in range(2):
            sel = first if hh == 0 else jnp.logical_not(first)
            qm = jnp.where(sel, q, jnp.zeros_like(q))
            s = lax.dot_general(qm, k, _NT_DIMS, preferred_element_type=F32)
            p = jnp.exp2(s - jnp.max(s, axis=-1, keepdims=True))
            denom = jnp.sum(p, axis=-1, keepdims=True)
            outs.append(jnp.dot(p.astype(BF16), v, preferred_element_type=F32) / denom)
        o_ref[:, cols] = jnp.where(first, outs[0], outs[1]).astype(BF16)


def _ctx_attention(q, k, v, o):
    ctx_spec = pl.BlockSpec((CTX_LEN, D_MODEL), lambda b: (N_LAT // CTX_LEN + b, 0))
    return pl.pallas_call(
        _ctx_attn_kernel,
        grid=(BATCH,),
        in_specs=[ctx_spec, ctx_spec, ctx_spec, pl.BlockSpec(memory_space=pl.ANY)],
        out_specs=ctx_spec,
        out_shape=jax.ShapeDtypeStruct((N_TOT, D_MODEL), BF16),
        input_output_aliases={3: 0},
        compiler_params=pltpu.CompilerParams(dimension_semantics=("arbitrary",)),
        name="ctx_attention",
    )(q, k, v, o)


def kernel(x, c, ctx, c_ctx, w_mod, b_mod, norm_mix, norm_ffn, w_in_ab, ln_v, w_spatial,
           b_spatial, w_pool, pool_scale, w_out_ab, w_qkv, rpb, w_out_na, w_ffn_in,
           w_ffn_out, norm_final):
    cond = jnp.zeros((MOD_ROWS, D_MODEL), F32).at[:BATCH].set(c).at[CTX_MOD_ROW].set(c_ctx)
    mod3 = _adaln(cond, w_mod, b_mod).reshape(DEPTH * MOD_ROWS * N_MOD, 1, D_MODEL)

    x_lat, x_ctx = x.reshape(N_LAT, D_MODEL), ctx.reshape(N_CTX, D_MODEL)
    nf = norm_final.reshape(1, D_MODEL)
    w_in_ab, w_qkv = w_in_ab.astype(BF16), w_qkv.astype(BF16)
    for i in range(DEPTH):
        last = i == DEPTH - 1
        j = i // 2
        nm = norm_mix[i].reshape(1, D_MODEL)
        ffn_cast = _ffn_cast_io(w_out_na if i % 2 == 1 else w_out_ab, j, w_ffn_in, w_ffn_out, i)
        if i % 2 == 1:
            q, k, v, w_o, w_in, w_out = _qkv(x_lat, mod3, i, nm, w_qkv, j, ffn_cast)
            y = _na_attention(q, k, v, rpb[j])
            if not last:
                y = _ctx_attention(q, k, v, y)
        else:
            b_s = jnp.broadcast_to(b_spatial[j][:, :, None], (A_GROUPS, CHUNK, GROUP_DIM))
            y, w_o, w_in, w_out = _ab_mixer(
                x_lat, x_ctx, mod3, i, nm, w_in_ab, j, ln_v[j].reshape(1, D_A),
                w_spatial[j].astype(BF16), b_s, w_pool[j].astype(BF16),
                pool_scale[j].reshape(1, D_B), ffn_cast)
        x_lat = x_ctx = _ffn(x_lat, x_ctx, y, mod3, i, norm_ffn[i].reshape(1, D_MODEL), w_o,
                             w_in, w_out, nf, n_rows=N_LAT if last else N_TOT, final=last)
    return x_lat.reshape(BATCH, SEQ, D_MODEL)
```

```python
import functools
import math

import jax
import jax.numpy as jnp
from jax import lax
from jax.experimental import pallas as pl
from jax.experimental.pallas import tpu as pltpu

D_MODEL = 1024
BATCH = 4
SEQ = 4096
DEPTH = 4
GRID_W = 64
GRID_H = SEQ // GRID_W
CTX_LEN = 256
D_A = D_MODEL // 2
A_GROUPS = 4
CHUNK = 128
D_B = D_MODEL - D_A
POOL_WINDOWS = (2, 4, 8, 16)
GROUP_DIM = 128
N_HEADS = 16
HEAD_DIM = 64
NA_ROWS = 8
NA_COLS = 16
D_FF = 2816
EPS = 1e-6
LOG2E = 1.4426950408889634
NEG_INF = -1e30

N_LAT = BATCH * SEQ
N_CTX = BATCH * CTX_LEN
N_TOT = N_LAT + N_CTX
MOD_ROWS = 8
CTX_MOD_ROW = BATCH
N_MOD = 6

TM_FFN = 1024
TM_QKV = 1024
TM_AB = 1024
AB_SUB = 256
HALO = 8
FF_CHUNK = 256
CAST_STEPS = 16
MOD_TN = 1536
HEAD_PAIR = 2 * HEAD_DIM
N_HEAD_PAIRS = N_HEADS // 2
NB_KEYS = NA_ROWS * GRID_W

F32 = jnp.float32
BF16 = jnp.bfloat16


def _const_spec(shape):
    nd = len(shape)
    return pl.BlockSpec(shape, lambda *_: (0,) * nd, pipeline_mode=pl.Buffered(1))


def _layer_spec(shape, layer):
    nd = len(shape)
    return pl.BlockSpec((None,) + tuple(shape), lambda *_: (layer,) + (0,) * nd,
                        pipeline_mode=pl.Buffered(1))


def _mod_spec(layer, k, tm):
    n_lat_tiles = N_LAT // tm
    tiles_per_batch = SEQ // tm

    def index_map(t, *_):
        row = jnp.where(t < n_lat_tiles, t // tiles_per_batch, CTX_MOD_ROW)
        return ((layer * MOD_ROWS + row) * N_MOD + k, 0, 0)

    return pl.BlockSpec((1, 1, D_MODEL), index_map)


def _stream_specs(tm, split):
    if not split:
        return (pl.BlockSpec((tm, D_MODEL), lambda t: (t, 0)),
                pl.BlockSpec((HALO, D_MODEL), lambda t: (0, 0)))
    n_lat = N_LAT // tm
    return (pl.BlockSpec((tm, D_MODEL), lambda t: (jnp.minimum(t, n_lat - 1), 0)),
            pl.BlockSpec((tm, D_MODEL), lambda t: (jnp.maximum(t - n_lat, 0), 0)))


def _stream_tile(xl_ref, xc_ref, tm, split):
    if not split:
        return xl_ref[...]
    return jnp.where(pl.program_id(0) >= N_LAT // tm, xc_ref[...], xl_ref[...])


def _rms(x, g):
    return x * lax.rsqrt(jnp.mean(x * x, axis=-1, keepdims=True) + EPS) * g


def _adaln_norm(x, g, scale, shift):
    return x * lax.rsqrt(jnp.mean(x * x, axis=-1, keepdims=True) + EPS) * (g * (1.0 + scale)) + shift


def _adaln_kernel(cond_ref, w_ref, b_ref, o_ref):
    c = cond_ref[...]
    s = c * jax.nn.sigmoid(c)
    w = w_ref[0]
    s_hi = s.astype(BF16)
    s_lo = (s - s_hi.astype(F32)).astype(BF16)
    w_hi = w.astype(BF16)
    w_lo = (w - w_hi.astype(F32)).astype(BF16)
    s_parts = jnp.concatenate([s_hi, s_lo], axis=0)
    by_hi = jnp.dot(s_parts, w_hi, preferred_element_type=F32)
    by_lo = jnp.dot(s_parts, w_lo, preferred_element_type=F32)
    o_ref[0] = (by_hi[:MOD_ROWS] + by_hi[MOD_ROWS:]) + (by_lo[:MOD_ROWS] + by_lo[MOD_ROWS:]) + b_ref[0]


def _adaln(cond, w_mod, b_mod):
    return pl.pallas_call(
        _adaln_kernel,
        grid=(DEPTH, N_MOD * D_MODEL // MOD_TN),
        in_specs=[
            pl.BlockSpec((MOD_ROWS, D_MODEL), lambda l, n: (0, 0)),
            pl.BlockSpec((1, D_MODEL, MOD_TN), lambda l, n: (l, 0, n)),
            pl.BlockSpec((1, 1, MOD_TN), lambda l, n: (l, 0, n)),
        ],
        out_specs=pl.BlockSpec((1, MOD_ROWS, MOD_TN), lambda l, n: (l, 0, n)),
        out_shape=jax.ShapeDtypeStruct((DEPTH, MOD_ROWS, N_MOD * D_MODEL), F32),
        compiler_params=pltpu.CompilerParams(
            dimension_semantics=("arbitrary", "arbitrary"), vmem_limit_bytes=32 << 20),
        name="adaln",
    )(cond, w_mod, b_mod.reshape(DEPTH, 1, N_MOD * D_MODEL))


N_CAST = 3


def _ffn_cast_io(w_o, mixer_layer, w_ffn_in, w_ffn_out, layer):
    jobs = ((w_o, mixer_layer, D_MODEL, D_MODEL), (w_ffn_in, layer, D_MODEL, 2 * D_FF),
            (w_ffn_out, layer, D_FF, D_MODEL))
    in_specs, out_specs, out_shape = [], [], []
    for _, index, rows, cols in jobs:
        chunk = (None, rows // CAST_STEPS, cols)
        in_specs.append(pl.BlockSpec(
            chunk, lambda t, index=index: (index, jnp.minimum(t, CAST_STEPS - 1), 0)))
        out_specs.append(pl.BlockSpec(chunk, lambda t: (0, jnp.minimum(t, CAST_STEPS - 1), 0)))
        out_shape.append(jax.ShapeDtypeStruct((1, rows, cols), BF16))
    return in_specs, [job[0] for job in jobs], out_specs, out_shape


def _cast_chunks(src_refs, dst_refs):
    for src_ref, dst_ref in zip(src_refs, dst_refs):
        dst_ref[...] = src_ref[...].astype(BF16)


def _ffn_kernel(xl_ref, xc_ref, y_ref, g1_ref, sh_ref, sc_ref, g2_ref, nrm_ref, wo_ref, wi_ref,
                wout_ref, nf_ref, o_ref, hmid_ref, *, split, final):
    x1 = _stream_tile(xl_ref, xc_ref, TM_FFN, split) + g1_ref[0] * jnp.dot(
        y_ref[...], wo_ref[...], preferred_element_type=F32)
    h = _adaln_norm(x1, nrm_ref[...], sc_ref[0], sh_ref[0])
    hb = h.astype(BF16)
    for c in range(D_FF // FF_CHUNK):
        lo = c * FF_CHUNK
        a = jnp.dot(hb, wi_ref[:, lo:lo + FF_CHUNK], preferred_element_type=F32)
        g = jnp.dot(hb, wi_ref[:, D_FF + lo:D_FF + lo + FF_CHUNK], preferred_element_type=F32)
        hmid_ref[:, lo:lo + FF_CHUNK] = (a * jax.nn.sigmoid(a) * g).astype(BF16)
    out = x1 + g2_ref[0] * jnp.dot(hmid_ref[...], wout_ref[...], preferred_element_type=F32)
    if final:
        out = _rms(out, nf_ref[...])
    o_ref[...] = out


def _ffn(x_lat, x_ctx, y, mod3, layer, norm_ffn, w_o, w_in, w_out, norm_final, *, n_rows, final):
    tm = TM_FFN
    row_spec = pl.BlockSpec((tm, D_MODEL), lambda t: (t, 0))
    split = x_ctx is not x_lat
    lat_spec, ctx_spec = _stream_specs(tm, split)
    return pl.pallas_call(
        functools.partial(_ffn_kernel, split=split, final=final),
        grid=(n_rows // tm,),
        in_specs=[
            lat_spec, ctx_spec, row_spec,
            _mod_spec(layer, 2, tm), _mod_spec(layer, 3, tm), _mod_spec(layer, 4, tm),
            _mod_spec(layer, 5, tm),
            _const_spec((1, D_MODEL)),
            _layer_spec((D_MODEL, D_MODEL), 0),
            _layer_spec((D_MODEL, 2 * D_FF), 0),
            _layer_spec((D_FF, D_MODEL), 0),
            _const_spec((1, D_MODEL)),
        ],
        out_specs=row_spec,
        out_shape=jax.ShapeDtypeStruct((n_rows, D_MODEL), F32),
        scratch_shapes=[pltpu.VMEM((tm, D_FF), BF16)],
        compiler_params=pltpu.CompilerParams(
            dimension_semantics=("arbitrary",), vmem_limit_bytes=60 << 20),
        name="outproj_ffn",
    )(x_lat, x_ctx, y, mod3, mod3, mod3, mod3, norm_ffn, w_o, w_in, w_out, norm_final)


def _gelu_tanh(x):
    k1 = -2.0 * math.sqrt(2.0 / math.pi) * LOG2E
    return x / (1.0 + jnp.exp2(x * (k1 + (k1 * 0.044715) * (x * x))))


def _ab_kernel(xl_ref, xc_ref, xp_ref, xn_ref, sh_ref, sc_ref, nrm_ref, win_ref, lnv_ref, ws_ref,
               bs_ref, wp_ref, ps_ref, *rest, split):
    y_ref, (p_ref, a2_ref, a4_ref) = rest[N_CAST], rest[2 * N_CAST + 1:]
    _cast_chunks(rest[:N_CAST], rest[N_CAST + 1:2 * N_CAST + 1])
    x_tile = _stream_tile(xl_ref, xc_ref, TM_AB, split)
    n_sub = TM_AB // AB_SUB
    for s in range(n_sub):
        rows = slice(s * AB_SUB, (s + 1) * AB_SUB)
        prev = xp_ref[...] if s == 0 else x_tile[s * AB_SUB - HALO:s * AB_SUB]
        nxt = xn_ref[...] if s == n_sub - 1 else x_tile[(s + 1) * AB_SUB:(s + 1) * AB_SUB + HALO]
        _ab_subtile(jnp.concatenate([prev, x_tile[rows], nxt], axis=0), pl.program_id(0) * n_sub + s,
                    sh_ref, sc_ref, nrm_ref, win_ref, lnv_ref, ws_ref, bs_ref, wp_ref, ps_ref,
                    y_ref.at[rows], p_ref.at[s], a2_ref.at[s], a4_ref.at[s])


def _ab_subtile(x_all, sub_index, sh_ref, sc_ref, nrm_ref, win_ref, lnv_ref, ws_ref, bs_ref, wp_ref,
                ps_ref, y_ref, p_ref, a2_ref, a4_ref):
    tm = AB_SUB
    is_ctx = sub_index >= N_LAT // tm
    seq_len = jnp.where(is_ctx, CTX_LEN, SEQ)
    pos0 = jnp.where(is_ctx, 0, (sub_index % (SEQ // tm)) * tm)

    h = _adaln_norm(x_all, nrm_ref[...], sc_ref[0], sh_ref[0])
    z = jnp.dot(h.astype(BF16), win_ref[...], preferred_element_type=F32)

    n_p = tm + 2 * HALO
    pos_all = pos0 - HALO + lax.broadcasted_iota(jnp.int32, (n_p, D_B), 0)
    p_ref[:n_p, :] = jnp.where((pos_all >= 0) & (pos_all < seq_len), z[:, 2 * D_A:], 0.0)
    p_ref[n_p:, :] = jnp.zeros((2 * HALO, D_B), F32)

    za = _gelu_tanh(z[HALO:HALO + tm, :2 * D_A])
    u = za[:, :D_A]
    vv = za[:, D_A:]
    vc = vv - jnp.mean(vv, axis=-1, keepdims=True)
    v = (vc * lax.rsqrt(jnp.mean(vc * vc, axis=-1, keepdims=True) + EPS) * lnv_ref[...]).astype(BF16)

    n_blocks = tm // CHUNK
    for g in range(A_GROUPS):
        cs = slice(g * GROUP_DIM, (g + 1) * GROUP_DIM)
        v_cat = jnp.concatenate([v[n * CHUNK:(n + 1) * CHUNK, cs] for n in range(n_blocks)], axis=1)
        mixed = jnp.dot(ws_ref[g], v_cat, preferred_element_type=F32)
        for n in range(n_blocks):
            rs = slice(n * CHUNK, (n + 1) * CHUNK)
            y_ref[rs, cs] = (u[rs, cs] * (mixed[:, n * CHUNK:(n + 1) * CHUNK] + bs_ref[g])).astype(BF16)

    a2_ref[...] = p_ref[0:n_p + HALO, GROUP_DIM:] + p_ref[1:n_p + HALO + 1, GROUP_DIM:]
    a4_ref[...] = a2_ref[0:n_p, :] + a2_ref[2:n_p + 2, :]
    a8 = a4_ref[0:tm + HALO, 2 * GROUP_DIM:] + a4_ref[4:tm + HALO + 4, 2 * GROUP_DIM:]
    segs = (
        p_ref[HALO - 1:HALO - 1 + tm, :GROUP_DIM] + p_ref[HALO:HALO + tm, :GROUP_DIM],
        a4_ref[HALO - 2:HALO - 2 + tm, :GROUP_DIM],
        a4_ref[HALO - 4:HALO - 4 + tm, GROUP_DIM:2 * GROUP_DIM] + a4_ref[HALO:HALO + tm, GROUP_DIM:2 * GROUP_DIM],
        a8[0:tm] + a8[HALO:HALO + tm],
    )

    def window_count(first_row, half):
        pos = pos0 + first_row + lax.broadcasted_iota(jnp.int32, (HALO, GROUP_DIM), 0)
        return (jnp.minimum(pos + half, seq_len) - jnp.maximum(pos - half, 0)).astype(F32)

    for g, w in enumerate(POOL_WINDOWS):
        half = w // 2
        cs = slice(g * GROUP_DIM, (g + 1) * GROUP_DIM)
        seg = segs[g]
        pooled = jnp.concatenate([seg[:HALO] / window_count(0, half),
                                  seg[HALO:tm - HALO] * (1.0 / w),
                                  seg[tm - HALO:] / window_count(tm - HALO, half)], axis=0)
        diff = pooled - p_ref[HALO:HALO + tm, cs]
        yb = jnp.dot(diff.astype(BF16), wp_ref[g], preferred_element_type=F32)
        y_ref[:, D_A + g * GROUP_DIM:D_A + (g + 1) * GROUP_DIM] = (yb * ps_ref[:, cs]).astype(BF16)


def _ab_mixer(x_lat, x_ctx, mod3, layer, norm_mix, w_in, mixer_layer, ln_v, w_s, b_s, w_pool,
              pool_scale, ffn_cast):
    cast_in_specs, cast_inputs, cast_out_specs, cast_out_shape = ffn_cast
    tm = TM_AB
    sub = tm // HALO
    last_halo_block = x_lat.shape[0] // HALO - 1
    split = x_ctx is not x_lat
    lat_spec, ctx_spec = _stream_specs(tm, split)
    return pl.pallas_call(
        functools.partial(_ab_kernel, split=split),
        grid=(N_TOT // tm,),
        in_specs=[
            lat_spec, ctx_spec,
            pl.BlockSpec((HALO, D_MODEL), lambda t: (jnp.clip(t * sub - 1, 0, last_halo_block), 0)),
            pl.BlockSpec((HALO, D_MODEL), lambda t: (jnp.minimum((t + 1) * sub, last_halo_block), 0)),
            _mod_spec(layer, 0, tm), _mod_spec(layer, 1, tm),
            _const_spec((1, D_MODEL)),
            _layer_spec((D_MODEL, 2 * D_A + D_B), mixer_layer),
            _const_spec((1, D_A)),
            _const_spec((A_GROUPS, CHUNK, CHUNK)),
            _const_spec((A_GROUPS, CHUNK, GROUP_DIM)),
            _const_spec((A_GROUPS, GROUP_DIM, GROUP_DIM)),
            _const_spec((1, D_B)),
            *cast_in_specs,
        ],
        out_specs=[pl.BlockSpec((tm, D_MODEL), lambda t: (t, 0)), *cast_out_specs],
        out_shape=[jax.ShapeDtypeStruct((N_TOT, D_MODEL), BF16), *cast_out_shape],
        scratch_shapes=[pltpu.VMEM((tm // AB_SUB, AB_SUB + 4 * HALO, D_B), F32),
                        pltpu.VMEM((tm // AB_SUB, AB_SUB + 3 * HALO, D_B - GROUP_DIM), F32),
                        pltpu.VMEM((tm // AB_SUB, AB_SUB + 2 * HALO, D_B - GROUP_DIM), F32)],
        compiler_params=pltpu.CompilerParams(
            dimension_semantics=("arbitrary",), vmem_limit_bytes=48 << 20),
        name="ab_mixer",
    )(x_lat, x_ctx, x_lat, x_lat, mod3, mod3, norm_mix, w_in, ln_v, w_s, b_s, w_pool, pool_scale,
      *cast_inputs)


def _qkv_kernel(x_ref, sh_ref, sc_ref, nrm_ref, w_ref, *rest):
    q_ref, k_ref, v_ref = rest[N_CAST:N_CAST + 3]
    _cast_chunks(rest[:N_CAST], rest[N_CAST + 3:])
    h = _adaln_norm(x_ref[...], nrm_ref[...], sc_ref[0], sh_ref[0])
    hb = h.astype(BF16)
    scale = HEAD_DIM ** -0.5 * LOG2E
    q_ref[...] = (jnp.dot(hb, w_ref[:, :D_MODEL], preferred_element_type=F32) * scale).astype(BF16)
    k_ref[...] = jnp.dot(hb, w_ref[:, D_MODEL:2 * D_MODEL], preferred_element_type=F32).astype(BF16)
    v_ref[...] = jnp.dot(hb, w_ref[:, 2 * D_MODEL:], preferred_element_type=F32).astype(BF16)


def _qkv(xs, mod3, layer, norm_mix, w_qkv, mixer_layer, ffn_cast):
    cast_in_specs, cast_inputs, cast_out_specs, cast_out_shape = ffn_cast
    tm = TM_QKV
    row_spec = pl.BlockSpec((tm, D_MODEL), lambda t: (t, 0))
    out = jax.ShapeDtypeStruct((N_TOT, D_MODEL), BF16)
    return pl.pallas_call(
        _qkv_kernel,
        grid=(N_TOT // tm,),
        in_specs=[row_spec, _mod_spec(layer, 0, tm), _mod_spec(layer, 1, tm),
                  _const_spec((1, D_MODEL)), _layer_spec((D_MODEL, 3 * D_MODEL), mixer_layer),
                  *cast_in_specs],
        out_specs=[row_spec, row_spec, row_spec, *cast_out_specs],
        out_shape=[out, out, out, *cast_out_shape],
        compiler_params=pltpu.CompilerParams(
            dimension_semantics=("arbitrary",), vmem_limit_bytes=48 << 20),
        name="qkv_proj",
    )(xs, mod3, mod3, norm_mix, w_qkv, *cast_inputs)


_NT_DIMS = (((1,), (1,)), ((), ()))


def _head_masks(rows):
    lane = lax.broadcasted_iota(jnp.int32, (rows, HEAD_PAIR), 1)
    return lane < HEAD_DIM


N_DR_PAIRS = 2 * NA_ROWS - 2


def _bias_rows(rpb):
    lo, hi = rpb[:, :-1], rpb[:, 1:]
    gap = jnp.zeros((N_HEADS, N_DR_PAIRS, GRID_W - 2 * NA_COLS + 1), F32)
    rows = jnp.concatenate([lo[..., NA_COLS - 1:], gap, hi, gap, lo[..., :NA_COLS - 1]], axis=-1)
    return rows[:, :, None, :]


def _build_bias_pairs(rows_ref, bias_ref):
    shape = (GRID_W, HEAD_PAIR)
    c = lax.broadcasted_iota(jnp.int32, shape, 0)
    kc = lax.broadcasted_iota(jnp.int32, shape, 1) % GRID_W
    col_start = jnp.clip(c - NA_COLS // 2, 0, GRID_W - NA_COLS)
    live = (kc >= col_start) & (kc < col_start + NA_COLS)
    for d in range(N_DR_PAIRS):
        for hh in range(2):
            t = pltpu.roll(jnp.broadcast_to(rows_ref[hh, d], shape), 0, 1, stride=1, stride_axis=0)
            bias_ref[d, hh * GRID_W:(hh + 1) * GRID_W, :] = jnp.where(live, t, NEG_INF) * LOG2E


N_KEYS = NB_KEYS + CTX_LEN
ROWS_PER_STEP = 32


IMAGES_PER_STEP = 2


def _na_kernel(rows_ref, q_ref, k_ref, v_ref, kc_ref, vc_ref, o_ref,
               bias_ref, vaug, s_scr, m_scr):
    @pl.when(pl.program_id(1) == 0)
    def _():
        _build_bias_pairs(rows_ref, bias_ref)

    for img in range(IMAGES_PER_STEP):
        vaug[img, :SEQ, :HEAD_PAIR] = v_ref[img * SEQ:(img + 1) * SEQ, :]
        vaug[img, SEQ:, :HEAD_PAIR] = vc_ref[img * CTX_LEN:(img + 1) * CTX_LEN, :]
        vaug[img, :, HEAD_PAIR:] = jnp.ones((SEQ + CTX_LEN, HEAD_PAIR), BF16)

    first = _head_masks(GRID_W)
    steps_per_image = GRID_H // ROWS_PER_STEP

    def locate(i, j):
        img = i // steps_per_image
        r = (i % steps_per_image) * ROWS_PER_STEP + j
        return img, r, jnp.clip(r - NA_ROWS // 2, 0, GRID_H - NA_ROWS)

    def rows(row, n):
        return pl.ds(pl.multiple_of(row * GRID_W, GRID_W), n)

    def qk_stage(i):
        for j in range(ROWS_PER_STEP):
            img, r, start = locate(i, j)
            dr0 = start - r + (NA_ROWS - 1)
            q_r = q_ref[rows(img * GRID_H + r, GRID_W), :]
            zero = jnp.zeros_like(q_r)
            qs = jnp.concatenate([jnp.where(first, q_r, zero), jnp.where(first, zero, q_r)], axis=0)
            bias = jnp.concatenate([bias_ref[dr0 + 2 * t] for t in range(NA_ROWS // 2)], axis=1)
            k_w = k_ref[rows(img * GRID_H + start, NB_KEYS), :]
            kc = kc_ref[pl.ds(pl.multiple_of(img * CTX_LEN, CTX_LEN), CTX_LEN), :]
            s_nb = lax.dot_general(qs, k_w, _NT_DIMS, preferred_element_type=F32) + bias
            s_cx = lax.dot_general(qs, kc, _NT_DIMS, preferred_element_type=F32)
            s_scr[i % 2, j, :, :NB_KEYS] = s_nb
            s_scr[i % 2, j, :, NB_KEYS:] = s_cx
            groups = [s_nb[:, t * HEAD_PAIR:(t + 1) * HEAD_PAIR] for t in range(NB_KEYS // HEAD_PAIR)]
            groups += [s_cx[:, t * HEAD_PAIR:(t + 1) * HEAD_PAIR] for t in range(CTX_LEN // HEAD_PAIR)]
            m_scr[i % 2, j] = functools.reduce(jnp.maximum, groups)

    def softmax_pv_stage(i):
        for j in range(ROWS_PER_STEP):
            img, r, start = locate(i, j)
            m = jnp.max(m_scr[i % 2, j], axis=-1, keepdims=True)
            p = jnp.exp2((s_scr[i % 2, j] - m).astype(BF16))
            oa = (jnp.dot(p[:, :NB_KEYS], vaug[img, rows(start, NB_KEYS), :], preferred_element_type=F32)
                  + jnp.dot(p[:, NB_KEYS:], vaug[img, SEQ:, :], preferred_element_type=F32))
            o2 = oa[:, :HEAD_PAIR] / oa[:, HEAD_PAIR:]
            o_ref[rows(img * GRID_H + r, GRID_W), :] = (
                jnp.where(first, o2[:GRID_W], o2[GRID_W:]).astype(BF16))

    n_steps = IMAGES_PER_STEP * steps_per_image
    qk_stage(0)

    def step(i, carry):
        softmax_pv_stage(i - 1)
        qk_stage(i)
        return carry

    lax.fori_loop(1, n_steps, step, 0)
    softmax_pv_stage(n_steps - 1)


def _na_attention(q, k, v, rpb):
    lat_rows = IMAGES_PER_STEP * SEQ
    ctx_rows = IMAGES_PER_STEP * CTX_LEN
    lat_spec = pl.BlockSpec((lat_rows, HEAD_PAIR), lambda hp, g: (g, hp))
    ctx_spec = pl.BlockSpec((ctx_rows, HEAD_PAIR), lambda hp, g: (N_LAT // ctx_rows + g, hp))
    return pl.pallas_call(
        _na_kernel,
        grid=(N_HEAD_PAIRS, BATCH // IMAGES_PER_STEP),
        in_specs=[pl.BlockSpec((2, N_DR_PAIRS, 1, HEAD_PAIR), lambda hp, g: (hp, 0, 0, 0)),
                  lat_spec, lat_spec, lat_spec, ctx_spec, ctx_spec],
        out_specs=lat_spec,
        out_shape=jax.ShapeDtypeStruct((N_TOT, D_MODEL), BF16),
        scratch_shapes=[
            pltpu.VMEM((N_DR_PAIRS, HEAD_PAIR, HEAD_PAIR), F32),
            pltpu.VMEM((IMAGES_PER_STEP, SEQ + CTX_LEN, 2 * HEAD_PAIR), BF16),
            pltpu.VMEM((2, ROWS_PER_STEP, HEAD_PAIR, N_KEYS), F32),
            pltpu.VMEM((2, ROWS_PER_STEP, HEAD_PAIR, HEAD_PAIR), F32),
        ],
        compiler_params=pltpu.CompilerParams(
            dimension_semantics=("arbitrary", "arbitrary"), vmem_limit_bytes=56 << 20),
        name="na_attention",
    )(_bias_rows(rpb), q, k, v, k, v)


def _ctx_attn_kernel(q_ref, k_ref, v_ref, o_in_ref, o_ref):
    del o_in_ref
    first = _head_masks(CTX_LEN)
    for hp in range(N_HEAD_PAIRS):
        cols = slice(hp * HEAD_PAIR, (hp + 1) * HEAD_PAIR)
        q = q_ref[:, cols]
        k = k_ref[:, cols]
        v = v_ref[:, cols]
        outs = []
        for hh in range(2):
            sel = first if hh == 0 else jnp.logical_not(first)
            qm = jnp.where(sel, q, jnp.zeros_like(q))
            s = lax.dot_general(qm, k, _NT_DIMS, preferred_element_type=F32)
            p = jnp.exp2(s - jnp.max(s, axis=-1, keepdims=True))
            denom = jnp.sum(p, axis=-1, keepdims=True)
            outs.append(jnp.dot(p.astype(BF16), v, preferred_element_type=F32) / denom)
        o_ref[:, cols] = jnp.where(first, outs[0], outs[1]).astype(BF16)


def _ctx_attention(q, k, v, o):
    ctx_spec = pl.BlockSpec((CTX_LEN, D_MODEL), lambda b: (N_LAT // CTX_LEN + b, 0))
    return pl.pallas_call(
        _ctx_attn_kernel,
        grid=(BATCH,),
        in_specs=[ctx_spec, ctx_spec, ctx_spec, pl.BlockSpec(memory_space=pl.ANY)],
        out_specs=ctx_spec,
        out_shape=jax.ShapeDtypeStruct((N_TOT, D_MODEL), BF16),
        input_output_aliases={3: 0},
        compiler_params=pltpu.CompilerParams(dimension_semantics=("arbitrary",)),
        name="ctx_attention",
    )(q, k, v, o)


def kernel(x, c, ctx, c_ctx, w_mod, b_mod, norm_mix, norm_ffn, w_in_ab, ln_v, w_spatial,
           b_spatial, w_pool, pool_scale, w_out_ab, w_qkv, rpb, w_out_na, w_ffn_in,
           w_ffn_out, norm_final):
    cond = jnp.zeros((MOD_ROWS, D_MODEL), F32).at[:BATCH].set(c).at[CTX_MOD_ROW].set(c_ctx)
    mod3 = _adaln(cond, w_mod, b_mod).reshape(DEPTH * MOD_ROWS * N_MOD, 1, D_MODEL)

    x_lat, x_ctx = x.reshape(N_LAT, D_MODEL), ctx.reshape(N_CTX, D_MODEL)
    nf = norm_final.reshape(1, D_MODEL)
    w_in_ab, w_qkv = w_in_ab.astype(BF16), w_qkv.astype(BF16)
    for i in range(DEPTH):
        last = i == DEPTH - 1
        j = i // 2
        nm = norm_mix[i].reshape(1, D_MODEL)
        ffn_cast = _ffn_cast_io(w_out_na if i % 2 == 1 else w_out_ab, j, w_ffn_in, w_ffn_out, i)
        if i % 2 == 1:
            q, k, v, w_o, w_in, w_out = _qkv(x_lat, mod3, i, nm, w_qkv, j, ffn_cast)
            y = _na_attention(q, k, v, rpb[j])
            if not last:
                y = _ctx_attention(q, k, v, y)
        else:
            b_s = jnp.broadcast_to(b_spatial[j][:, :, None], (A_GROUPS, CHUNK, GROUP_DIM))
            y, w_o, w_in, w_out = _ab_mixer(
                x_lat, x_ctx, mod3, i, nm, w_in_ab, j, ln_v[j].reshape(1, D_A),
                w_spatial[j].astype(BF16), b_s, w_pool[j].astype(BF16),
                pool_scale[j].reshape(1, D_B), ffn_cast)
        x_lat = x_ctx = _ffn(x_lat, x_ctx, y, mod3, i, norm_ffn[i].reshape(1, D_MODEL), w_o,
                             w_in, w_out, nf, n_rows=N_LAT if last else N_TOT, final=last)
    return x_lat.reshape(BATCH, SEQ, D_MODEL)
```

```python
import functools
import math

import jax
import jax.numpy as jnp
from jax import lax
from jax.experimental import pallas as pl
from jax.experimental.pallas import tpu as pltpu

D_MODEL = 1024
BATCH = 4
SEQ = 4096
DEPTH = 4
GRID_W = 64
GRID_H = SEQ // GRID_W
CTX_LEN = 256
D_A = D_MODEL // 2
A_GROUPS = 4
CHUNK = 128
D_B = D_MODEL - D_A
POOL_WINDOWS = (2, 4, 8, 16)
GROUP_DIM = 128
N_HEADS = 16
HEAD_DIM = 64
NA_ROWS = 8
NA_COLS = 16
D_FF = 2816
EPS = 1e-6
LOG2E = 1.4426950408889634
NEG_INF = -1e30

N_LAT = BATCH * SEQ
N_CTX = BATCH * CTX_LEN
N_TOT = N_LAT + N_CTX
MOD_ROWS = 8
CTX_MOD_ROW = BATCH
N_MOD = 6

TM_FFN = 1024
TM_QKV = 1024
TM_AB = 1024
AB_SUB = 256
HALO = 8
FF_CHUNK = 256
CAST_STEPS = 16
MOD_TN = 1536
HEAD_PAIR = 2 * HEAD_DIM
N_HEAD_PAIRS = N_HEADS // 2
NB_KEYS = NA_ROWS * GRID_W

VMEM_LIMIT_MIB = {"adaln": 32, "ffn": 60, "mixer": 48, "attention": 56}

F32 = jnp.float32
BF16 = jnp.bfloat16


def _const_spec(shape):
    nd = len(shape)
    return pl.BlockSpec(shape, lambda *_: (0,) * nd, pipeline_mode=pl.Buffered(1))


def _layer_spec(shape, layer):
    nd = len(shape)
    return pl.BlockSpec((None,) + tuple(shape), lambda *_: (layer,) + (0,) * nd,
                        pipeline_mode=pl.Buffered(1))


def _mod_spec(layer, k, tm):
    n_lat_tiles = N_LAT // tm
    tiles_per_batch = SEQ // tm

    def index_map(t, *_):
        row = jnp.where(t < n_lat_tiles, t // tiles_per_batch, CTX_MOD_ROW)
        return ((layer * MOD_ROWS + row) * N_MOD + k, 0, 0)

    return pl.BlockSpec((1, 1, D_MODEL), index_map)


def _stream_specs(tm, split):
    if not split:
        return (pl.BlockSpec((tm, D_MODEL), lambda t: (t, 0)),
                pl.BlockSpec((HALO, D_MODEL), lambda t: (0, 0)))
    n_lat = N_LAT // tm
    return (pl.BlockSpec((tm, D_MODEL), lambda t: (jnp.minimum(t, n_lat - 1), 0)),
            pl.BlockSpec((tm, D_MODEL), lambda t: (jnp.maximum(t - n_lat, 0), 0)))


def _stream_tile(xl_ref, xc_ref, tm, split):
    if not split:
        return xl_ref[...]
    return jnp.where(pl.program_id(0) >= N_LAT // tm, xc_ref[...], xl_ref[...])


def _rms(x, g):
    return x * lax.rsqrt(jnp.mean(x * x, axis=-1, keepdims=True) + EPS) * g


def _adaln_norm(x, g, scale, shift):
    return x * lax.rsqrt(jnp.mean(x * x, axis=-1, keepdims=True) + EPS) * (g * (1.0 + scale)) + shift


def _adaln_kernel(cond_ref, w_ref, b_ref, wqkv_ref, winab_ref, o_ref, wqkv_cast_ref, winab_cast_ref):
    _cast_chunks((wqkv_ref, winab_ref), (wqkv_cast_ref, winab_cast_ref))
    c = cond_ref[...]
    s = c * jax.nn.sigmoid(c)
    w = w_ref[0]
    s_hi = s.astype(BF16)
    s_lo = (s - s_hi.astype(F32)).astype(BF16)
    w_hi = w.astype(BF16)
    w_lo = (w - w_hi.astype(F32)).astype(BF16)
    s_parts = jnp.concatenate([s_hi, s_lo], axis=0)
    by_hi = jnp.dot(s_parts, w_hi, preferred_element_type=F32)
    by_lo = jnp.dot(s_parts, w_lo, preferred_element_type=F32)
    o_ref[0] = (by_hi[:MOD_ROWS] + by_hi[MOD_ROWS:]) + (by_lo[:MOD_ROWS] + by_lo[MOD_ROWS:]) + b_ref[0]


def _adaln(cond, w_mod, b_mod, w_qkv, w_in_ab):
    n_col = N_MOD * D_MODEL // MOD_TN
    n_steps = DEPTH * n_col
    w_qkv2 = w_qkv.reshape(-1, w_qkv.shape[-1])
    w_in_ab2 = w_in_ab.reshape(-1, w_in_ab.shape[-1])

    def chunk_spec(w2):
        return pl.BlockSpec((w2.shape[0] // n_steps, w2.shape[1]), lambda l, n: (l * n_col + n, 0))

    mod, w_qkv_b, w_in_ab_b = pl.pallas_call(
        _adaln_kernel,
        grid=(DEPTH, n_col),
        in_specs=[
            pl.BlockSpec((MOD_ROWS, D_MODEL), lambda l, n: (0, 0)),
            pl.BlockSpec((1, D_MODEL, MOD_TN), lambda l, n: (l, 0, n)),
            pl.BlockSpec((1, 1, MOD_TN), lambda l, n: (l, 0, n)),
            chunk_spec(w_qkv2), chunk_spec(w_in_ab2),
        ],
        out_specs=[pl.BlockSpec((1, MOD_ROWS, MOD_TN), lambda l, n: (l, 0, n)),
                   chunk_spec(w_qkv2), chunk_spec(w_in_ab2)],
        out_shape=[jax.ShapeDtypeStruct((DEPTH, MOD_ROWS, N_MOD * D_MODEL), F32),
                   jax.ShapeDtypeStruct(w_qkv2.shape, BF16),
                   jax.ShapeDtypeStruct(w_in_ab2.shape, BF16)],
        compiler_params=pltpu.CompilerParams(
            dimension_semantics=("arbitrary", "arbitrary"),
            vmem_limit_bytes=VMEM_LIMIT_MIB["adaln"] << 20),
        name="adaln",
    )(cond, w_mod, b_mod.reshape(DEPTH, 1, N_MOD * D_MODEL), w_qkv2, w_in_ab2)
    return mod, w_qkv_b.reshape(w_qkv.shape), w_in_ab_b.reshape(w_in_ab.shape)


N_CAST = 3


def _ffn_cast_io(w_o, mixer_layer, w_ffn_in, w_ffn_out, layer):
    jobs = ((w_o, mixer_layer, D_MODEL, D_MODEL), (w_ffn_in, layer, D_MODEL, 2 * D_FF),
            (w_ffn_out, layer, D_FF, D_MODEL))
    in_specs, out_specs, out_shape = [], [], []
    for _, index, rows, cols in jobs:
        chunk = (None, rows // CAST_STEPS, cols)
        in_specs.append(pl.BlockSpec(
            chunk, lambda t, index=index: (index, jnp.minimum(t, CAST_STEPS - 1), 0)))
        out_specs.append(pl.BlockSpec(chunk, lambda t: (0, jnp.minimum(t, CAST_STEPS - 1), 0)))
        out_shape.append(jax.ShapeDtypeStruct((1, rows, cols), BF16))
    return in_specs, [job[0] for job in jobs], out_specs, out_shape


def _cast_chunks(src_refs, dst_refs):
    for src_ref, dst_ref in zip(src_refs, dst_refs):
        dst_ref[...] = src_ref[...].astype(BF16)


def _ffn_kernel(xl_ref, xc_ref, y_ref, g1_ref, sh_ref, sc_ref, g2_ref, nrm_ref, wo_ref, wi_ref,
                wout_ref, nf_ref, o_ref, hmid_ref, *, split, final):
    x1 = _stream_tile(xl_ref, xc_ref, TM_FFN, split) + g1_ref[0] * jnp.dot(
        y_ref[...], wo_ref[...], preferred_element_type=F32)
    h = _adaln_norm(x1, nrm_ref[...], sc_ref[0], sh_ref[0])
    hb = h.astype(BF16)
    for c in range(D_FF // FF_CHUNK):
        lo = c * FF_CHUNK
        a = jnp.dot(hb, wi_ref[:, lo:lo + FF_CHUNK], preferred_element_type=F32)
        g = jnp.dot(hb, wi_ref[:, D_FF + lo:D_FF + lo + FF_CHUNK], preferred_element_type=F32)
        hmid_ref[:, lo:lo + FF_CHUNK] = (a * jax.nn.sigmoid(a) * g).astype(BF16)
    out = x1 + g2_ref[0] * jnp.dot(hmid_ref[...], wout_ref[...], preferred_element_type=F32)
    if final:
        out = _rms(out, nf_ref[...])
    o_ref[...] = out


def _ffn(x_lat, x_ctx, y, mod3, layer, norm_ffn, w_o, w_in, w_out, norm_final, *, n_rows, final):
    tm = TM_FFN
    row_spec = pl.BlockSpec((tm, D_MODEL), lambda t: (t, 0))
    split = x_ctx is not x_lat
    lat_spec, ctx_spec = _stream_specs(tm, split)
    return pl.pallas_call(
        functools.partial(_ffn_kernel, split=split, final=final),
        grid=(n_rows // tm,),
        in_specs=[
            lat_spec, ctx_spec, row_spec,
            _mod_spec(layer, 2, tm), _mod_spec(layer, 3, tm), _mod_spec(layer, 4, tm),
            _mod_spec(layer, 5, tm),
            _const_spec((1, D_MODEL)),
            _layer_spec((D_MODEL, D_MODEL), 0),
            _layer_spec((D_MODEL, 2 * D_FF), 0),
            _layer_spec((D_FF, D_MODEL), 0),
            _const_spec((1, D_MODEL)),
        ],
        out_specs=row_spec,
        out_shape=jax.ShapeDtypeStruct((n_rows, D_MODEL), F32),
        scratch_shapes=[pltpu.VMEM((tm, D_FF), BF16)],
        compiler_params=pltpu.CompilerParams(
            dimension_semantics=("arbitrary",), vmem_limit_bytes=VMEM_LIMIT_MIB["ffn"] << 20),
        name="outproj_ffn",
    )(x_lat, x_ctx, y, mod3, mod3, mod3, mod3, norm_ffn, w_o, w_in, w_out, norm_final)


def _gelu_tanh(x):
    k1 = -2.0 * math.sqrt(2.0 / math.pi) * LOG2E
    return x / (1.0 + jnp.exp2(x * (k1 + (k1 * 0.044715) * (x * x))))


def _ab_kernel(xl_ref, xc_ref, xp_ref, xn_ref, sh_ref, sc_ref, nrm_ref, win_ref, lnv_ref, ws_ref,
               bs_ref, wp_ref, ps_ref, *rest, split):
    y_ref, (p_ref, a2_ref, a4_ref) = rest[N_CAST], rest[2 * N_CAST + 1:]
    _cast_chunks(rest[:N_CAST], rest[N_CAST + 1:2 * N_CAST + 1])
    x_tile = _stream_tile(xl_ref, xc_ref, TM_AB, split)
    n_sub = TM_AB // AB_SUB
    for s in range(n_sub):
        rows = slice(s * AB_SUB, (s + 1) * AB_SUB)
        prev = xp_ref[...] if s == 0 else x_tile[s * AB_SUB - HALO:s * AB_SUB]
        nxt = xn_ref[...] if s == n_sub - 1 else x_tile[(s + 1) * AB_SUB:(s + 1) * AB_SUB + HALO]
        _ab_subtile(jnp.concatenate([prev, x_tile[rows], nxt], axis=0), pl.program_id(0) * n_sub + s,
                    sh_ref, sc_ref, nrm_ref, win_ref, lnv_ref, ws_ref, bs_ref, wp_ref, ps_ref,
                    y_ref.at[rows], p_ref.at[s], a2_ref.at[s], a4_ref.at[s])


def _ab_subtile(x_all, sub_index, sh_ref, sc_ref, nrm_ref, win_ref, lnv_ref, ws_ref, bs_ref, wp_ref,
                ps_ref, y_ref, p_ref, a2_ref, a4_ref):
    tm = AB_SUB
    is_ctx = sub_index >= N_LAT // tm
    seq_len = jnp.where(is_ctx, CTX_LEN, SEQ)
    pos0 = jnp.where(is_ctx, 0, (sub_index % (SEQ // tm)) * tm)

    h = _adaln_norm(x_all, nrm_ref[...], sc_ref[0], sh_ref[0])
    z = jnp.dot(h.astype(BF16), win_ref[...], preferred_element_type=F32)

    n_p = tm + 2 * HALO
    pos_all = pos0 - HALO + lax.broadcasted_iota(jnp.int32, (n_p, D_B), 0)
    p_ref[:n_p, :] = jnp.where((pos_all >= 0) & (pos_all < seq_len), z[:, 2 * D_A:], 0.0)
    p_ref[n_p:, :] = jnp.zeros((2 * HALO, D_B), F32)

    za = _gelu_tanh(z[HALO:HALO + tm, :2 * D_A])
    u = za[:, :D_A]
    vv = za[:, D_A:]
    vc = vv - jnp.mean(vv, axis=-1, keepdims=True)
    v = (vc * lax.rsqrt(jnp.mean(vc * vc, axis=-1, keepdims=True) + EPS) * lnv_ref[...]).astype(BF16)

    n_blocks = tm // CHUNK
    for g in range(A_GROUPS):
        cs = slice(g * GROUP_DIM, (g + 1) * GROUP_DIM)
        v_cat = jnp.concatenate([v[n * CHUNK:(n + 1) * CHUNK, cs] for n in range(n_blocks)], axis=1)
        mixed = jnp.dot(ws_ref[g], v_cat, preferred_element_type=F32)
        for n in range(n_blocks):
            rs = slice(n * CHUNK, (n + 1) * CHUNK)
            y_ref[rs, cs] = (u[rs, cs] * (mixed[:, n * CHUNK:(n + 1) * CHUNK] + bs_ref[g])).astype(BF16)

    a2_ref[...] = p_ref[0:n_p + HALO, GROUP_DIM:] + p_ref[1:n_p + HALO + 1, GROUP_DIM:]
    a4_ref[...] = a2_ref[0:n_p, :] + a2_ref[2:n_p + 2, :]
    a8 = a4_ref[0:tm + HALO, 2 * GROUP_DIM:] + a4_ref[4:tm + HALO + 4, 2 * GROUP_DIM:]
    segs = (
        p_ref[HALO - 1:HALO - 1 + tm, :GROUP_DIM] + p_ref[HALO:HALO + tm, :GROUP_DIM],
        a4_ref[HALO - 2:HALO - 2 + tm, :GROUP_DIM],
        a4_ref[HALO - 4:HALO - 4 + tm, GROUP_DIM:2 * GROUP_DIM] + a4_ref[HALO:HALO + tm, GROUP_DIM:2 * GROUP_DIM],
        a8[0:tm] + a8[HALO:HALO + tm],
    )

    def window_count(first_row, half):
        pos = pos0 + first_row + lax.broadcasted_iota(jnp.int32, (HALO, GROUP_DIM), 0)
        return (jnp.minimum(pos + half, seq_len) - jnp.maximum(pos - half, 0)).astype(F32)

    for g, w in enumerate(POOL_WINDOWS):
        half = w // 2
        cs = slice(g * GROUP_DIM, (g + 1) * GROUP_DIM)
        seg = segs[g]
        pooled = jnp.concatenate([seg[:HALO] / window_count(0, half),
                                  seg[HALO:tm - HALO] * (1.0 / w),
                                  seg[tm - HALO:] / window_count(tm - HALO, half)], axis=0)
        diff = pooled - p_ref[HALO:HALO + tm, cs]
        yb = jnp.dot(diff.astype(BF16), wp_ref[g], preferred_element_type=F32)
        y_ref[:, D_A + g * GROUP_DIM:D_A + (g + 1) * GROUP_DIM] = (yb * ps_ref[:, cs]).astype(BF16)


def _ab_mixer(x_lat, x_ctx, mod3, layer, norm_mix, w_in, mixer_layer, ln_v, w_s, b_s, w_pool,
              pool_scale, ffn_cast):
    cast_in_specs, cast_inputs, cast_out_specs, cast_out_shape = ffn_cast
    tm = TM_AB
    sub = tm // HALO
    last_halo_block = x_lat.shape[0] // HALO - 1
    split = x_ctx is not x_lat
    lat_spec, ctx_spec = _stream_specs(tm, split)
    return pl.pallas_call(
        functools.partial(_ab_kernel, split=split),
        grid=(N_TOT // tm,),
        in_specs=[
            lat_spec, ctx_spec,
            pl.BlockSpec((HALO, D_MODEL), lambda t: (jnp.clip(t * sub - 1, 0, last_halo_block), 0)),
            pl.BlockSpec((HALO, D_MODEL), lambda t: (jnp.minimum((t + 1) * sub, last_halo_block), 0)),
            _mod_spec(layer, 0, tm), _mod_spec(layer, 1, tm),
            _const_spec((1, D_MODEL)),
            _layer_spec((D_MODEL, 2 * D_A + D_B), mixer_layer),
            _const_spec((1, D_A)),
            _const_spec((A_GROUPS, CHUNK, CHUNK)),
            _const_spec((A_GROUPS, CHUNK, GROUP_DIM)),
            _const_spec((A_GROUPS, GROUP_DIM, GROUP_DIM)),
            _const_spec((1, D_B)),
            *cast_in_specs,
        ],
        out_specs=[pl.BlockSpec((tm, D_MODEL), lambda t: (t, 0)), *cast_out_specs],
        out_shape=[jax.ShapeDtypeStruct((N_TOT, D_MODEL), BF16), *cast_out_shape],
        scratch_shapes=[pltpu.VMEM((tm // AB_SUB, AB_SUB + 4 * HALO, D_B), F32),
                        pltpu.VMEM((tm // AB_SUB, AB_SUB + 3 * HALO, D_B - GROUP_DIM), F32),
                        pltpu.VMEM((tm // AB_SUB, AB_SUB + 2 * HALO, D_B - GROUP_DIM), F32)],
        compiler_params=pltpu.CompilerParams(
            dimension_semantics=("arbitrary",), vmem_limit_bytes=VMEM_LIMIT_MIB["mixer"] << 20),
        name="ab_mixer",
    )(x_lat, x_ctx, x_lat, x_lat, mod3, mod3, norm_mix, w_in, ln_v, w_s, b_s, w_pool, pool_scale,
      *cast_inputs)


def _qkv_kernel(x_ref, sh_ref, sc_ref, nrm_ref, w_ref, *rest):
    q_ref, k_ref, v_ref = rest[N_CAST:N_CAST + 3]
    _cast_chunks(rest[:N_CAST], rest[N_CAST + 3:])
    h = _adaln_norm(x_ref[...], nrm_ref[...], sc_ref[0], sh_ref[0])
    hb = h.astype(BF16)
    scale = HEAD_DIM ** -0.5 * LOG2E
    q_ref[...] = (jnp.dot(hb, w_ref[:, :D_MODEL], preferred_element_type=F32) * scale).astype(BF16)
    k_ref[...] = jnp.dot(hb, w_ref[:, D_MODEL:2 * D_MODEL], preferred_element_type=F32).astype(BF16)
    v_ref[...] = jnp.dot(hb, w_ref[:, 2 * D_MODEL:], preferred_element_type=F32).astype(BF16)


def _qkv(xs, mod3, layer, norm_mix, w_qkv, mixer_layer, ffn_cast):
    cast_in_specs, cast_inputs, cast_out_specs, cast_out_shape = ffn_cast
    tm = TM_QKV
    row_spec = pl.BlockSpec((tm, D_MODEL), lambda t: (t, 0))
    out = jax.ShapeDtypeStruct((N_TOT, D_MODEL), BF16)
    return pl.pallas_call(
        _qkv_kernel,
        grid=(N_TOT // tm,),
        in_specs=[row_spec, _mod_spec(layer, 0, tm), _mod_spec(layer, 1, tm),
                  _const_spec((1, D_MODEL)), _layer_spec((D_MODEL, 3 * D_MODEL), mixer_layer),
                  *cast_in_specs],
        out_specs=[row_spec, row_spec, row_spec, *cast_out_specs],
        out_shape=[out, out, out, *cast_out_shape],
        compiler_params=pltpu.CompilerParams(
            dimension_semantics=("arbitrary",), vmem_limit_bytes=VMEM_LIMIT_MIB["mixer"] << 20),
        name="qkv_proj",
    )(xs, mod3, mod3, norm_mix, w_qkv, *cast_inputs)


_NT_DIMS = (((1,), (1,)), ((), ()))


def _head_masks(rows):
    lane = lax.broadcasted_iota(jnp.int32, (rows, HEAD_PAIR), 1)
    return lane < HEAD_DIM


N_DR_PAIRS = 2 * NA_ROWS - 2


def _bias_rows(rpb):
    lo, hi = rpb[:, :-1], rpb[:, 1:]
    gap = jnp.zeros((N_HEADS, N_DR_PAIRS, GRID_W - 2 * NA_COLS + 1), F32)
    rows = jnp.concatenate([lo[..., NA_COLS - 1:], gap, hi, gap, lo[..., :NA_COLS - 1]], axis=-1)
    return rows[:, :, None, :]


def _build_bias_pairs(rows_ref, bias_ref):
    shape = (GRID_W, HEAD_PAIR)
    c = lax.broadcasted_iota(jnp.int32, shape, 0)
    kc = lax.broadcasted_iota(jnp.int32, shape, 1) % GRID_W
    col_start = jnp.clip(c - NA_COLS // 2, 0, GRID_W - NA_COLS)
    live = (kc >= col_start) & (kc < col_start + NA_COLS)
    for d in range(N_DR_PAIRS):
        for hh in range(2):
            t = pltpu.roll(jnp.broadcast_to(rows_ref[hh, d], shape), 0, 1, stride=1, stride_axis=0)
            bias_ref[d, hh * GRID_W:(hh + 1) * GRID_W, :] = jnp.where(live, t, NEG_INF) * LOG2E


N_KEYS = NB_KEYS + CTX_LEN
ROWS_PER_STEP = 32


IMAGES_PER_STEP = 2


def _na_kernel(rows_ref, q_ref, k_ref, v_ref, kc_ref, vc_ref, o_ref,
               bias_ref, vaug, s_scr, m_scr):
    @pl.when(pl.program_id(1) == 0)
    def _():
        _build_bias_pairs(rows_ref, bias_ref)

    for img in range(IMAGES_PER_STEP):
        vaug[img, :SEQ, :HEAD_PAIR] = v_ref[img * SEQ:(img + 1) * SEQ, :]
        vaug[img, SEQ:, :HEAD_PAIR] = vc_ref[img * CTX_LEN:(img + 1) * CTX_LEN, :]
        vaug[img, :, HEAD_PAIR:] = jnp.ones((SEQ + CTX_LEN, HEAD_PAIR), BF16)

    first = _head_masks(GRID_W)
    steps_per_image = GRID_H // ROWS_PER_STEP

    def locate(i, j):
        img = i // steps_per_image
        r = (i % steps_per_image) * ROWS_PER_STEP + j
        return img, r, jnp.clip(r - NA_ROWS // 2, 0, GRID_H - NA_ROWS)

    def rows(row, n):
        return pl.ds(pl.multiple_of(row * GRID_W, GRID_W), n)

    def qk_stage(i):
        for j in range(ROWS_PER_STEP):
            img, r, start = locate(i, j)
            dr0 = start - r + (NA_ROWS - 1)
            q_r = q_ref[rows(img * GRID_H + r, GRID_W), :]
            zero = jnp.zeros_like(q_r)
            qs = jnp.concatenate([jnp.where(first, q_r, zero), jnp.where(first, zero, q_r)], axis=0)
            bias = jnp.concatenate([bias_ref[dr0 + 2 * t] for t in range(NA_ROWS // 2)], axis=1)
            k_w = k_ref[rows(img * GRID_H + start, NB_KEYS), :]
            kc = kc_ref[pl.ds(pl.multiple_of(img * CTX_LEN, CTX_LEN), CTX_LEN), :]
            s_nb = lax.dot_general(qs, k_w, _NT_DIMS, preferred_element_type=F32) + bias
            s_cx = lax.dot_general(qs, kc, _NT_DIMS, preferred_element_type=F32)
            s_scr[i % 2, j, :, :NB_KEYS] = s_nb
            s_scr[i % 2, j, :, NB_KEYS:] = s_cx
            groups = [s_nb[:, t * HEAD_PAIR:(t + 1) * HEAD_PAIR] for t in range(NB_KEYS // HEAD_PAIR)]
            groups += [s_cx[:, t * HEAD_PAIR:(t + 1) * HEAD_PAIR] for t in range(CTX_LEN // HEAD_PAIR)]
            m_scr[i % 2, j] = functools.reduce(jnp.maximum, groups)

    def softmax_pv_stage(i):
        for j in range(ROWS_PER_STEP):
            img, r, start = locate(i, j)
            m = jnp.max(m_scr[i % 2, j], axis=-1, keepdims=True)
            p = jnp.exp2((s_scr[i % 2, j] - m).astype(BF16))
            oa = (jnp.dot(p[:, :NB_KEYS], vaug[img, rows(start, NB_KEYS), :], preferred_element_type=F32)
                  + jnp.dot(p[:, NB_KEYS:], vaug[img, SEQ:, :], preferred_element_type=F32))
            o2 = oa[:, :HEAD_PAIR] / oa[:, HEAD_PAIR:]
            o_ref[rows(img * GRID_H + r, GRID_W), :] = (
                jnp.where(first, o2[:GRID_W], o2[GRID_W:]).astype(BF16))

    n_steps = IMAGES_PER_STEP * steps_per_image
    qk_stage(0)

    def step(i, carry):
        softmax_pv_stage(i - 1)
        qk_stage(i)
        return carry

    lax.fori_loop(1, n_steps, step, 0)
    softmax_pv_stage(n_steps - 1)


def _na_attention(q, k, v, rpb):
    lat_rows = IMAGES_PER_STEP * SEQ
    ctx_rows = IMAGES_PER_STEP * CTX_LEN
    lat_spec = pl.BlockSpec((lat_rows, HEAD_PAIR), lambda hp, g: (g, hp))
    ctx_spec = pl.BlockSpec((ctx_rows, HEAD_PAIR), lambda hp, g: (N_LAT // ctx_rows + g, hp))
    return pl.pallas_call(
        _na_kernel,
        grid=(N_HEAD_PAIRS, BATCH // IMAGES_PER_STEP),
        in_specs=[pl.BlockSpec((2, N_DR_PAIRS, 1, HEAD_PAIR), lambda hp, g: (hp, 0, 0, 0)),
                  lat_spec, lat_spec, lat_spec, ctx_spec, ctx_spec],
        out_specs=lat_spec,
        out_shape=jax.ShapeDtypeStruct((N_TOT, D_MODEL), BF16),
        scratch_shapes=[
            pltpu.VMEM((N_DR_PAIRS, HEAD_PAIR, HEAD_PAIR), F32),
            pltpu.VMEM((IMAGES_PER_STEP, SEQ + CTX_LEN, 2 * HEAD_PAIR), BF16),
            pltpu.VMEM((2, ROWS_PER_STEP, HEAD_PAIR, N_KEYS), F32),
            pltpu.VMEM((2, ROWS_PER_STEP, HEAD_PAIR, HEAD_PAIR), F32),
        ],
        compiler_params=pltpu.CompilerParams(
            dimension_semantics=("arbitrary", "arbitrary"),
            vmem_limit_bytes=VMEM_LIMIT_MIB["attention"] << 20),
        name="na_attention",
    )(_bias_rows(rpb), q, k, v, k, v)


def _ctx_attn_kernel(q_ref, k_ref, v_ref, o_in_ref, o_ref):
    del o_in_ref
    first = _head_masks(CTX_LEN)
    for hp in range(N_HEAD_PAIRS):
        cols = slice(hp * HEAD_PAIR, (hp + 1) * HEAD_PAIR)
        q = q_ref[:, cols]
        k = k_ref[:, cols]
        v = v_ref[:, cols]
        outs = []
        for hh in range(2):
            sel = first if hh == 0 else jnp.logical_not(first)
            qm = jnp.where(sel, q, jnp.zeros_like(q))
            s = lax.dot_general(qm, k, _NT_DIMS, preferred_element_type=F32)
            p = jnp.exp2(s - jnp.max(s, axis=-1, keepdims=True))
            denom = jnp.sum(p, axis=-1, keepdims=True)
            outs.append(jnp.dot(p.astype(BF16), v, preferred_element_type=F32) / denom)
        o_ref[:, cols] = jnp.where(first, outs[0], outs[1]).astype(BF16)


def _ctx_attention(q, k, v, o):
    ctx_spec = pl.BlockSpec((CTX_LEN, D_MODEL), lambda b: (N_LAT // CTX_LEN + b, 0))
    return pl.pallas_call(
        _ctx_attn_kernel,
        grid=(BATCH,),
        in_specs=[ctx_spec, ctx_spec, ctx_spec, pl.BlockSpec(memory_space=pl.ANY)],
        out_specs=ctx_spec,
        out_shape=jax.ShapeDtypeStruct((N_TOT, D_MODEL), BF16),
        input_output_aliases={3: 0},
        compiler_params=pltpu.CompilerParams(dimension_semantics=("arbitrary",)),
        name="ctx_attention",
    )(q, k, v, o)


def kernel(x, c, ctx, c_ctx, w_mod, b_mod, norm_mix, norm_ffn, w_in_ab, ln_v, w_spatial,
           b_spatial, w_pool, pool_scale, w_out_ab, w_qkv, rpb, w_out_na, w_ffn_in,
           w_ffn_out, norm_final):
    cond = jnp.zeros((MOD_ROWS, D_MODEL), F32).at[:BATCH].set(c).at[CTX_MOD_ROW].set(c_ctx)
    mod, w_qkv, w_in_ab = _adaln(cond, w_mod, b_mod, w_qkv, w_in_ab)
    mod3 = mod.reshape(DEPTH * MOD_ROWS * N_MOD, 1, D_MODEL)

    x_lat, x_ctx = x.reshape(N_LAT, D_MODEL), ctx.reshape(N_CTX, D_MODEL)
    nf = norm_final.reshape(1, D_MODEL)
    for i in range(DEPTH):
        last = i == DEPTH - 1
        j = i // 2
        nm = norm_mix[i].reshape(1, D_MODEL)
        ffn_cast = _ffn_cast_io(w_out_na if i % 2 == 1 else w_out_ab, j, w_ffn_in, w_ffn_out, i)
        if i % 2 == 1:
            q, k, v, w_o, w_in, w_out = _qkv(x_lat, mod3, i, nm, w_qkv, j, ffn_cast)
            y = _na_attention(q, k, v, rpb[j])
            if not last:
                y = _ctx_attention(q, k, v, y)
        else:
            b_s = jnp.broadcast_to(b_spatial[j][:, :, None], (A_GROUPS, CHUNK, GROUP_DIM))
            y, w_o, w_in, w_out = _ab_mixer(
                x_lat, x_ctx, mod3, i, nm, w_in_ab, j, ln_v[j].reshape(1, D_A),
                w_spatial[j].astype(BF16), b_s, w_pool[j].astype(BF16),
                pool_scale[j].reshape(1, D_B), ffn_cast)
        x_lat = x_ctx = _ffn(x_lat, x_ctx, y, mod3, i, norm_ffn[i].reshape(1, D_MODEL), w_o,
                             w_in, w_out, nf, n_rows=N_LAT if last else N_TOT, final=last)
    return x_lat.reshape(BATCH, SEQ, D_MODEL)
```

```python
import functools
import math

import jax
import jax.numpy as jnp
from jax import lax
from jax.experimental import pallas as pl
from jax.experimental.pallas import tpu as pltpu

D_MODEL = 1024
BATCH = 4
SEQ = 4096
DEPTH = 4
GRID_W = 64
GRID_H = SEQ // GRID_W
CTX_LEN = 256
D_A = D_MODEL // 2
A_GROUPS = 4
CHUNK = 128
D_B = D_MODEL - D_A
POOL_WINDOWS = (2, 4, 8, 16)
GROUP_DIM = 128
N_HEADS = 16
HEAD_DIM = 64
NA_ROWS = 8
NA_COLS = 16
D_FF = 2816
EPS = 1e-6
LOG2E = 1.4426950408889634
NEG_INF = -1e30

N_LAT = BATCH * SEQ
N_CTX = BATCH * CTX_LEN
N_TOT = N_LAT + N_CTX
MOD_ROWS = 8
CTX_MOD_ROW = BATCH
N_MOD = 6

TM_FFN = 1024
TM_QKV = 1024
TM_AB = 1024
AB_SUB = 256
HALO = 8
FF_CHUNK = 256
CAST_STEPS = 16
MOD_TN = 1536
HEAD_PAIR = 2 * HEAD_DIM
N_HEAD_PAIRS = N_HEADS // 2
NB_KEYS = NA_ROWS * GRID_W

VMEM_LIMIT_MIB = {"adaln": 32, "ffn": 60, "mixer": 48, "attention": 56}

F32 = jnp.float32
BF16 = jnp.bfloat16


def _const_spec(shape):
    nd = len(shape)
    return pl.BlockSpec(shape, lambda *_: (0,) * nd, pipeline_mode=pl.Buffered(1))


def _layer_spec(shape, layer):
    nd = len(shape)
    return pl.BlockSpec((None,) + tuple(shape), lambda *_: (layer,) + (0,) * nd,
                        pipeline_mode=pl.Buffered(1))


def _mod_spec(layer, k, tm):
    n_lat_tiles = N_LAT // tm
    tiles_per_batch = SEQ // tm

    def index_map(t, *_):
        row = jnp.where(t < n_lat_tiles, t // tiles_per_batch, CTX_MOD_ROW)
        return ((layer * MOD_ROWS + row) * N_MOD + k, 0, 0)

    return pl.BlockSpec((1, 1, D_MODEL), index_map)


def _stream_specs(tm, split):
    if not split:
        return (pl.BlockSpec((tm, D_MODEL), lambda t: (t, 0)),
                pl.BlockSpec((HALO, D_MODEL), lambda t: (0, 0)))
    n_lat = N_LAT // tm
    return (pl.BlockSpec((tm, D_MODEL), lambda t: (jnp.minimum(t, n_lat - 1), 0)),
            pl.BlockSpec((tm, D_MODEL), lambda t: (jnp.maximum(t - n_lat, 0), 0)))


def _stream_tile(xl_ref, xc_ref, tm, split):
    if not split:
        return xl_ref[...]
    return jnp.where(pl.program_id(0) >= N_LAT // tm, xc_ref[...], xl_ref[...])


def _rms(x, g):
    return x * lax.rsqrt(jnp.mean(x * x, axis=-1, keepdims=True) + EPS) * g


def _adaln_norm(x, g, scale, shift):
    return x * lax.rsqrt(jnp.mean(x * x, axis=-1, keepdims=True) + EPS) * (g * (1.0 + scale)) + shift


def _adaln_kernel(cond_ref, w_ref, b_ref, wqkv_ref, winab_ref, o_ref, wqkv_cast_ref, winab_cast_ref):
    _cast_chunks((wqkv_ref, winab_ref), (wqkv_cast_ref, winab_cast_ref))
    c = cond_ref[...]
    s = c * jax.nn.sigmoid(c)
    w = w_ref[0]
    s_hi = s.astype(BF16)
    s_lo = (s - s_hi.astype(F32)).astype(BF16)
    w_hi = w.astype(BF16)
    w_lo = (w - w_hi.astype(F32)).astype(BF16)
    s_parts = jnp.concatenate([s_hi, s_lo], axis=0)
    by_hi = jnp.dot(s_parts, w_hi, preferred_element_type=F32)
    by_lo = jnp.dot(s_parts, w_lo, preferred_element_type=F32)
    o_ref[0] = (by_hi[:MOD_ROWS] + by_hi[MOD_ROWS:]) + (by_lo[:MOD_ROWS] + by_lo[MOD_ROWS:]) + b_ref[0]


def _adaln(cond, w_mod, b_mod, w_qkv, w_in_ab):
    n_col = N_MOD * D_MODEL // MOD_TN
    n_steps = DEPTH * n_col
    w_qkv2 = w_qkv.reshape(-1, w_qkv.shape[-1])
    w_in_ab2 = w_in_ab.reshape(-1, w_in_ab.shape[-1])

    def chunk_spec(w2):
        assert w2.shape[0] % n_steps == 0
        return pl.BlockSpec((w2.shape[0] // n_steps, w2.shape[1]), lambda l, n: (l * n_col + n, 0))

    mod, w_qkv_b, w_in_ab_b = pl.pallas_call(
        _adaln_kernel,
        grid=(DEPTH, n_col),
        in_specs=[
            pl.BlockSpec((MOD_ROWS, D_MODEL), lambda l, n: (0, 0)),
            pl.BlockSpec((1, D_MODEL, MOD_TN), lambda l, n: (l, 0, n)),
            pl.BlockSpec((1, 1, MOD_TN), lambda l, n: (l, 0, n)),
            chunk_spec(w_qkv2), chunk_spec(w_in_ab2),
        ],
        out_specs=[pl.BlockSpec((1, MOD_ROWS, MOD_TN), lambda l, n: (l, 0, n)),
                   chunk_spec(w_qkv2), chunk_spec(w_in_ab2)],
        out_shape=[jax.ShapeDtypeStruct((DEPTH, MOD_ROWS, N_MOD * D_MODEL), F32),
                   jax.ShapeDtypeStruct(w_qkv2.shape, BF16),
                   jax.ShapeDtypeStruct(w_in_ab2.shape, BF16)],
        compiler_params=pltpu.CompilerParams(
            dimension_semantics=("arbitrary", "arbitrary"),
            vmem_limit_bytes=VMEM_LIMIT_MIB["adaln"] << 20),
        name="adaln",
    )(cond, w_mod, b_mod.reshape(DEPTH, 1, N_MOD * D_MODEL), w_qkv2, w_in_ab2)
    return mod, w_qkv_b.reshape(w_qkv.shape), w_in_ab_b.reshape(w_in_ab.shape)


N_CAST = 3


def _ffn_cast_io(w_o, mixer_layer, w_ffn_in, w_ffn_out, layer):
    jobs = ((w_o, mixer_layer, D_MODEL, D_MODEL), (w_ffn_in, layer, D_MODEL, 2 * D_FF),
            (w_ffn_out, layer, D_FF, D_MODEL))
    in_specs, out_specs, out_shape = [], [], []
    for _, index, rows, cols in jobs:
        assert rows % CAST_STEPS == 0
        chunk = (None, rows // CAST_STEPS, cols)
        in_specs.append(pl.BlockSpec(
            chunk, lambda t, index=index: (index, jnp.minimum(t, CAST_STEPS - 1), 0)))
        out_specs.append(pl.BlockSpec(chunk, lambda t: (0, jnp.minimum(t, CAST_STEPS - 1), 0)))
        out_shape.append(jax.ShapeDtypeStruct((1, rows, cols), BF16))
    return in_specs, [job[0] for job in jobs], out_specs, out_shape


def _cast_chunks(src_refs, dst_refs):
    for src_ref, dst_ref in zip(src_refs, dst_refs):
        dst_ref[...] = src_ref[...].astype(BF16)


def _ffn_kernel(xl_ref, xc_ref, y_ref, g1_ref, sh_ref, sc_ref, g2_ref, nrm_ref, wo_ref, wi_ref,
                wout_ref, nf_ref, o_ref, hmid_ref, *, split, final):
    x1 = _stream_tile(xl_ref, xc_ref, TM_FFN, split) + g1_ref[0] * jnp.dot(
        y_ref[...], wo_ref[...], preferred_element_type=F32)
    h = _adaln_norm(x1, nrm_ref[...], sc_ref[0], sh_ref[0])
    hb = h.astype(BF16)
    for c in range(D_FF // FF_CHUNK):
        lo = c * FF_CHUNK
        a = jnp.dot(hb, wi_ref[:, lo:lo + FF_CHUNK], preferred_element_type=F32)
        g = jnp.dot(hb, wi_ref[:, D_FF + lo:D_FF + lo + FF_CHUNK], preferred_element_type=F32)
        hmid_ref[:, lo:lo + FF_CHUNK] = (a * jax.nn.sigmoid(a) * g).astype(BF16)
    out = x1 + g2_ref[0] * jnp.dot(hmid_ref[...], wout_ref[...], preferred_element_type=F32)
    if final:
        out = _rms(out, nf_ref[...])
    o_ref[...] = out


def _ffn(x_lat, x_ctx, y, mod3, layer, norm_ffn, w_o, w_in, w_out, norm_final, *, n_rows, final):
    tm = TM_FFN
    row_spec = pl.BlockSpec((tm, D_MODEL), lambda t: (t, 0))
    split = x_ctx is not x_lat
    lat_spec, ctx_spec = _stream_specs(tm, split)
    return pl.pallas_call(
        functools.partial(_ffn_kernel, split=split, final=final),
        grid=(n_rows // tm,),
        in_specs=[
            lat_spec, ctx_spec, row_spec,
            _mod_spec(layer, 2, tm), _mod_spec(layer, 3, tm), _mod_spec(layer, 4, tm),
            _mod_spec(layer, 5, tm),
            _const_spec((1, D_MODEL)),
            _layer_spec((D_MODEL, D_MODEL), 0),
            _layer_spec((D_MODEL, 2 * D_FF), 0),
            _layer_spec((D_FF, D_MODEL), 0),
            _const_spec((1, D_MODEL)),
        ],
        out_specs=row_spec,
        out_shape=jax.ShapeDtypeStruct((n_rows, D_MODEL), F32),
        scratch_shapes=[pltpu.VMEM((tm, D_FF), BF16)],
        compiler_params=pltpu.CompilerParams(
            dimension_semantics=("arbitrary",), vmem_limit_bytes=VMEM_LIMIT_MIB["ffn"] << 20),
        name="outproj_ffn",
    )(x_lat, x_ctx, y, mod3, mod3, mod3, mod3, norm_ffn, w_o, w_in, w_out, norm_final)


def _gelu_tanh(x):
    k1 = -2.0 * math.sqrt(2.0 / math.pi) * LOG2E
    return x / (1.0 + jnp.exp2(x * (k1 + (k1 * 0.044715) * (x * x))))


def _ab_kernel(xl_ref, xc_ref, xp_ref, xn_ref, sh_ref, sc_ref, nrm_ref, win_ref, lnv_ref, ws_ref,
               bs_ref, wp_ref, ps_ref, *rest, split):
    y_ref, (p_ref, a2_ref, a4_ref) = rest[N_CAST], rest[2 * N_CAST + 1:]
    _cast_chunks(rest[:N_CAST], rest[N_CAST + 1:2 * N_CAST + 1])
    x_tile = _stream_tile(xl_ref, xc_ref, TM_AB, split)
    n_sub = TM_AB // AB_SUB
    for s in range(n_sub):
        rows = slice(s * AB_SUB, (s + 1) * AB_SUB)
        prev = xp_ref[...] if s == 0 else x_tile[s * AB_SUB - HALO:s * AB_SUB]
        nxt = xn_ref[...] if s == n_sub - 1 else x_tile[(s + 1) * AB_SUB:(s + 1) * AB_SUB + HALO]
        _ab_subtile(jnp.concatenate([prev, x_tile[rows], nxt], axis=0), pl.program_id(0) * n_sub + s,
                    sh_ref, sc_ref, nrm_ref, win_ref, lnv_ref, ws_ref, bs_ref, wp_ref, ps_ref,
                    y_ref.at[rows], p_ref.at[s], a2_ref.at[s], a4_ref.at[s])


def _ab_subtile(x_all, sub_index, sh_ref, sc_ref, nrm_ref, win_ref, lnv_ref, ws_ref, bs_ref, wp_ref,
                ps_ref, y_ref, p_ref, a2_ref, a4_ref):
    tm = AB_SUB
    is_ctx = sub_index >= N_LAT // tm
    seq_len = jnp.where(is_ctx, CTX_LEN, SEQ)
    pos0 = jnp.where(is_ctx, 0, (sub_index % (SEQ // tm)) * tm)

    h = _adaln_norm(x_all, nrm_ref[...], sc_ref[0], sh_ref[0])
    z = jnp.dot(h.astype(BF16), win_ref[...], preferred_element_type=F32)

    n_p = tm + 2 * HALO
    pos_all = pos0 - HALO + lax.broadcasted_iota(jnp.int32, (n_p, D_B), 0)
    p_ref[:n_p, :] = jnp.where((pos_all >= 0) & (pos_all < seq_len), z[:, 2 * D_A:], 0.0)
    p_ref[n_p:, :] = jnp.zeros((2 * HALO, D_B), F32)

    za = _gelu_tanh(z[HALO:HALO + tm, :2 * D_A])
    u = za[:, :D_A]
    vv = za[:, D_A:]
    vc = vv - jnp.mean(vv, axis=-1, keepdims=True)
    v = (vc * lax.rsqrt(jnp.mean(vc * vc, axis=-1, keepdims=True) + EPS) * lnv_ref[...]).astype(BF16)

    n_blocks = tm // CHUNK
    for g in range(A_GROUPS):
        cs = slice(g * GROUP_DIM, (g + 1) * GROUP_DIM)
        v_cat = jnp.concatenate([v[n * CHUNK:(n + 1) * CHUNK, cs] for n in range(n_blocks)], axis=1)
        mixed = jnp.dot(ws_ref[g], v_cat, preferred_element_type=F32)
        for n in range(n_blocks):
            rs = slice(n * CHUNK, (n + 1) * CHUNK)
            y_ref[rs, cs] = (u[rs, cs] * (mixed[:, n * CHUNK:(n + 1) * CHUNK] + bs_ref[g])).astype(BF16)

    a2_ref[...] = p_ref[0:n_p + HALO, GROUP_DIM:] + p_ref[1:n_p + HALO + 1, GROUP_DIM:]
    a4_ref[...] = a2_ref[0:n_p, :] + a2_ref[2:n_p + 2, :]
    a8 = a4_ref[0:tm + HALO, 2 * GROUP_DIM:] + a4_ref[4:tm + HALO + 4, 2 * GROUP_DIM:]
    segs = (
        p_ref[HALO - 1:HALO - 1 + tm, :GROUP_DIM] + p_ref[HALO:HALO + tm, :GROUP_DIM],
        a4_ref[HALO - 2:HALO - 2 + tm, :GROUP_DIM],
        a4_ref[HALO - 4:HALO - 4 + tm, GROUP_DIM:2 * GROUP_DIM] + a4_ref[HALO:HALO + tm, GROUP_DIM:2 * GROUP_DIM],
        a8[0:tm] + a8[HALO:HALO + tm],
    )

    def window_count(first_row, half):
        pos = pos0 + first_row + lax.broadcasted_iota(jnp.int32, (HALO, GROUP_DIM), 0)
        return (jnp.minimum(pos + half, seq_len) - jnp.maximum(pos - half, 0)).astype(F32)

    for g, w in enumerate(POOL_WINDOWS):
        half = w // 2
        cs = slice(g * GROUP_DIM, (g + 1) * GROUP_DIM)
        seg = segs[g]
        pooled = jnp.concatenate([seg[:HALO] / window_count(0, half),
                                  seg[HALO:tm - HALO] * (1.0 / w),
                                  seg[tm - HALO:] / window_count(tm - HALO, half)], axis=0)
        diff = pooled - p_ref[HALO:HALO + tm, cs]
        yb = jnp.dot(diff.astype(BF16), wp_ref[g], preferred_element_type=F32)
        y_ref[:, D_A + g * GROUP_DIM:D_A + (g + 1) * GROUP_DIM] = (yb * ps_ref[:, cs]).astype(BF16)


def _ab_mixer(x_lat, x_ctx, mod3, layer, norm_mix, w_in, mixer_layer, ln_v, w_s, b_s, w_pool,
              pool_scale, ffn_cast):
    cast_in_specs, cast_inputs, cast_out_specs, cast_out_shape = ffn_cast
    tm = TM_AB
    assert N_TOT // tm >= CAST_STEPS and AB_SUB == CTX_LEN and SEQ % tm == 0
    sub = tm // HALO
    last_halo_block = x_lat.shape[0] // HALO - 1
    split = x_ctx is not x_lat
    lat_spec, ctx_spec = _stream_specs(tm, split)
    return pl.pallas_call(
        functools.partial(_ab_kernel, split=split),
        grid=(N_TOT // tm,),
        in_specs=[
            lat_spec, ctx_spec,
            pl.BlockSpec((HALO, D_MODEL), lambda t: (jnp.clip(t * sub - 1, 0, last_halo_block), 0)),
            pl.BlockSpec((HALO, D_MODEL), lambda t: (jnp.minimum((t + 1) * sub, last_halo_block), 0)),
            _mod_spec(layer, 0, tm), _mod_spec(layer, 1, tm),
            _const_spec((1, D_MODEL)),
            _layer_spec((D_MODEL, 2 * D_A + D_B), mixer_layer),
            _const_spec((1, D_A)),
            _const_spec((A_GROUPS, CHUNK, CHUNK)),
            _const_spec((A_GROUPS, CHUNK, GROUP_DIM)),
            _const_spec((A_GROUPS, GROUP_DIM, GROUP_DIM)),
            _const_spec((1, D_B)),
            *cast_in_specs,
        ],
        out_specs=[pl.BlockSpec((tm, D_MODEL), lambda t: (t, 0)), *cast_out_specs],
        out_shape=[jax.ShapeDtypeStruct((N_TOT, D_MODEL), BF16), *cast_out_shape],
        scratch_shapes=[pltpu.VMEM((tm // AB_SUB, AB_SUB + 4 * HALO, D_B), F32),
                        pltpu.VMEM((tm // AB_SUB, AB_SUB + 3 * HALO, D_B - GROUP_DIM), F32),
                        pltpu.VMEM((tm // AB_SUB, AB_SUB + 2 * HALO, D_B - GROUP_DIM), F32)],
        compiler_params=pltpu.CompilerParams(
            dimension_semantics=("arbitrary",), vmem_limit_bytes=VMEM_LIMIT_MIB["mixer"] << 20),
        name="ab_mixer",
    )(x_lat, x_ctx, x_lat, x_lat, mod3, mod3, norm_mix, w_in, ln_v, w_s, b_s, w_pool, pool_scale,
      *cast_inputs)


def _qkv_kernel(x_ref, sh_ref, sc_ref, nrm_ref, w_ref, *rest):
    q_ref, k_ref, v_ref = rest[N_CAST:N_CAST + 3]
    _cast_chunks(rest[:N_CAST], rest[N_CAST + 3:])
    h = _adaln_norm(x_ref[...], nrm_ref[...], sc_ref[0], sh_ref[0])
    hb = h.astype(BF16)
    scale = HEAD_DIM ** -0.5 * LOG2E
    q_ref[...] = (jnp.dot(hb, w_ref[:, :D_MODEL], preferred_element_type=F32) * scale).astype(BF16)
    k_ref[...] = jnp.dot(hb, w_ref[:, D_MODEL:2 * D_MODEL], preferred_element_type=F32).astype(BF16)
    v_ref[...] = jnp.dot(hb, w_ref[:, 2 * D_MODEL:], preferred_element_type=F32).astype(BF16)


def _qkv(xs, mod3, layer, norm_mix, w_qkv, mixer_layer, ffn_cast):
    cast_in_specs, cast_inputs, cast_out_specs, cast_out_shape = ffn_cast
    tm = TM_QKV
    assert N_TOT // tm >= CAST_STEPS
    row_spec = pl.BlockSpec((tm, D_MODEL), lambda t: (t, 0))
    out = jax.ShapeDtypeStruct((N_TOT, D_MODEL), BF16)
    return pl.pallas_call(
        _qkv_kernel,
        grid=(N_TOT // tm,),
        in_specs=[row_spec, _mod_spec(layer, 0, tm), _mod_spec(layer, 1, tm),
                  _const_spec((1, D_MODEL)), _layer_spec((D_MODEL, 3 * D_MODEL), mixer_layer),
                  *cast_in_specs],
        out_specs=[row_spec, row_spec, row_spec, *cast_out_specs],
        out_shape=[out, out, out, *cast_out_shape],
        compiler_params=pltpu.CompilerParams(
            dimension_semantics=("arbitrary",), vmem_limit_bytes=VMEM_LIMIT_MIB["mixer"] << 20),
        name="qkv_proj",
    )(xs, mod3, mod3, norm_mix, w_qkv, *cast_inputs)


_NT_DIMS = (((1,), (1,)), ((), ()))


def _head_masks(rows):
    lane = lax.broadcasted_iota(jnp.int32, (rows, HEAD_PAIR), 1)
    return lane < HEAD_DIM


N_DR_PAIRS = 2 * NA_ROWS - 2


def _bias_rows(rpb):
    lo, hi = rpb[:, :-1], rpb[:, 1:]
    gap = jnp.zeros((N_HEADS, N_DR_PAIRS, GRID_W - 2 * NA_COLS + 1), F32)
    rows = jnp.concatenate([lo[..., NA_COLS - 1:], gap, hi, gap, lo[..., :NA_COLS - 1]], axis=-1)
    return rows[:, :, None, :]


def _build_bias_pairs(rows_ref, bias_ref):
    shape = (GRID_W, HEAD_PAIR)
    c = lax.broadcasted_iota(jnp.int32, shape, 0)
    kc = lax.broadcasted_iota(jnp.int32, shape, 1) % GRID_W
    col_start = jnp.clip(c - NA_COLS // 2, 0, GRID_W - NA_COLS)
    live = (kc >= col_start) & (kc < col_start + NA_COLS)
    for d in range(N_DR_PAIRS):
        for hh in range(2):
            t = pltpu.roll(jnp.broadcast_to(rows_ref[hh, d], shape), 0, 1, stride=1, stride_axis=0)
            bias_ref[d, hh * GRID_W:(hh + 1) * GRID_W, :] = jnp.where(live, t, NEG_INF) * LOG2E


N_KEYS = NB_KEYS + CTX_LEN
ROWS_PER_STEP = 32


IMAGES_PER_STEP = 2


def _na_kernel(rows_ref, q_ref, k_ref, v_ref, kc_ref, vc_ref, o_ref,
               bias_ref, vaug, s_scr, m_scr):
    @pl.when(pl.program_id(1) == 0)
    def _():
        _build_bias_pairs(rows_ref, bias_ref)

    for img in range(IMAGES_PER_STEP):
        vaug[img, :SEQ, :HEAD_PAIR] = v_ref[img * SEQ:(img + 1) * SEQ, :]
        vaug[img, SEQ:, :HEAD_PAIR] = vc_ref[img * CTX_LEN:(img + 1) * CTX_LEN, :]
        vaug[img, :, HEAD_PAIR:] = jnp.ones((SEQ + CTX_LEN, HEAD_PAIR), BF16)

    first = _head_masks(GRID_W)
    steps_per_image = GRID_H // ROWS_PER_STEP

    def locate(i, j):
        img = i // steps_per_image
        r = (i % steps_per_image) * ROWS_PER_STEP + j
        return img, r, jnp.clip(r - NA_ROWS // 2, 0, GRID_H - NA_ROWS)

    def rows(row, n):
        return pl.ds(pl.multiple_of(row * GRID_W, GRID_W), n)

    def qk_stage(i):
        for j in range(ROWS_PER_STEP):
            img, r, start = locate(i, j)
            dr0 = start - r + (NA_ROWS - 1)
            q_r = q_ref[rows(img * GRID_H + r, GRID_W), :]
            zero = jnp.zeros_like(q_r)
            qs = jnp.concatenate([jnp.where(first, q_r, zero), jnp.where(first, zero, q_r)], axis=0)
            bias = jnp.concatenate([bias_ref[dr0 + 2 * t] for t in range(NA_ROWS // 2)], axis=1)
            k_w = k_ref[rows(img * GRID_H + start, NB_KEYS), :]
            kc = kc_ref[pl.ds(pl.multiple_of(img * CTX_LEN, CTX_LEN), CTX_LEN), :]
            s_nb = lax.dot_general(qs, k_w, _NT_DIMS, preferred_element_type=F32) + bias
            s_cx = lax.dot_general(qs, kc, _NT_DIMS, preferred_element_type=F32)
            s_scr[i % 2, j, :, :NB_KEYS] = s_nb
            s_scr[i % 2, j, :, NB_KEYS:] = s_cx
            groups = [s_nb[:, t * HEAD_PAIR:(t + 1) * HEAD_PAIR] for t in range(NB_KEYS // HEAD_PAIR)]
            groups += [s_cx[:, t * HEAD_PAIR:(t + 1) * HEAD_PAIR] for t in range(CTX_LEN // HEAD_PAIR)]
            m_scr[i % 2, j] = functools.reduce(jnp.maximum, groups)

    def softmax_pv_stage(i):
        for j in range(ROWS_PER_STEP):
            img, r, start = locate(i, j)
            m = jnp.max(m_scr[i % 2, j], axis=-1, keepdims=True)
            p = jnp.exp2((s_scr[i % 2, j] - m).astype(BF16))
            oa = (jnp.dot(p[:, :NB_KEYS], vaug[img, rows(start, NB_KEYS), :], preferred_element_type=F32)
                  + jnp.dot(p[:, NB_KEYS:], vaug[img, SEQ:, :], preferred_element_type=F32))
            o2 = oa[:, :HEAD_PAIR] / oa[:, HEAD_PAIR:]
            o_ref[rows(img * GRID_H + r, GRID_W), :] = (
                jnp.where(first, o2[:GRID_W], o2[GRID_W:]).astype(BF16))

    n_steps = IMAGES_PER_STEP * steps_per_image
    qk_stage(0)

    def step(i, carry):
        softmax_pv_stage(i - 1)
        qk_stage(i)
        return carry

    lax.fori_loop(1, n_steps, step, 0)
    softmax_pv_stage(n_steps - 1)


def _na_attention(q, k, v, rpb):
    lat_rows = IMAGES_PER_STEP * SEQ
    ctx_rows = IMAGES_PER_STEP * CTX_LEN
    lat_spec = pl.BlockSpec((lat_rows, HEAD_PAIR), lambda hp, g: (g, hp))
    ctx_spec = pl.BlockSpec((ctx_rows, HEAD_PAIR), lambda hp, g: (N_LAT // ctx_rows + g, hp))
    return pl.pallas_call(
        _na_kernel,
        grid=(N_HEAD_PAIRS, BATCH // IMAGES_PER_STEP),
        in_specs=[pl.BlockSpec((2, N_DR_PAIRS, 1, HEAD_PAIR), lambda hp, g: (hp, 0, 0, 0)),
                  lat_spec, lat_spec, lat_spec, ctx_spec, ctx_spec],
        out_specs=lat_spec,
        out_shape=jax.ShapeDtypeStruct((N_TOT, D_MODEL), BF16),
        scratch_shapes=[
            pltpu.VMEM((N_DR_PAIRS, HEAD_PAIR, HEAD_PAIR), F32),
            pltpu.VMEM((IMAGES_PER_STEP, SEQ + CTX_LEN, 2 * HEAD_PAIR), BF16),
            pltpu.VMEM((2, ROWS_PER_STEP, HEAD_PAIR, N_KEYS), F32),
            pltpu.VMEM((2, ROWS_PER_STEP, HEAD_PAIR, HEAD_PAIR), F32),
        ],
        compiler_params=pltpu.CompilerParams(
            dimension_semantics=("arbitrary", "arbitrary"),
            vmem_limit_bytes=VMEM_LIMIT_MIB["attention"] << 20),
        name="na_attention",
    )(_bias_rows(rpb), q, k, v, k, v)


def _ctx_attn_kernel(q_ref, k_ref, v_ref, o_in_ref, o_ref):
    del o_in_ref
    first = _head_masks(CTX_LEN)
    for hp in range(N_HEAD_PAIRS):
        cols = slice(hp * HEAD_PAIR, (hp + 1) * HEAD_PAIR)
        q = q_ref[:, cols]
        k = k_ref[:, cols]
        v = v_ref[:, cols]
        outs = []
        for hh in range(2):
            sel = first if hh == 0 else jnp.logical_not(first)
            qm = jnp.where(sel, q, jnp.zeros_like(q))
            s = lax.dot_general(qm, k, _NT_DIMS, preferred_element_type=F32)
            p = jnp.exp2(s - jnp.max(s, axis=-1, keepdims=True))
            denom = jnp.sum(p, axis=-1, keepdims=True)
            outs.append(jnp.dot(p.astype(BF16), v, preferred_element_type=F32) / denom)
        o_ref[:, cols] = jnp.where(first, outs[0], outs[1]).astype(BF16)


def _ctx_attention(q, k, v, o):
    ctx_spec = pl.BlockSpec((CTX_LEN, D_MODEL), lambda b: (N_LAT // CTX_LEN + b, 0))
    return pl.pallas_call(
        _ctx_attn_kernel,
        grid=(BATCH,),
        in_specs=[ctx_spec, ctx_spec, ctx_spec, pl.BlockSpec(memory_space=pl.ANY)],
        out_specs=ctx_spec,
        out_shape=jax.ShapeDtypeStruct((N_TOT, D_MODEL), BF16),
        input_output_aliases={3: 0},
        compiler_params=pltpu.CompilerParams(dimension_semantics=("arbitrary",)),
        name="ctx_attention",
    )(q, k, v, o)


def kernel(x, c, ctx, c_ctx, w_mod, b_mod, norm_mix, norm_ffn, w_in_ab, ln_v, w_spatial,
           b_spatial, w_pool, pool_scale, w_out_ab, w_qkv, rpb, w_out_na, w_ffn_in,
           w_ffn_out, norm_final):
    cond = jnp.zeros((MOD_ROWS, D_MODEL), F32).at[:BATCH].set(c).at[CTX_MOD_ROW].set(c_ctx)
    mod, w_qkv, w_in_ab = _adaln(cond, w_mod, b_mod, w_qkv, w_in_ab)
    mod3 = mod.reshape(DEPTH * MOD_ROWS * N_MOD, 1, D_MODEL)

    x_lat, x_ctx = x.reshape(N_LAT, D_MODEL), ctx.reshape(N_CTX, D_MODEL)
    nf = norm_final.reshape(1, D_MODEL)
    for i in range(DEPTH):
        last = i == DEPTH - 1
        j = i // 2
        nm = norm_mix[i].reshape(1, D_MODEL)
        ffn_cast = _ffn_cast_io(w_out_na if i % 2 == 1 else w_out_ab, j, w_ffn_in, w_ffn_out, i)
        if i % 2 == 1:
            q, k, v, w_o, w_in, w_out = _qkv(x_lat, mod3, i, nm, w_qkv, j, ffn_cast)
            y = _na_attention(q, k, v, rpb[j])
            if not last:
                y = _ctx_attention(q, k, v, y)
        else:
            b_s = jnp.broadcast_to(b_spatial[j][:, :, None], (A_GROUPS, CHUNK, GROUP_DIM))
            y, w_o, w_in, w_out = _ab_mixer(
                x_lat, x_ctx, mod3, i, nm, w_in_ab, j, ln_v[j].reshape(1, D_A),
                w_spatial[j].astype(BF16), b_s, w_pool[j].astype(BF16),
                pool_scale[j].reshape(1, D_B), ffn_cast)
        x_lat = x_ctx = _ffn(x_lat, x_ctx, y, mod3, i, norm_ffn[i].reshape(1, D_MODEL), w_o,
                             w_in, w_out, nf, n_rows=N_LAT if last else N_TOT, final=last)
    return x_lat.reshape(BATCH, SEQ, D_MODEL)
```

```python
import functools
import math

import jax
import jax.numpy as jnp
from jax import lax
from jax.experimental import pallas as pl
from jax.experimental.pallas import tpu as pltpu

D_MODEL = 1024
BATCH = 4
SEQ = 4096
DEPTH = 4
GRID_W = 64
GRID_H = SEQ // GRID_W
CTX_LEN = 256
D_A = D_MODEL // 2
A_GROUPS = 4
CHUNK = 128
D_B = D_MODEL - D_A
POOL_WINDOWS = (2, 4, 8, 16)
GROUP_DIM = 128
N_HEADS = 16
HEAD_DIM = 64
NA_ROWS = 8
NA_COLS = 16
D_FF = 2816
EPS = 1e-6
LOG2E = 1.4426950408889634
NEG_INF = -1e30

N_LAT = BATCH * SEQ
N_CTX = BATCH * CTX_LEN
N_TOT = N_LAT + N_CTX
MOD_ROWS = 8
CTX_MOD_ROW = BATCH
N_MOD = 6

TM_FFN = 1024
TM_QKV = 1024
TM_AB = 1024
AB_SUB = 256
HALO = 8
FF_CHUNK = 256
CAST_STEPS = 16
MOD_TN = 1536
HEAD_PAIR = 2 * HEAD_DIM
N_HEAD_PAIRS = N_HEADS // 2
NB_KEYS = NA_ROWS * GRID_W

VMEM_LIMIT_MIB = {"adaln": 32, "ffn": 60, "mixer": 48, "attention": 56}

F32 = jnp.float32
BF16 = jnp.bfloat16


def _const_spec(shape):
    nd = len(shape)
    return pl.BlockSpec(shape, lambda *_: (0,) * nd, pipeline_mode=pl.Buffered(1))


def _layer_spec(shape, layer):
    nd = len(shape)
    return pl.BlockSpec((None,) + tuple(shape), lambda *_: (layer,) + (0,) * nd,
                        pipeline_mode=pl.Buffered(1))


def _mod_spec(layer, k, tm):
    n_lat_tiles = N_LAT // tm
    tiles_per_batch = SEQ // tm

    def index_map(t, *_):
        row = jnp.where(t < n_lat_tiles, t // tiles_per_batch, CTX_MOD_ROW)
        return ((layer * MOD_ROWS + row) * N_MOD + k, 0, 0)

    return pl.BlockSpec((1, 1, D_MODEL), index_map)


def _stream_specs(tm, split):
    if not split:
        return (pl.BlockSpec((tm, D_MODEL), lambda t: (t, 0)),
                pl.BlockSpec((HALO, D_MODEL), lambda t: (0, 0)))
    n_lat = N_LAT // tm
    return (pl.BlockSpec((tm, D_MODEL), lambda t: (jnp.minimum(t, n_lat - 1), 0)),
            pl.BlockSpec((tm, D_MODEL), lambda t: (jnp.maximum(t - n_lat, 0), 0)))


def _stream_tile(xl_ref, xc_ref, tm, split):
    if not split:
        return xl_ref[...]
    return jnp.where(pl.program_id(0) >= N_LAT // tm, xc_ref[...], xl_ref[...])


def _rms(x, g):
    return x * lax.rsqrt(jnp.mean(x * x, axis=-1, keepdims=True) + EPS) * g


def _adaln_norm(x, g, scale, shift):
    return x * lax.rsqrt(jnp.mean(x * x, axis=-1, keepdims=True) + EPS) * (g * (1.0 + scale)) + shift


def _adaln_kernel(cond_ref, w_ref, b_ref, wqkv_ref, winab_ref, o_ref, wqkv_cast_ref, winab_cast_ref):
    _cast_chunks((wqkv_ref, winab_ref), (wqkv_cast_ref, winab_cast_ref))
    c = cond_ref[...]
    s = c * jax.nn.sigmoid(c)
    w = w_ref[0]
    s_hi = s.astype(BF16)
    s_lo = (s - s_hi.astype(F32)).astype(BF16)
    w_hi = w.astype(BF16)
    w_lo = (w - w_hi.astype(F32)).astype(BF16)
    s_parts = jnp.concatenate([s_hi, s_lo], axis=0)
    by_hi = jnp.dot(s_parts, w_hi, preferred_element_type=F32)
    by_lo = jnp.dot(s_parts, w_lo, preferred_element_type=F32)
    o_ref[0] = (by_hi[:MOD_ROWS] + by_hi[MOD_ROWS:]) + (by_lo[:MOD_ROWS] + by_lo[MOD_ROWS:]) + b_ref[0]


def _adaln(cond, w_mod, b_mod, w_qkv, w_in_ab):
    n_col = N_MOD * D_MODEL // MOD_TN
    n_steps = DEPTH * n_col
    w_qkv2 = w_qkv.reshape(-1, w_qkv.shape[-1])
    w_in_ab2 = w_in_ab.reshape(-1, w_in_ab.shape[-1])

    def chunk_spec(w2):
        assert w2.shape[0] % n_steps == 0
        return pl.BlockSpec((w2.shape[0] // n_steps, w2.shape[1]), lambda l, n: (l * n_col + n, 0))

    mod, w_qkv_b, w_in_ab_b = pl.pallas_call(
        _adaln_kernel,
        grid=(DEPTH, n_col),
        in_specs=[
            pl.BlockSpec((MOD_ROWS, D_MODEL), lambda l, n: (0, 0)),
            pl.BlockSpec((1, D_MODEL, MOD_TN), lambda l, n: (l, 0, n)),
            pl.BlockSpec((1, 1, MOD_TN), lambda l, n: (l, 0, n)),
            chunk_spec(w_qkv2), chunk_spec(w_in_ab2),
        ],
        out_specs=[pl.BlockSpec((1, MOD_ROWS, MOD_TN), lambda l, n: (l, 0, n)),
                   chunk_spec(w_qkv2), chunk_spec(w_in_ab2)],
        out_shape=[jax.ShapeDtypeStruct((DEPTH, MOD_ROWS, N_MOD * D_MODEL), F32),
                   jax.ShapeDtypeStruct(w_qkv2.shape, BF16),
                   jax.ShapeDtypeStruct(w_in_ab2.shape, BF16)],
        compiler_params=pltpu.CompilerParams(
            dimension_semantics=("arbitrary", "arbitrary"),
            vmem_limit_bytes=VMEM_LIMIT_MIB["adaln"] << 20),
        name="adaln",
    )(cond, w_mod, b_mod.reshape(DEPTH, 1, N_MOD * D_MODEL), w_qkv2, w_in_ab2)
    return mod, w_qkv_b.reshape(w_qkv.shape), w_in_ab_b.reshape(w_in_ab.shape)


N_CAST = 3


def _ffn_cast_io(w_o, mixer_layer, w_ffn_in, w_ffn_out, layer):
    jobs = ((w_o, mixer_layer, D_MODEL, D_MODEL), (w_ffn_in, layer, D_MODEL, 2 * D_FF),
            (w_ffn_out, layer, D_FF, D_MODEL))
    in_specs, out_specs, out_shape = [], [], []
    for _, index, rows, cols in jobs:
        assert rows % CAST_STEPS == 0
        chunk = (None, rows // CAST_STEPS, cols)
        in_specs.append(pl.BlockSpec(
            chunk, lambda t, index=index: (index, jnp.minimum(t, CAST_STEPS - 1), 0)))
        out_specs.append(pl.BlockSpec(chunk, lambda t: (0, jnp.minimum(t, CAST_STEPS - 1), 0)))
        out_shape.append(jax.ShapeDtypeStruct((1, rows, cols), BF16))
    return in_specs, [job[0] for job in jobs], out_specs, out_shape


def _cast_chunks(src_refs, dst_refs):
    for src_ref, dst_ref in zip(src_refs, dst_refs):
        dst_ref[...] = src_ref[...].astype(BF16)


def _ffn_kernel(xl_ref, xc_ref, y_ref, g1_ref, sh_ref, sc_ref, g2_ref, nrm_ref, wo_ref, wi_ref,
                wout_ref, nf_ref, o_ref, hmid_ref, *, split, final):
    x1 = _stream_tile(xl_ref, xc_ref, TM_FFN, split) + g1_ref[0] * jnp.dot(
        y_ref[...], wo_ref[...], preferred_element_type=F32)
    h = _adaln_norm(x1, nrm_ref[...], sc_ref[0], sh_ref[0])
    hb = h.astype(BF16)
    for c in range(D_FF // FF_CHUNK):
        lo = c * FF_CHUNK
        a = jnp.dot(hb, wi_ref[:, lo:lo + FF_CHUNK], preferred_element_type=F32)
        g = jnp.dot(hb, wi_ref[:, D_FF + lo:D_FF + lo + FF_CHUNK], preferred_element_type=F32)
        hmid_ref[:, lo:lo + FF_CHUNK] = (a * jax.nn.sigmoid(a) * g).astype(BF16)
    out = x1 + g2_ref[0] * jnp.dot(hmid_ref[...], wout_ref[...], preferred_element_type=F32)
    if final:
        out = _rms(out, nf_ref[...])
    o_ref[...] = out


def _ffn(x_lat, x_ctx, y, mod3, layer, norm_ffn, w_o, w_in, w_out, norm_final, *, n_rows, final):
    tm = TM_FFN
    row_spec = pl.BlockSpec((tm, D_MODEL), lambda t: (t, 0))
    split = x_ctx is not x_lat
    lat_spec, ctx_spec = _stream_specs(tm, split)
    return pl.pallas_call(
        functools.partial(_ffn_kernel, split=split, final=final),
        grid=(n_rows // tm,),
        in_specs=[
            lat_spec, ctx_spec, row_spec,
            _mod_spec(layer, 2, tm), _mod_spec(layer, 3, tm), _mod_spec(layer, 4, tm),
            _mod_spec(layer, 5, tm),
            _const_spec((1, D_MODEL)),
            _layer_spec((D_MODEL, D_MODEL), 0),
            _layer_spec((D_MODEL, 2 * D_FF), 0),
            _layer_spec((D_FF, D_MODEL), 0),
            _const_spec((1, D_MODEL)),
        ],
        out_specs=row_spec,
        out_shape=jax.ShapeDtypeStruct((n_rows, D_MODEL), F32),
        scratch_shapes=[pltpu.VMEM((tm, D_FF), BF16)],
        compiler_params=pltpu.CompilerParams(
            dimension_semantics=("arbitrary",), vmem_limit_bytes=VMEM_LIMIT_MIB["ffn"] << 20),
        name="outproj_ffn",
    )(x_lat, x_ctx, y, mod3, mod3, mod3, mod3, norm_ffn, w_o, w_in, w_out, norm_final)


def _gelu_tanh(x):
    k1 = -2.0 * math.sqrt(2.0 / math.pi) * LOG2E
    return x / (1.0 + jnp.exp2(x * (k1 + (k1 * 0.044715) * (x * x))))


def _ab_kernel(xl_ref, xc_ref, xp_ref, xn_ref, sh_ref, sc_ref, nrm_ref, win_ref, lnv_ref, ws_ref,
               bs_ref, wp_ref, ps_ref, *rest, split):
    y_ref, (p_ref, a2_ref, a4_ref) = rest[N_CAST], rest[2 * N_CAST + 1:]
    _cast_chunks(rest[:N_CAST], rest[N_CAST + 1:2 * N_CAST + 1])
    x_tile = _stream_tile(xl_ref, xc_ref, TM_AB, split)
    n_sub = TM_AB // AB_SUB
    for s in range(n_sub):
        rows = slice(s * AB_SUB, (s + 1) * AB_SUB)
        prev = xp_ref[...] if s == 0 else x_tile[s * AB_SUB - HALO:s * AB_SUB]
        nxt = xn_ref[...] if s == n_sub - 1 else x_tile[(s + 1) * AB_SUB:(s + 1) * AB_SUB + HALO]
        _ab_subtile(jnp.concatenate([prev, x_tile[rows], nxt], axis=0), pl.program_id(0) * n_sub + s,
                    sh_ref, sc_ref, nrm_ref, win_ref, lnv_ref, ws_ref, bs_ref, wp_ref, ps_ref,
                    y_ref.at[rows], p_ref.at[s], a2_ref.at[s], a4_ref.at[s])


def _ab_subtile(x_all, sub_index, sh_ref, sc_ref, nrm_ref, win_ref, lnv_ref, ws_ref, bs_ref, wp_ref,
                ps_ref, y_ref, p_ref, a2_ref, a4_ref):
    tm = AB_SUB
    is_ctx = sub_index >= N_LAT // tm
    seq_len = jnp.where(is_ctx, CTX_LEN, SEQ)
    pos0 = jnp.where(is_ctx, 0, (sub_index % (SEQ // tm)) * tm)

    h = _adaln_norm(x_all, nrm_ref[...], sc_ref[0], sh_ref[0])
    z = jnp.dot(h.astype(BF16), win_ref[...], preferred_element_type=F32)

    n_p = tm + 2 * HALO
    pos_all = pos0 - HALO + lax.broadcasted_iota(jnp.int32, (n_p, D_B), 0)
    p_ref[:n_p, :] = jnp.where((pos_all >= 0) & (pos_all < seq_len), z[:, 2 * D_A:], 0.0)
    p_ref[n_p:, :] = jnp.zeros((2 * HALO, D_B), F32)

    za = _gelu_tanh(z[HALO:HALO + tm, :2 * D_A])
    u = za[:, :D_A]
    vv = za[:, D_A:]
    vc = vv - jnp.mean(vv, axis=-1, keepdims=True)
    v = (vc * lax.rsqrt(jnp.mean(vc * vc, axis=-1, keepdims=True) + EPS) * lnv_ref[...]).astype(BF16)

    n_blocks = tm // CHUNK
    for g in range(A_GROUPS):
        cs = slice(g * GROUP_DIM, (g + 1) * GROUP_DIM)
        v_cat = jnp.concatenate([v[n * CHUNK:(n + 1) * CHUNK, cs] for n in range(n_blocks)], axis=1)
        mixed = jnp.dot(ws_ref[g], v_cat, preferred_element_type=F32)
        for n in range(n_blocks):
            rs = slice(n * CHUNK, (n + 1) * CHUNK)
            y_ref[rs, cs] = (u[rs, cs] * (mixed[:, n * CHUNK:(n + 1) * CHUNK] + bs_ref[g])).astype(BF16)

    a2_ref[...] = p_ref[0:n_p + HALO, GROUP_DIM:] + p_ref[1:n_p + HALO + 1, GROUP_DIM:]
    a4_ref[...] = a2_ref[0:n_p, :] + a2_ref[2:n_p + 2, :]
    a8 = a4_ref[0:tm + HALO, 2 * GROUP_DIM:] + a4_ref[4:tm + HALO + 4, 2 * GROUP_DIM:]
    segs = (
        p_ref[HALO - 1:HALO - 1 + tm, :GROUP_DIM] + p_ref[HALO:HALO + tm, :GROUP_DIM],
        a4_ref[HALO - 2:HALO - 2 + tm, :GROUP_DIM],
        a4_ref[HALO - 4:HALO - 4 + tm, GROUP_DIM:2 * GROUP_DIM] + a4_ref[HALO:HALO + tm, GROUP_DIM:2 * GROUP_DIM],
        a8[0:tm] + a8[HALO:HALO + tm],
    )

    def window_count(first_row, half):
        pos = pos0 + first_row + lax.broadcasted_iota(jnp.int32, (HALO, GROUP_DIM), 0)
        return (jnp.minimum(pos + half, seq_len) - jnp.maximum(pos - half, 0)).astype(F32)

    for g, w in enumerate(POOL_WINDOWS):
        half = w // 2
        cs = slice(g * GROUP_DIM, (g + 1) * GROUP_DIM)
        seg = segs[g]
        pooled = jnp.concatenate([seg[:HALO] / window_count(0, half),
                                  seg[HALO:tm - HALO] * (1.0 / w),
                                  seg[tm - HALO:] / window_count(tm - HALO, half)], axis=0)
        diff = pooled - p_ref[HALO:HALO + tm, cs]
        yb = jnp.dot(diff.astype(BF16), wp_ref[g], preferred_element_type=F32)
        y_ref[:, D_A + g * GROUP_DIM:D_A + (g + 1) * GROUP_DIM] = (yb * ps_ref[:, cs]).astype(BF16)


def _ab_mixer(x_lat, x_ctx, mod3, layer, norm_mix, w_in, mixer_layer, ln_v, w_s, b_s, w_pool,
              pool_scale, ffn_cast):
    cast_in_specs, cast_inputs, cast_out_specs, cast_out_shape = ffn_cast
    tm = TM_AB
    assert N_TOT // tm >= CAST_STEPS and AB_SUB == CTX_LEN and SEQ % tm == 0
    sub = tm // HALO
    last_halo_block = x_lat.shape[0] // HALO - 1
    split = x_ctx is not x_lat
    lat_spec, ctx_spec = _stream_specs(tm, split)
    return pl.pallas_call(
        functools.partial(_ab_kernel, split=split),
        grid=(N_TOT // tm,),
        in_specs=[
            lat_spec, ctx_spec,
            pl.BlockSpec((HALO, D_MODEL), lambda t: (jnp.clip(t * sub - 1, 0, last_halo_block), 0)),
            pl.BlockSpec((HALO, D_MODEL), lambda t: (jnp.minimum((t + 1) * sub, last_halo_block), 0)),
            _mod_spec(layer, 0, tm), _mod_spec(layer, 1, tm),
            _const_spec((1, D_MODEL)),
            _layer_spec((D_MODEL, 2 * D_A + D_B), mixer_layer),
            _const_spec((1, D_A)),
            _const_spec((A_GROUPS, CHUNK, CHUNK)),
            _const_spec((A_GROUPS, CHUNK, GROUP_DIM)),
            _const_spec((A_GROUPS, GROUP_DIM, GROUP_DIM)),
            _const_spec((1, D_B)),
            *cast_in_specs,
        ],
        out_specs=[pl.BlockSpec((tm, D_MODEL), lambda t: (t, 0)), *cast_out_specs],
        out_shape=[jax.ShapeDtypeStruct((N_TOT, D_MODEL), BF16), *cast_out_shape],
        scratch_shapes=[pltpu.VMEM((tm // AB_SUB, AB_SUB + 4 * HALO, D_B), F32),
                        pltpu.VMEM((tm // AB_SUB, AB_SUB + 3 * HALO, D_B - GROUP_DIM), F32),
                        pltpu.VMEM((tm // AB_SUB, AB_SUB + 2 * HALO, D_B - GROUP_DIM), F32)],
        compiler_params=pltpu.CompilerParams(
            dimension_semantics=("arbitrary",), vmem_limit_bytes=VMEM_LIMIT_MIB["mixer"] << 20),
        name="ab_mixer",
    )(x_lat, x_ctx, x_lat, x_lat, mod3, mod3, norm_mix, w_in, ln_v, w_s, b_s, w_pool, pool_scale,
      *cast_inputs)


def _qkv_kernel(x_ref, sh_ref, sc_ref, nrm_ref, w_ref, *rest):
    q_ref, k_ref, v_ref = rest[N_CAST:N_CAST + 3]
    _cast_chunks(rest[:N_CAST], rest[N_CAST + 3:])
    h = _adaln_norm(x_ref[...], nrm_ref[...], sc_ref[0], sh_ref[0])
    hb = h.astype(BF16)
    scale = HEAD_DIM ** -0.5 * LOG2E
    q_ref[...] = (jnp.dot(hb, w_ref[:, :D_MODEL], preferred_element_type=F32) * scale).astype(BF16)
    k_ref[...] = jnp.dot(hb, w_ref[:, D_MODEL:2 * D_MODEL], preferred_element_type=F32).astype(BF16)
    v = jnp.dot(hb, w_ref[:, 2 * D_MODEL:], preferred_element_type=F32).astype(BF16)
    ones = jnp.ones((v.shape[0], HEAD_PAIR), BF16)
    for hp in range(N_HEAD_PAIRS):
        v_ref[:, 2 * hp * HEAD_PAIR:(2 * hp + 1) * HEAD_PAIR] = v[:, hp * HEAD_PAIR:(hp + 1) * HEAD_PAIR]
        v_ref[:, (2 * hp + 1) * HEAD_PAIR:(2 * hp + 2) * HEAD_PAIR] = ones


def _qkv(xs, mod3, layer, norm_mix, w_qkv, mixer_layer, ffn_cast):
    cast_in_specs, cast_inputs, cast_out_specs, cast_out_shape = ffn_cast
    tm = TM_QKV
    assert N_TOT // tm >= CAST_STEPS
    row_spec = pl.BlockSpec((tm, D_MODEL), lambda t: (t, 0))
    out = jax.ShapeDtypeStruct((N_TOT, D_MODEL), BF16)
    return pl.pallas_call(
        _qkv_kernel,
        grid=(N_TOT // tm,),
        in_specs=[row_spec, _mod_spec(layer, 0, tm), _mod_spec(layer, 1, tm),
                  _const_spec((1, D_MODEL)), _layer_spec((D_MODEL, 3 * D_MODEL), mixer_layer),
                  *cast_in_specs],
        out_specs=[row_spec, row_spec, pl.BlockSpec((tm, 2 * D_MODEL), lambda t: (t, 0)),
                   *cast_out_specs],
        out_shape=[out, out, jax.ShapeDtypeStruct((N_TOT, 2 * D_MODEL), BF16), *cast_out_shape],
        compiler_params=pltpu.CompilerParams(
            dimension_semantics=("arbitrary",), vmem_limit_bytes=VMEM_LIMIT_MIB["mixer"] << 20),
        name="qkv_proj",
    )(xs, mod3, mod3, norm_mix, w_qkv, *cast_inputs)


_NT_DIMS = (((1,), (1,)), ((), ()))


def _head_masks(rows):
    lane = lax.broadcasted_iota(jnp.int32, (rows, HEAD_PAIR), 1)
    return lane < HEAD_DIM


N_DR_PAIRS = 2 * NA_ROWS - 2


def _bias_rows(rpb):
    lo, hi = rpb[:, :-1], rpb[:, 1:]
    gap = jnp.zeros((N_HEADS, N_DR_PAIRS, GRID_W - 2 * NA_COLS + 1), F32)
    rows = jnp.concatenate([lo[..., NA_COLS - 1:], gap, hi, gap, lo[..., :NA_COLS - 1]], axis=-1)
    return rows[:, :, None, :]


def _build_bias_pairs(rows_ref, bias_ref):
    shape = (GRID_W, HEAD_PAIR)
    c = lax.broadcasted_iota(jnp.int32, shape, 0)
    kc = lax.broadcasted_iota(jnp.int32, shape, 1) % GRID_W
    col_start = jnp.clip(c - NA_COLS // 2, 0, GRID_W - NA_COLS)
    live = (kc >= col_start) & (kc < col_start + NA_COLS)
    for d in range(N_DR_PAIRS):
        for hh in range(2):
            t = pltpu.roll(jnp.broadcast_to(rows_ref[hh, d], shape), 0, 1, stride=1, stride_axis=0)
            bias_ref[d, hh * GRID_W:(hh + 1) * GRID_W, :] = jnp.where(live, t, NEG_INF) * LOG2E


N_KEYS = NB_KEYS + CTX_LEN
ROWS_PER_STEP = 32


IMAGES_PER_STEP = 2


def _na_kernel(rows_ref, q_ref, k_ref, v_ref, kc_ref, vc_ref, o_ref,
               bias_ref, s_scr, m_scr):
    @pl.when(pl.program_id(1) == 0)
    def _():
        _build_bias_pairs(rows_ref, bias_ref)

    first = _head_masks(GRID_W)
    steps_per_image = GRID_H // ROWS_PER_STEP

    def locate(i, j):
        img = i // steps_per_image
        r = (i % steps_per_image) * ROWS_PER_STEP + j
        return img, r, jnp.clip(r - NA_ROWS // 2, 0, GRID_H - NA_ROWS)

    def rows(row, n):
        return pl.ds(pl.multiple_of(row * GRID_W, GRID_W), n)

    def qk_stage(i):
        for j in range(ROWS_PER_STEP):
            img, r, start = locate(i, j)
            dr0 = start - r + (NA_ROWS - 1)
            q_r = q_ref[rows(img * GRID_H + r, GRID_W), :]
            zero = jnp.zeros_like(q_r)
            qs = jnp.concatenate([jnp.where(first, q_r, zero), jnp.where(first, zero, q_r)], axis=0)
            bias = jnp.concatenate([bias_ref[dr0 + 2 * t] for t in range(NA_ROWS // 2)], axis=1)
            k_w = k_ref[rows(img * GRID_H + start, NB_KEYS), :]
            kc = kc_ref[pl.ds(pl.multiple_of(img * CTX_LEN, CTX_LEN), CTX_LEN), :]
            s_nb = lax.dot_general(qs, k_w, _NT_DIMS, preferred_element_type=F32) + bias
            s_cx = lax.dot_general(qs, kc, _NT_DIMS, preferred_element_type=F32)
            s_scr[i % 2, j, :, :NB_KEYS] = s_nb
            s_scr[i % 2, j, :, NB_KEYS:] = s_cx
            groups = [s_nb[:, t * HEAD_PAIR:(t + 1) * HEAD_PAIR] for t in range(NB_KEYS // HEAD_PAIR)]
            groups += [s_cx[:, t * HEAD_PAIR:(t + 1) * HEAD_PAIR] for t in range(CTX_LEN // HEAD_PAIR)]
            m_scr[i % 2, j] = functools.reduce(jnp.maximum, groups)

    def softmax_pv_stage(i):
        for j in range(ROWS_PER_STEP):
            img, r, start = locate(i, j)
            m = jnp.max(m_scr[i % 2, j], axis=-1, keepdims=True)
            p = jnp.exp2((s_scr[i % 2, j] - m).astype(BF16))
            v_w = v_ref[rows(img * GRID_H + start, NB_KEYS), :]
            vc = vc_ref[pl.ds(pl.multiple_of(img * CTX_LEN, CTX_LEN), CTX_LEN), :]
            oa = (jnp.dot(p[:, :NB_KEYS], v_w, preferred_element_type=F32)
                  + jnp.dot(p[:, NB_KEYS:], vc, preferred_element_type=F32))
            o2 = oa[:, :HEAD_PAIR] / oa[:, HEAD_PAIR:]
            o_ref[rows(img * GRID_H + r, GRID_W), :] = (
                jnp.where(first, o2[:GRID_W], o2[GRID_W:]).astype(BF16))

    n_steps = IMAGES_PER_STEP * steps_per_image
    qk_stage(0)

    def step(i, carry):
        softmax_pv_stage(i - 1)
        qk_stage(i)
        return carry

    lax.fori_loop(1, n_steps, step, 0)
    softmax_pv_stage(n_steps - 1)


def _na_attention(q, k, v, rpb):
    lat_rows = IMAGES_PER_STEP * SEQ
    ctx_rows = IMAGES_PER_STEP * CTX_LEN
    lat_spec = pl.BlockSpec((lat_rows, HEAD_PAIR), lambda hp, g: (g, hp))
    ctx_spec = pl.BlockSpec((ctx_rows, HEAD_PAIR), lambda hp, g: (N_LAT // ctx_rows + g, hp))
    return pl.pallas_call(
        _na_kernel,
        grid=(N_HEAD_PAIRS, BATCH // IMAGES_PER_STEP),
        in_specs=[pl.BlockSpec((2, N_DR_PAIRS, 1, HEAD_PAIR), lambda hp, g: (hp, 0, 0, 0)),
                  lat_spec, lat_spec,
                  pl.BlockSpec((lat_rows, 2 * HEAD_PAIR), lambda hp, g: (g, hp)),
                  ctx_spec,
                  pl.BlockSpec((ctx_rows, 2 * HEAD_PAIR), lambda hp, g: (N_LAT // ctx_rows + g, hp))],
        out_specs=lat_spec,
        out_shape=jax.ShapeDtypeStruct((N_TOT, D_MODEL), BF16),
        scratch_shapes=[
            pltpu.VMEM((N_DR_PAIRS, HEAD_PAIR, HEAD_PAIR), F32),
            pltpu.VMEM((2, ROWS_PER_STEP, HEAD_PAIR, N_KEYS), F32),
            pltpu.VMEM((2, ROWS_PER_STEP, HEAD_PAIR, HEAD_PAIR), F32),
        ],
        compiler_params=pltpu.CompilerParams(
            dimension_semantics=("arbitrary", "arbitrary"),
            vmem_limit_bytes=VMEM_LIMIT_MIB["attention"] << 20),
        name="na_attention",
    )(_bias_rows(rpb), q, k, v, k, v)


def _ctx_attn_kernel(q_ref, k_ref, v_ref, o_in_ref, o_ref):
    del o_in_ref
    first = _head_masks(CTX_LEN)
    for hp in range(N_HEAD_PAIRS):
        cols = slice(hp * HEAD_PAIR, (hp + 1) * HEAD_PAIR)
        q = q_ref[:, cols]
        k = k_ref[:, cols]
        v = v_ref[:, 2 * hp * HEAD_PAIR:(2 * hp + 1) * HEAD_PAIR]
        outs = []
        for hh in range(2):
            sel = first if hh == 0 else jnp.logical_not(first)
            qm = jnp.where(sel, q, jnp.zeros_like(q))
            s = lax.dot_general(qm, k, _NT_DIMS, preferred_element_type=F32)
            p = jnp.exp2(s - jnp.max(s, axis=-1, keepdims=True))
            denom = jnp.sum(p, axis=-1, keepdims=True)
            outs.append(jnp.dot(p.astype(BF16), v, preferred_element_type=F32) / denom)
        o_ref[:, cols] = jnp.where(first, outs[0], outs[1]).astype(BF16)


def _ctx_attention(q, k, v, o):
    ctx_spec = pl.BlockSpec((CTX_LEN, D_MODEL), lambda b: (N_LAT // CTX_LEN + b, 0))
    return pl.pallas_call(
        _ctx_attn_kernel,
        grid=(BATCH,),
        in_specs=[ctx_spec, ctx_spec,
                  pl.BlockSpec((CTX_LEN, 2 * D_MODEL), lambda b: (N_LAT // CTX_LEN + b, 0)),
                  pl.BlockSpec(memory_space=pl.ANY)],
        out_specs=ctx_spec,
        out_shape=jax.ShapeDtypeStruct((N_TOT, D_MODEL), BF16),
        input_output_aliases={3: 0},
        compiler_params=pltpu.CompilerParams(dimension_semantics=("arbitrary",)),
        name="ctx_attention",
    )(q, k, v, o)


def kernel(x, c, ctx, c_ctx, w_mod, b_mod, norm_mix, norm_ffn, w_in_ab, ln_v, w_spatial,
           b_spatial, w_pool, pool_scale, w_out_ab, w_qkv, rpb, w_out_na, w_ffn_in,
           w_ffn_out, norm_final):
    cond = jnp.zeros((MOD_ROWS, D_MODEL), F32).at[:BATCH].set(c).at[CTX_MOD_ROW].set(c_ctx)
    mod, w_qkv, w_in_ab = _adaln(cond, w_mod, b_mod, w_qkv, w_in_ab)
    mod3 = mod.reshape(DEPTH * MOD_ROWS * N_MOD, 1, D_MODEL)

    x_lat, x_ctx = x.reshape(N_LAT, D_MODEL), ctx.reshape(N_CTX, D_MODEL)
    nf = norm_final.reshape(1, D_MODEL)
    for i in range(DEPTH):
        last = i == DEPTH - 1
        j = i // 2
        nm = norm_mix[i].reshape(1, D_MODEL)
        ffn_cast = _ffn_cast_io(w_out_na if i % 2 == 1 else w_out_ab, j, w_ffn_in, w_ffn_out, i)
        if i % 2 == 1:
            q, k, v, w_o, w_in, w_out = _qkv(x_lat, mod3, i, nm, w_qkv, j, ffn_cast)
            y = _na_attention(q, k, v, rpb[j])
            if not last:
                y = _ctx_attention(q, k, v, y)
        else:
            b_s = jnp.broadcast_to(b_spatial[j][:, :, None], (A_GROUPS, CHUNK, GROUP_DIM))
            y, w_o, w_in, w_out = _ab_mixer(
                x_lat, x_ctx, mod3, i, nm, w_in_ab, j, ln_v[j].reshape(1, D_A),
                w_spatial[j].astype(BF16), b_s, w_pool[j].astype(BF16),
                pool_scale[j].reshape(1, D_B), ffn_cast)
        x_lat = x_ctx = _ffn(x_lat, x_ctx, y, mod3, i, norm_ffn[i].reshape(1, D_MODEL), w_o,
                             w_in, w_out, nf, n_rows=N_LAT if last else N_TOT, final=last)
    return x_lat.reshape(BATCH, SEQ, D_MODEL)
```

```python
import functools
import math

import jax
import jax.numpy as jnp
from jax import lax
from jax.experimental import pallas as pl
from jax.experimental.pallas import tpu as pltpu

D_MODEL = 1024
BATCH = 4
SEQ = 4096
DEPTH = 4
GRID_W = 64
GRID_H = SEQ // GRID_W
CTX_LEN = 256
D_A = D_MODEL // 2
A_GROUPS = 4
CHUNK = 128
D_B = D_MODEL - D_A
POOL_WINDOWS = (2, 4, 8, 16)
GROUP_DIM = 128
N_HEADS = 16
HEAD_DIM = 64
NA_ROWS = 8
NA_COLS = 16
D_FF = 2816
EPS = 1e-6
LOG2E = 1.4426950408889634
NEG_INF = -1e30

N_LAT = BATCH * SEQ
N_CTX = BATCH * CTX_LEN
N_TOT = N_LAT + N_CTX
MOD_ROWS = 8
CTX_MOD_ROW = BATCH
N_MOD = 6

TM_FFN = 1024
TM_QKV = 1024
TM_AB = 1024
AB_SUB = 256
HALO = 8
FF_CHUNK = 256
CAST_STEPS = 16
MOD_TN = 1536
HEAD_PAIR = 2 * HEAD_DIM
N_HEAD_PAIRS = N_HEADS // 2
NB_KEYS = NA_ROWS * GRID_W

VMEM_LIMIT_MIB = {"adaln": 32, "ffn": 60, "mixer": 48, "attention": 56}

F32 = jnp.float32
BF16 = jnp.bfloat16


def _const_spec(shape):
    nd = len(shape)
    return pl.BlockSpec(shape, lambda *_: (0,) * nd, pipeline_mode=pl.Buffered(1))


def _layer_spec(shape, layer):
    nd = len(shape)
    return pl.BlockSpec((None,) + tuple(shape), lambda *_: (layer,) + (0,) * nd,
                        pipeline_mode=pl.Buffered(1))


def _mod_spec(layer, k, tm):
    n_lat_tiles = N_LAT // tm
    tiles_per_batch = SEQ // tm

    def index_map(t, *_):
        row = jnp.where(t < n_lat_tiles, t // tiles_per_batch, CTX_MOD_ROW)
        return ((layer * MOD_ROWS + row) * N_MOD + k, 0, 0)

    return pl.BlockSpec((1, 1, D_MODEL), index_map)


def _stream_specs(tm, split):
    if not split:
        return (pl.BlockSpec((tm, D_MODEL), lambda t: (t, 0)),
                pl.BlockSpec((HALO, D_MODEL), lambda t: (0, 0)))
    n_lat = N_LAT // tm
    return (pl.BlockSpec((tm, D_MODEL), lambda t: (jnp.minimum(t, n_lat - 1), 0)),
            pl.BlockSpec((tm, D_MODEL), lambda t: (jnp.maximum(t - n_lat, 0), 0)))


def _stream_tile(xl_ref, xc_ref, tm, split):
    if not split:
        return xl_ref[...]
    return jnp.where(pl.program_id(0) >= N_LAT // tm, xc_ref[...], xl_ref[...])


def _rms(x, g):
    return x * lax.rsqrt(jnp.mean(x * x, axis=-1, keepdims=True) + EPS) * g


def _adaln_norm(x, g, scale, shift):
    return x * lax.rsqrt(jnp.mean(x * x, axis=-1, keepdims=True) + EPS) * (g * (1.0 + scale)) + shift


def _adaln_kernel(cond_ref, w_ref, b_ref, wqkv_ref, winab_ref, o_ref, wqkv_cast_ref, winab_cast_ref):
    _cast_chunks((wqkv_ref, winab_ref), (wqkv_cast_ref, winab_cast_ref))
    c = cond_ref[...]
    s = c * jax.nn.sigmoid(c)
    w = w_ref[0]
    s_hi = s.astype(BF16)
    s_lo = (s - s_hi.astype(F32)).astype(BF16)
    w_hi = w.astype(BF16)
    w_lo = (w - w_hi.astype(F32)).astype(BF16)
    s_parts = jnp.concatenate([s_hi, s_lo], axis=0)
    by_hi = jnp.dot(s_parts, w_hi, preferred_element_type=F32)
    by_lo = jnp.dot(s_parts, w_lo, preferred_element_type=F32)
    o_ref[0] = (by_hi[:MOD_ROWS] + by_hi[MOD_ROWS:]) + (by_lo[:MOD_ROWS] + by_lo[MOD_ROWS:]) + b_ref[0]


def _adaln(cond, w_mod, b_mod, w_qkv, w_in_ab):
    n_col = N_MOD * D_MODEL // MOD_TN
    n_steps = DEPTH * n_col
    w_qkv2 = w_qkv.reshape(-1, w_qkv.shape[-1])
    w_in_ab2 = w_in_ab.reshape(-1, w_in_ab.shape[-1])

    def chunk_spec(w2):
        assert w2.shape[0] % n_steps == 0
        return pl.BlockSpec((w2.shape[0] // n_steps, w2.shape[1]), lambda l, n: (l * n_col + n, 0))

    mod, w_qkv_b, w_in_ab_b = pl.pallas_call(
        _adaln_kernel,
        grid=(DEPTH, n_col),
        in_specs=[
            pl.BlockSpec((MOD_ROWS, D_MODEL), lambda l, n: (0, 0)),
            pl.BlockSpec((1, D_MODEL, MOD_TN), lambda l, n: (l, 0, n)),
            pl.BlockSpec((1, 1, MOD_TN), lambda l, n: (l, 0, n)),
            chunk_spec(w_qkv2), chunk_spec(w_in_ab2),
        ],
        out_specs=[pl.BlockSpec((1, MOD_ROWS, MOD_TN), lambda l, n: (l, 0, n)),
                   chunk_spec(w_qkv2), chunk_spec(w_in_ab2)],
        out_shape=[jax.ShapeDtypeStruct((DEPTH, MOD_ROWS, N_MOD * D_MODEL), F32),
                   jax.ShapeDtypeStruct(w_qkv2.shape, BF16),
                   jax.ShapeDtypeStruct(w_in_ab2.shape, BF16)],
        compiler_params=pltpu.CompilerParams(
            dimension_semantics=("parallel", "parallel"),
            vmem_limit_bytes=VMEM_LIMIT_MIB["adaln"] << 20),
        name="adaln",
    )(cond, w_mod, b_mod.reshape(DEPTH, 1, N_MOD * D_MODEL), w_qkv2, w_in_ab2)
    return mod, w_qkv_b.reshape(w_qkv.shape), w_in_ab_b.reshape(w_in_ab.shape)


N_CAST = 3


def _ffn_cast_io(w_o, mixer_layer, w_ffn_in, w_ffn_out, layer):
    jobs = ((w_o, mixer_layer, D_MODEL, D_MODEL), (w_ffn_in, layer, D_MODEL, 2 * D_FF),
            (w_ffn_out, layer, D_FF, D_MODEL))
    in_specs, out_specs, out_shape = [], [], []
    for _, index, rows, cols in jobs:
        assert rows % CAST_STEPS == 0
        chunk = (None, rows // CAST_STEPS, cols)
        in_specs.append(pl.BlockSpec(
            chunk, lambda t, index=index: (index, jnp.minimum(t, CAST_STEPS - 1), 0)))
        out_specs.append(pl.BlockSpec(chunk, lambda t: (0, jnp.minimum(t, CAST_STEPS - 1), 0)))
        out_shape.append(jax.ShapeDtypeStruct((1, rows, cols), BF16))
    return in_specs, [job[0] for job in jobs], out_specs, out_shape


def _cast_chunks(src_refs, dst_refs):
    for src_ref, dst_ref in zip(src_refs, dst_refs):
        dst_ref[...] = src_ref[...].astype(BF16)


def _ffn_kernel(xl_ref, xc_ref, y_ref, g1_ref, sh_ref, sc_ref, g2_ref, nrm_ref, wo_ref, wi_ref,
                wout_ref, nf_ref, o_ref, hmid_ref, *, split, final):
    x1 = _stream_tile(xl_ref, xc_ref, TM_FFN, split) + g1_ref[0] * jnp.dot(
        y_ref[...], wo_ref[...], preferred_element_type=F32)
    h = _adaln_norm(x1, nrm_ref[...], sc_ref[0], sh_ref[0])
    hb = h.astype(BF16)
    for c in range(D_FF // FF_CHUNK):
        lo = c * FF_CHUNK
        a = jnp.dot(hb, wi_ref[:, lo:lo + FF_CHUNK], preferred_element_type=F32)
        g = jnp.dot(hb, wi_ref[:, D_FF + lo:D_FF + lo + FF_CHUNK], preferred_element_type=F32)
        hmid_ref[:, lo:lo + FF_CHUNK] = (a * jax.nn.sigmoid(a) * g).astype(BF16)
    out = x1 + g2_ref[0] * jnp.dot(hmid_ref[...], wout_ref[...], preferred_element_type=F32)
    if final:
        out = _rms(out, nf_ref[...])
    o_ref[...] = out


def _ffn(x_lat, x_ctx, y, mod3, layer, norm_ffn, w_o, w_in, w_out, norm_final, *, n_rows, final):
    tm = TM_FFN
    row_spec = pl.BlockSpec((tm, D_MODEL), lambda t: (t, 0))
    split = x_ctx is not x_lat
    lat_spec, ctx_spec = _stream_specs(tm, split)
    return pl.pallas_call(
        functools.partial(_ffn_kernel, split=split, final=final),
        grid=(n_rows // tm,),
        in_specs=[
            lat_spec, ctx_spec, row_spec,
            _mod_spec(layer, 2, tm), _mod_spec(layer, 3, tm), _mod_spec(layer, 4, tm),
            _mod_spec(layer, 5, tm),
            _const_spec((1, D_MODEL)),
            _layer_spec((D_MODEL, D_MODEL), 0),
            _layer_spec((D_MODEL, 2 * D_FF), 0),
            _layer_spec((D_FF, D_MODEL), 0),
            _const_spec((1, D_MODEL)),
        ],
        out_specs=row_spec,
        out_shape=jax.ShapeDtypeStruct((n_rows, D_MODEL), F32),
        scratch_shapes=[pltpu.VMEM((tm, D_FF), BF16)],
        compiler_params=pltpu.CompilerParams(
            dimension_semantics=("parallel",), vmem_limit_bytes=VMEM_LIMIT_MIB["ffn"] << 20),
        name="outproj_ffn",
    )(x_lat, x_ctx, y, mod3, mod3, mod3, mod3, norm_ffn, w_o, w_in, w_out, norm_final)


def _gelu_tanh(x):
    k1 = -2.0 * math.sqrt(2.0 / math.pi) * LOG2E
    return x / (1.0 + jnp.exp2(x * (k1 + (k1 * 0.044715) * (x * x))))


def _ab_kernel(xl_ref, xc_ref, xp_ref, xn_ref, sh_ref, sc_ref, nrm_ref, win_ref, lnv_ref, ws_ref,
               bs_ref, wp_ref, ps_ref, *rest, split):
    y_ref, (p_ref, a2_ref, a4_ref) = rest[N_CAST], rest[2 * N_CAST + 1:]
    _cast_chunks(rest[:N_CAST], rest[N_CAST + 1:2 * N_CAST + 1])
    x_tile = _stream_tile(xl_ref, xc_ref, TM_AB, split)
    n_sub = TM_AB // AB_SUB
    for s in range(n_sub):
        rows = slice(s * AB_SUB, (s + 1) * AB_SUB)
        prev = xp_ref[...] if s == 0 else x_tile[s * AB_SUB - HALO:s * AB_SUB]
        nxt = xn_ref[...] if s == n_sub - 1 else x_tile[(s + 1) * AB_SUB:(s + 1) * AB_SUB + HALO]
        _ab_subtile(jnp.concatenate([prev, x_tile[rows], nxt], axis=0), pl.program_id(0) * n_sub + s,
                    sh_ref, sc_ref, nrm_ref, win_ref, lnv_ref, ws_ref, bs_ref, wp_ref, ps_ref,
                    y_ref.at[rows], p_ref.at[s], a2_ref.at[s], a4_ref.at[s])


def _ab_subtile(x_all, sub_index, sh_ref, sc_ref, nrm_ref, win_ref, lnv_ref, ws_ref, bs_ref, wp_ref,
                ps_ref, y_ref, p_ref, a2_ref, a4_ref):
    tm = AB_SUB
    is_ctx = sub_index >= N_LAT // tm
    seq_len = jnp.where(is_ctx, CTX_LEN, SEQ)
    pos0 = jnp.where(is_ctx, 0, (sub_index % (SEQ // tm)) * tm)

    h = _adaln_norm(x_all, nrm_ref[...], sc_ref[0], sh_ref[0])
    z = jnp.dot(h.astype(BF16), win_ref[...], preferred_element_type=F32)

    n_p = tm + 2 * HALO
    pos_all = pos0 - HALO + lax.broadcasted_iota(jnp.int32, (n_p, D_B), 0)
    p_ref[:n_p, :] = jnp.where((pos_all >= 0) & (pos_all < seq_len), z[:, 2 * D_A:], 0.0)
    p_ref[n_p:, :] = jnp.zeros((2 * HALO, D_B), F32)

    za = _gelu_tanh(z[HALO:HALO + tm, :2 * D_A])
    u = za[:, :D_A]
    vv = za[:, D_A:]
    vc = vv - jnp.mean(vv, axis=-1, keepdims=True)
    v = (vc * lax.rsqrt(jnp.mean(vc * vc, axis=-1, keepdims=True) + EPS) * lnv_ref[...]).astype(BF16)

    n_blocks = tm // CHUNK
    for g in range(A_GROUPS):
        cs = slice(g * GROUP_DIM, (g + 1) * GROUP_DIM)
        v_cat = jnp.concatenate([v[n * CHUNK:(n + 1) * CHUNK, cs] for n in range(n_blocks)], axis=1)
        mixed = jnp.dot(ws_ref[g], v_cat, preferred_element_type=F32)
        for n in range(n_blocks):
            rs = slice(n * CHUNK, (n + 1) * CHUNK)
            y_ref[rs, cs] = (u[rs, cs] * (mixed[:, n * CHUNK:(n + 1) * CHUNK] + bs_ref[g])).astype(BF16)

    a2_ref[...] = p_ref[0:n_p + HALO, GROUP_DIM:] + p_ref[1:n_p + HALO + 1, GROUP_DIM:]
    a4_ref[...] = a2_ref[0:n_p, :] + a2_ref[2:n_p + 2, :]
    a8 = a4_ref[0:tm + HALO, 2 * GROUP_DIM:] + a4_ref[4:tm + HALO + 4, 2 * GROUP_DIM:]
    segs = (
        p_ref[HALO - 1:HALO - 1 + tm, :GROUP_DIM] + p_ref[HALO:HALO + tm, :GROUP_DIM],
        a4_ref[HALO - 2:HALO - 2 + tm, :GROUP_DIM],
        a4_ref[HALO - 4:HALO - 4 + tm, GROUP_DIM:2 * GROUP_DIM] + a4_ref[HALO:HALO + tm, GROUP_DIM:2 * GROUP_DIM],
        a8[0:tm] + a8[HALO:HALO + tm],
    )

    def window_count(first_row, half):
        pos = pos0 + first_row + lax.broadcasted_iota(jnp.int32, (HALO, GROUP_DIM), 0)
        return (jnp.minimum(pos + half, seq_len) - jnp.maximum(pos - half, 0)).astype(F32)

    for g, w in enumerate(POOL_WINDOWS):
        half = w // 2
        cs = slice(g * GROUP_DIM, (g + 1) * GROUP_DIM)
        seg = segs[g]
        pooled = jnp.concatenate([seg[:HALO] / window_count(0, half),
                                  seg[HALO:tm - HALO] * (1.0 / w),
                                  seg[tm - HALO:] / window_count(tm - HALO, half)], axis=0)
        diff = pooled - p_ref[HALO:HALO + tm, cs]
        yb = jnp.dot(diff.astype(BF16), wp_ref[g], preferred_element_type=F32)
        y_ref[:, D_A + g * GROUP_DIM:D_A + (g + 1) * GROUP_DIM] = (yb * ps_ref[:, cs]).astype(BF16)


def _ab_mixer(x_lat, x_ctx, mod3, layer, norm_mix, w_in, mixer_layer, ln_v, w_s, b_s, w_pool,
              pool_scale, ffn_cast):
    cast_in_specs, cast_inputs, cast_out_specs, cast_out_shape = ffn_cast
    tm = TM_AB
    assert N_TOT // tm >= CAST_STEPS and AB_SUB == CTX_LEN and SEQ % tm == 0
    sub = tm // HALO
    last_halo_block = x_lat.shape[0] // HALO - 1
    split = x_ctx is not x_lat
    lat_spec, ctx_spec = _stream_specs(tm, split)
    return pl.pallas_call(
        functools.partial(_ab_kernel, split=split),
        grid=(N_TOT // tm,),
        in_specs=[
            lat_spec, ctx_spec,
            pl.BlockSpec((HALO, D_MODEL), lambda t: (jnp.clip(t * sub - 1, 0, last_halo_block), 0)),
            pl.BlockSpec((HALO, D_MODEL), lambda t: (jnp.minimum((t + 1) * sub, last_halo_block), 0)),
            _mod_spec(layer, 0, tm), _mod_spec(layer, 1, tm),
            _const_spec((1, D_MODEL)),
            _layer_spec((D_MODEL, 2 * D_A + D_B), mixer_layer),
            _const_spec((1, D_A)),
            _const_spec((A_GROUPS, CHUNK, CHUNK)),
            _const_spec((A_GROUPS, CHUNK, GROUP_DIM)),
            _const_spec((A_GROUPS, GROUP_DIM, GROUP_DIM)),
            _const_spec((1, D_B)),
            *cast_in_specs,
        ],
        out_specs=[pl.BlockSpec((tm, D_MODEL), lambda t: (t, 0)), *cast_out_specs],
        out_shape=[jax.ShapeDtypeStruct((N_TOT, D_MODEL), BF16), *cast_out_shape],
        scratch_shapes=[pltpu.VMEM((tm // AB_SUB, AB_SUB + 4 * HALO, D_B), F32),
                        pltpu.VMEM((tm // AB_SUB, AB_SUB + 3 * HALO, D_B - GROUP_DIM), F32),
                        pltpu.VMEM((tm // AB_SUB, AB_SUB + 2 * HALO, D_B - GROUP_DIM), F32)],
        compiler_params=pltpu.CompilerParams(
            dimension_semantics=("arbitrary",), vmem_limit_bytes=VMEM_LIMIT_MIB["mixer"] << 20),
        name="ab_mixer",
    )(x_lat, x_ctx, x_lat, x_lat, mod3, mod3, norm_mix, w_in, ln_v, w_s, b_s, w_pool, pool_scale,
      *cast_inputs)


def _qkv_kernel(x_ref, sh_ref, sc_ref, nrm_ref, w_ref, *rest):
    q_ref, k_ref, v_ref = rest[N_CAST:N_CAST + 3]
    _cast_chunks(rest[:N_CAST], rest[N_CAST + 3:])
    h = _adaln_norm(x_ref[...], nrm_ref[...], sc_ref[0], sh_ref[0])
    hb = h.astype(BF16)
    scale = HEAD_DIM ** -0.5 * LOG2E
    q_ref[...] = (jnp.dot(hb, w_ref[:, :D_MODEL], preferred_element_type=F32) * scale).astype(BF16)
    k_ref[...] = jnp.dot(hb, w_ref[:, D_MODEL:2 * D_MODEL], preferred_element_type=F32).astype(BF16)
    v_ref[...] = jnp.dot(hb, w_ref[:, 2 * D_MODEL:], preferred_element_type=F32).astype(BF16)


def _qkv(xs, mod3, layer, norm_mix, w_qkv, mixer_layer, ffn_cast):
    cast_in_specs, cast_inputs, cast_out_specs, cast_out_shape = ffn_cast
    tm = TM_QKV
    assert N_TOT // tm >= CAST_STEPS
    row_spec = pl.BlockSpec((tm, D_MODEL), lambda t: (t, 0))
    out = jax.ShapeDtypeStruct((N_TOT, D_MODEL), BF16)
    return pl.pallas_call(
        _qkv_kernel,
        grid=(N_TOT // tm,),
        in_specs=[row_spec, _mod_spec(layer, 0, tm), _mod_spec(layer, 1, tm),
                  _const_spec((1, D_MODEL)), _layer_spec((D_MODEL, 3 * D_MODEL), mixer_layer),
                  *cast_in_specs],
        out_specs=[row_spec, row_spec, row_spec, *cast_out_specs],
        out_shape=[out, out, out, *cast_out_shape],
        compiler_params=pltpu.CompilerParams(
            dimension_semantics=("arbitrary",), vmem_limit_bytes=VMEM_LIMIT_MIB["mixer"] << 20),
        name="qkv_proj",
    )(xs, mod3, mod3, norm_mix, w_qkv, *cast_inputs)


_NT_DIMS = (((1,), (1,)), ((), ()))


def _head_masks(rows):
    lane = lax.broadcasted_iota(jnp.int32, (rows, HEAD_PAIR), 1)
    return lane < HEAD_DIM


N_DR_PAIRS = 2 * NA_ROWS - 2


def _bias_rows(rpb):
    lo, hi = rpb[:, :-1], rpb[:, 1:]
    gap = jnp.zeros((N_HEADS, N_DR_PAIRS, GRID_W - 2 * NA_COLS + 1), F32)
    rows = jnp.concatenate([lo[..., NA_COLS - 1:], gap, hi, gap, lo[..., :NA_COLS - 1]], axis=-1)
    return rows[:, :, None, :]


def _build_bias_pairs(rows_ref, bias_ref):
    shape = (GRID_W, HEAD_PAIR)
    c = lax.broadcasted_iota(jnp.int32, shape, 0)
    kc = lax.broadcasted_iota(jnp.int32, shape, 1) % GRID_W
    col_start = jnp.clip(c - NA_COLS // 2, 0, GRID_W - NA_COLS)
    live = (kc >= col_start) & (kc < col_start + NA_COLS)
    for d in range(N_DR_PAIRS):
        for hh in range(2):
            t = pltpu.roll(jnp.broadcast_to(rows_ref[hh, d], shape), 0, 1, stride=1, stride_axis=0)
            bias_ref[d, hh * GRID_W:(hh + 1) * GRID_W, :] = jnp.where(live, t, NEG_INF) * LOG2E


N_KEYS = NB_KEYS + CTX_LEN
ROWS_PER_STEP = 32


IMAGES_PER_STEP = 2


def _na_kernel(rows_ref, q_ref, k_ref, v_ref, kc_ref, vc_ref, o_ref,
               bias_ref, vaug, s_scr, m_scr):
    @pl.when(pl.program_id(1) == 0)
    def _():
        _build_bias_pairs(rows_ref, bias_ref)

    for img in range(IMAGES_PER_STEP):
        vaug[img, :SEQ, :HEAD_PAIR] = v_ref[img * SEQ:(img + 1) * SEQ, :]
        vaug[img, SEQ:, :HEAD_PAIR] = vc_ref[img * CTX_LEN:(img + 1) * CTX_LEN, :]
        vaug[img, :, HEAD_PAIR:] = jnp.ones((SEQ + CTX_LEN, HEAD_PAIR), BF16)

    first = _head_masks(GRID_W)
    steps_per_image = GRID_H // ROWS_PER_STEP

    def locate(i, j):
        img = i // steps_per_image
        r = (i % steps_per_image) * ROWS_PER_STEP + j
        return img, r, jnp.clip(r - NA_ROWS // 2, 0, GRID_H - NA_ROWS)

    def rows(row, n):
        return pl.ds(pl.multiple_of(row * GRID_W, GRID_W), n)

    def qk_stage(i):
        for j in range(ROWS_PER_STEP):
            img, r, start = locate(i, j)
            dr0 = start - r + (NA_ROWS - 1)
            q_r = q_ref[rows(img * GRID_H + r, GRID_W), :]
            zero = jnp.zeros_like(q_r)
            qs = jnp.concatenate([jnp.where(first, q_r, zero), jnp.where(first, zero, q_r)], axis=0)
            bias = jnp.concatenate([bias_ref[dr0 + 2 * t] for t in range(NA_ROWS // 2)], axis=1)
            k_w = k_ref[rows(img * GRID_H + start, NB_KEYS), :]
            kc = kc_ref[pl.ds(pl.multiple_of(img * CTX_LEN, CTX_LEN), CTX_LEN), :]
            s_nb = lax.dot_general(qs, k_w, _NT_DIMS, preferred_element_type=F32) + bias
            s_cx = lax.dot_general(qs, kc, _NT_DIMS, preferred_element_type=F32)
            s_scr[i % 2, j, :, :NB_KEYS] = s_nb
            s_scr[i % 2, j, :, NB_KEYS:] = s_cx
            groups = [s_nb[:, t * HEAD_PAIR:(t + 1) * HEAD_PAIR] for t in range(NB_KEYS // HEAD_PAIR)]
            groups += [s_cx[:, t * HEAD_PAIR:(t + 1) * HEAD_PAIR] for t in range(CTX_LEN // HEAD_PAIR)]
            m_scr[i % 2, j] = functools.reduce(jnp.maximum, groups)

    def softmax_pv_stage(i):
        for j in range(ROWS_PER_STEP):
            img, r, start = locate(i, j)
            m = jnp.max(m_scr[i % 2, j], axis=-1, keepdims=True)
            p = jnp.exp2((s_scr[i % 2, j] - m).astype(BF16))
            oa = (jnp.dot(p[:, :NB_KEYS], vaug[img, rows(start, NB_KEYS), :], preferred_element_type=F32)
                  + jnp.dot(p[:, NB_KEYS:], vaug[img, SEQ:, :], preferred_element_type=F32))
            o2 = oa[:, :HEAD_PAIR] / oa[:, HEAD_PAIR:]
            o_ref[rows(img * GRID_H + r, GRID_W), :] = (
                jnp.where(first, o2[:GRID_W], o2[GRID_W:]).astype(BF16))

    n_steps = IMAGES_PER_STEP * steps_per_image
    qk_stage(0)

    def step(i, carry):
        softmax_pv_stage(i - 1)
        qk_stage(i)
        return carry

    lax.fori_loop(1, n_steps, step, 0)
    softmax_pv_stage(n_steps - 1)


def _na_attention(q, k, v, rpb):
    lat_rows = IMAGES_PER_STEP * SEQ
    ctx_rows = IMAGES_PER_STEP * CTX_LEN
    lat_spec = pl.BlockSpec((lat_rows, HEAD_PAIR), lambda hp, g: (g, hp))
    ctx_spec = pl.BlockSpec((ctx_rows, HEAD_PAIR), lambda hp, g: (N_LAT // ctx_rows + g, hp))
    return pl.pallas_call(
        _na_kernel,
        grid=(N_HEAD_PAIRS, BATCH // IMAGES_PER_STEP),
        in_specs=[pl.BlockSpec((2, N_DR_PAIRS, 1, HEAD_PAIR), lambda hp, g: (hp, 0, 0, 0)),
                  lat_spec, lat_spec, lat_spec, ctx_spec, ctx_spec],
        out_specs=lat_spec,
        out_shape=jax.ShapeDtypeStruct((N_TOT, D_MODEL), BF16),
        scratch_shapes=[
            pltpu.VMEM((N_DR_PAIRS, HEAD_PAIR, HEAD_PAIR), F32),
            pltpu.VMEM((IMAGES_PER_STEP, SEQ + CTX_LEN, 2 * HEAD_PAIR), BF16),
            pltpu.VMEM((2, ROWS_PER_STEP, HEAD_PAIR, N_KEYS), F32),
            pltpu.VMEM((2, ROWS_PER_STEP, HEAD_PAIR, HEAD_PAIR), F32),
        ],
        compiler_params=pltpu.CompilerParams(
            dimension_semantics=("arbitrary", "arbitrary"),
            vmem_limit_bytes=VMEM_LIMIT_MIB["attention"] << 20),
        name="na_attention",
    )(_bias_rows(rpb), q, k, v, k, v)


def _ctx_attn_kernel(q_ref, k_ref, v_ref, o_in_ref, o_ref):
    del o_in_ref
    first = _head_masks(CTX_LEN)
    for hp in range(N_HEAD_PAIRS):
        cols = slice(hp * HEAD_PAIR, (hp + 1) * HEAD_PAIR)
        q = q_ref[:, cols]
        k = k_ref[:, cols]
        v = v_ref[:, cols]
        outs = []
        for hh in range(2):
            sel = first if hh == 0 else jnp.logical_not(first)
            qm = jnp.where(sel, q, jnp.zeros_like(q))
            s = lax.dot_general(qm, k, _NT_DIMS, preferred_element_type=F32)
            p = jnp.exp2(s - jnp.max(s, axis=-1, keepdims=True))
            denom = jnp.sum(p, axis=-1, keepdims=True)
            outs.append(jnp.dot(p.astype(BF16), v, preferred_element_type=F32) / denom)
        o_ref[:, cols] = jnp.where(first, outs[0], outs[1]).astype(BF16)


def _ctx_attention(q, k, v, o):
    ctx_spec = pl.BlockSpec((CTX_LEN, D_MODEL), lambda b: (N_LAT // CTX_LEN + b, 0))
    return pl.pallas_call(
        _ctx_attn_kernel,
        grid=(BATCH,),
        in_specs=[ctx_spec, ctx_spec, ctx_spec, pl.BlockSpec(memory_space=pl.ANY)],
        out_specs=ctx_spec,
        out_shape=jax.ShapeDtypeStruct((N_TOT, D_MODEL), BF16),
        input_output_aliases={3: 0},
        compiler_params=pltpu.CompilerParams(dimension_semantics=("parallel",)),
        name="ctx_attention",
    )(q, k, v, o)


def kernel(x, c, ctx, c_ctx, w_mod, b_mod, norm_mix, norm_ffn, w_in_ab, ln_v, w_spatial,
           b_spatial, w_pool, pool_scale, w_out_ab, w_qkv, rpb, w_out_na, w_ffn_in,
           w_ffn_out, norm_final):
    cond = jnp.zeros((MOD_ROWS, D_MODEL), F32).at[:BATCH].set(c).at[CTX_MOD_ROW].set(c_ctx)
    mod, w_qkv, w_in_ab = _adaln(cond, w_mod, b_mod, w_qkv, w_in_ab)
    mod3 = mod.reshape(DEPTH * MOD_ROWS * N_MOD, 1, D_MODEL)

    x_lat, x_ctx = x.reshape(N_LAT, D_MODEL), ctx.reshape(N_CTX, D_MODEL)
    nf = norm_final.reshape(1, D_MODEL)
    for i in range(DEPTH):
        last = i == DEPTH - 1
        j = i // 2
        nm = norm_mix[i].reshape(1, D_MODEL)
        ffn_cast = _ffn_cast_io(w_out_na if i % 2 == 1 else w_out_ab, j, w_ffn_in, w_ffn_out, i)
        if i % 2 == 1:
            q, k, v, w_o, w_in, w_out = _qkv(x_lat, mod3, i, nm, w_qkv, j, ffn_cast)
            y = _na_attention(q, k, v, rpb[j])
            if not last:
                y = _ctx_attention(q, k, v, y)
        else:
            b_s = jnp.broadcast_to(b_spatial[j][:, :, None], (A_GROUPS, CHUNK, GROUP_DIM))
            y, w_o, w_in, w_out = _ab_mixer(
                x_lat, x_ctx, mod3, i, nm, w_in_ab, j, ln_v[j].reshape(1, D_A),
                w_spatial[j].astype(BF16), b_s, w_pool[j].astype(BF16),
                pool_scale[j].reshape(1, D_B), ffn_cast)
        x_lat = x_ctx = _ffn(x_lat, x_ctx, y, mod3, i, norm_ffn[i].reshape(1, D_MODEL), w_o,
                             w_in, w_out, nf, n_rows=N_LAT if last else N_TOT, final=last)
    return x_lat.reshape(BATCH, SEQ, D_MODEL)
```

```python
import functools
import math

import jax
import jax.numpy as jnp
from jax import lax
from jax.experimental import pallas as pl
from jax.experimental.pallas import tpu as pltpu

D_MODEL = 1024
BATCH = 4
SEQ = 4096
DEPTH = 4
GRID_W = 64
GRID_H = SEQ // GRID_W
CTX_LEN = 256
D_A = D_MODEL // 2
A_GROUPS = 4
CHUNK = 128
D_B = D_MODEL - D_A
POOL_WINDOWS = (2, 4, 8, 16)
GROUP_DIM = 128
N_HEADS = 16
HEAD_DIM = 64
NA_ROWS = 8
NA_COLS = 16
D_FF = 2816
EPS = 1e-6
LOG2E = 1.4426950408889634
NEG_INF = -1e30

N_LAT = BATCH * SEQ
N_CTX = BATCH * CTX_LEN
N_TOT = N_LAT + N_CTX
MOD_ROWS = 8
CTX_MOD_ROW = BATCH
N_MOD = 6

TM_FFN = 1024
TM_QKV = 1024
TM_AB = 1024
AB_SUB = 256
HALO = 8
FF_CHUNK = 256
CAST_STEPS = 16
MOD_TN = 3072
HEAD_PAIR = 2 * HEAD_DIM
N_HEAD_PAIRS = N_HEADS // 2
NB_KEYS = NA_ROWS * GRID_W

VMEM_LIMIT_MIB = {"adaln": 48, "ffn": 60, "mixer": 48, "attention": 56}

F32 = jnp.float32
BF16 = jnp.bfloat16


def _const_spec(shape):
    nd = len(shape)
    return pl.BlockSpec(shape, lambda *_: (0,) * nd, pipeline_mode=pl.Buffered(1))


def _layer_spec(shape, layer):
    nd = len(shape)
    return pl.BlockSpec((None,) + tuple(shape), lambda *_: (layer,) + (0,) * nd,
                        pipeline_mode=pl.Buffered(1))


def _mod_spec(layer, k, tm):
    n_lat_tiles = N_LAT // tm
    tiles_per_batch = SEQ // tm

    def index_map(t, *_):
        row = jnp.where(t < n_lat_tiles, t // tiles_per_batch, CTX_MOD_ROW)
        return ((layer * MOD_ROWS + row) * N_MOD + k, 0, 0)

    return pl.BlockSpec((1, 1, D_MODEL), index_map)


def _stream_specs(tm, split):
    if not split:
        return (pl.BlockSpec((tm, D_MODEL), lambda t: (t, 0)),
                pl.BlockSpec((HALO, D_MODEL), lambda t: (0, 0)))
    n_lat = N_LAT // tm
    return (pl.BlockSpec((tm, D_MODEL), lambda t: (jnp.minimum(t, n_lat - 1), 0)),
            pl.BlockSpec((tm, D_MODEL), lambda t: (jnp.maximum(t - n_lat, 0), 0)))


def _stream_tile(xl_ref, xc_ref, tm, split):
    if not split:
        return xl_ref[...]
    return jnp.where(pl.program_id(0) >= N_LAT // tm, xc_ref[...], xl_ref[...])


def _rms(x, g):
    return x * lax.rsqrt(jnp.mean(x * x, axis=-1, keepdims=True) + EPS) * g


def _adaln_norm(x, g, scale, shift):
    return x * lax.rsqrt(jnp.mean(x * x, axis=-1, keepdims=True) + EPS) * (g * (1.0 + scale)) + shift


def _adaln_kernel(cond_ref, w_ref, b_ref, wqkv_ref, winab_ref, o_ref, wqkv_cast_ref, winab_cast_ref):
    _cast_chunks((wqkv_ref, winab_ref), (wqkv_cast_ref, winab_cast_ref))
    c = cond_ref[...]
    s = c * jax.nn.sigmoid(c)
    w = w_ref[0]
    s_hi = s.astype(BF16)
    s_lo = (s - s_hi.astype(F32)).astype(BF16)
    w_hi = w.astype(BF16)
    w_lo = (w - w_hi.astype(F32)).astype(BF16)
    s_parts = jnp.concatenate([s_hi, s_lo], axis=0)
    by_hi = jnp.dot(s_parts, w_hi, preferred_element_type=F32)
    by_lo = jnp.dot(s_parts, w_lo, preferred_element_type=F32)
    o_ref[0] = (by_hi[:MOD_ROWS] + by_hi[MOD_ROWS:]) + (by_lo[:MOD_ROWS] + by_lo[MOD_ROWS:]) + b_ref[0]


def _adaln(cond, w_mod, b_mod, w_qkv, w_in_ab):
    n_col = N_MOD * D_MODEL // MOD_TN
    n_steps = DEPTH * n_col
    w_qkv2 = w_qkv.reshape(-1, w_qkv.shape[-1])
    w_in_ab2 = w_in_ab.reshape(-1, w_in_ab.shape[-1])

    def chunk_spec(w2):
        assert w2.shape[0] % n_steps == 0
        return pl.BlockSpec((w2.shape[0] // n_steps, w2.shape[1]), lambda l, n: (l * n_col + n, 0))

    mod, w_qkv_b, w_in_ab_b = pl.pallas_call(
        _adaln_kernel,
        grid=(DEPTH, n_col),
        in_specs=[
            pl.BlockSpec((MOD_ROWS, D_MODEL), lambda l, n: (0, 0)),
            pl.BlockSpec((1, D_MODEL, MOD_TN), lambda l, n: (l, 0, n)),
            pl.BlockSpec((1, 1, MOD_TN), lambda l, n: (l, 0, n)),
            chunk_spec(w_qkv2), chunk_spec(w_in_ab2),
        ],
        out_specs=[pl.BlockSpec((1, MOD_ROWS, MOD_TN), lambda l, n: (l, 0, n)),
                   chunk_spec(w_qkv2), chunk_spec(w_in_ab2)],
        out_shape=[jax.ShapeDtypeStruct((DEPTH, MOD_ROWS, N_MOD * D_MODEL), F32),
                   jax.ShapeDtypeStruct(w_qkv2.shape, BF16),
                   jax.ShapeDtypeStruct(w_in_ab2.shape, BF16)],
        compiler_params=pltpu.CompilerParams(
            dimension_semantics=("arbitrary", "arbitrary"),
            vmem_limit_bytes=VMEM_LIMIT_MIB["adaln"] << 20),
        name="adaln",
    )(cond, w_mod, b_mod.reshape(DEPTH, 1, N_MOD * D_MODEL), w_qkv2, w_in_ab2)
    return mod, w_qkv_b.reshape(w_qkv.shape), w_in_ab_b.reshape(w_in_ab.shape)


N_CAST = 3


def _ffn_cast_io(w_o, mixer_layer, w_ffn_in, w_ffn_out, layer):
    jobs = ((w_o, mixer_layer, D_MODEL, D_MODEL), (w_ffn_in, layer, D_MODEL, 2 * D_FF),
            (w_ffn_out, layer, D_FF, D_MODEL))
    in_specs, out_specs, out_shape = [], [], []
    for _, index, rows, cols in jobs:
        assert rows % CAST_STEPS == 0
        chunk = (None, rows // CAST_STEPS, cols)
        in_specs.append(pl.BlockSpec(
            chunk, lambda t, index=index: (index, jnp.minimum(t, CAST_STEPS - 1), 0)))
        out_specs.append(pl.BlockSpec(chunk, lambda t: (0, jnp.minimum(t, CAST_STEPS - 1), 0)))
        out_shape.append(jax.ShapeDtypeStruct((1, rows, cols), BF16))
    return in_specs, [job[0] for job in jobs], out_specs, out_shape


def _cast_chunks(src_refs, dst_refs):
    for src_ref, dst_ref in zip(src_refs, dst_refs):
        dst_ref[...] = src_ref[...].astype(BF16)


def _ffn_kernel(xl_ref, xc_ref, y_ref, g1_ref, sh_ref, sc_ref, g2_ref, nrm_ref, wo_ref, wi_ref,
                wout_ref, nf_ref, o_ref, hmid_ref, *, split, final):
    x1 = _stream_tile(xl_ref, xc_ref, TM_FFN, split) + g1_ref[0] * jnp.dot(
        y_ref[...], wo_ref[...], preferred_element_type=F32)
    h = _adaln_norm(x1, nrm_ref[...], sc_ref[0], sh_ref[0])
    hb = h.astype(BF16)
    for c in range(D_FF // FF_CHUNK):
        lo = c * FF_CHUNK
        a = jnp.dot(hb, wi_ref[:, lo:lo + FF_CHUNK], preferred_element_type=F32)
        g = jnp.dot(hb, wi_ref[:, D_FF + lo:D_FF + lo + FF_CHUNK], preferred_element_type=F32)
        hmid_ref[:, lo:lo + FF_CHUNK] = (a * jax.nn.sigmoid(a) * g).astype(BF16)
    out = x1 + g2_ref[0] * jnp.dot(hmid_ref[...], wout_ref[...], preferred_element_type=F32)
    if final:
        out = _rms(out, nf_ref[...])
    o_ref[...] = out


def _ffn(x_lat, x_ctx, y, mod3, layer, norm_ffn, w_o, w_in, w_out, norm_final, *, n_rows, final):
    tm = TM_FFN
    row_spec = pl.BlockSpec((tm, D_MODEL), lambda t: (t, 0))
    split = x_ctx is not x_lat
    lat_spec, ctx_spec = _stream_specs(tm, split)
    return pl.pallas_call(
        functools.partial(_ffn_kernel, split=split, final=final),
        grid=(n_rows // tm,),
        in_specs=[
            lat_spec, ctx_spec, row_spec,
            _mod_spec(layer, 2, tm), _mod_spec(layer, 3, tm), _mod_spec(layer, 4, tm),
            _mod_spec(layer, 5, tm),
            _const_spec((1, D_MODEL)),
            _layer_spec((D_MODEL, D_MODEL), 0),
            _layer_spec((D_MODEL, 2 * D_FF), 0),
            _layer_spec((D_FF, D_MODEL), 0),
            _const_spec((1, D_MODEL)),
        ],
        out_specs=row_spec,
        out_shape=jax.ShapeDtypeStruct((n_rows, D_MODEL), F32),
        scratch_shapes=[pltpu.VMEM((tm, D_FF), BF16)],
        compiler_params=pltpu.CompilerParams(
            dimension_semantics=("arbitrary",), vmem_limit_bytes=VMEM_LIMIT_MIB["ffn"] << 20),
        name="outproj_ffn",
    )(x_lat, x_ctx, y, mod3, mod3, mod3, mod3, norm_ffn, w_o, w_in, w_out, norm_final)


def _gelu_tanh(x):
    k1 = -2.0 * math.sqrt(2.0 / math.pi) * LOG2E
    return x / (1.0 + jnp.exp2(x * (k1 + (k1 * 0.044715) * (x * x))))


def _ab_kernel(xl_ref, xc_ref, xp_ref, xn_ref, sh_ref, sc_ref, nrm_ref, win_ref, lnv_ref, ws_ref,
               bs_ref, wp_ref, ps_ref, *rest, split):
    y_ref, (p_ref, a2_ref, a4_ref) = rest[N_CAST], rest[2 * N_CAST + 1:]
    _cast_chunks(rest[:N_CAST], rest[N_CAST + 1:2 * N_CAST + 1])
    x_tile = _stream_tile(xl_ref, xc_ref, TM_AB, split)
    n_sub = TM_AB // AB_SUB
    for s in range(n_sub):
        rows = slice(s * AB_SUB, (s + 1) * AB_SUB)
        prev = xp_ref[...] if s == 0 else x_tile[s * AB_SUB - HALO:s * AB_SUB]
        nxt = xn_ref[...] if s == n_sub - 1 else x_tile[(s + 1) * AB_SUB:(s + 1) * AB_SUB + HALO]
        _ab_subtile(jnp.concatenate([prev, x_tile[rows], nxt], axis=0), pl.program_id(0) * n_sub + s,
                    sh_ref, sc_ref, nrm_ref, win_ref, lnv_ref, ws_ref, bs_ref, wp_ref, ps_ref,
                    y_ref.at[rows], p_ref.at[s], a2_ref.at[s], a4_ref.at[s])


def _ab_subtile(x_all, sub_index, sh_ref, sc_ref, nrm_ref, win_ref, lnv_ref, ws_ref, bs_ref, wp_ref,
                ps_ref, y_ref, p_ref, a2_ref, a4_ref):
    tm = AB_SUB
    is_ctx = sub_index >= N_LAT // tm
    seq_len = jnp.where(is_ctx, CTX_LEN, SEQ)
    pos0 = jnp.where(is_ctx, 0, (sub_index % (SEQ // tm)) * tm)

    h = _adaln_norm(x_all, nrm_ref[...], sc_ref[0], sh_ref[0])
    z = jnp.dot(h.astype(BF16), win_ref[...], preferred_element_type=F32)

    n_p = tm + 2 * HALO
    pos_all = pos0 - HALO + lax.broadcasted_iota(jnp.int32, (n_p, D_B), 0)
    p_ref[:n_p, :] = jnp.where((pos_all >= 0) & (pos_all < seq_len), z[:, 2 * D_A:], 0.0)
    p_ref[n_p:, :] = jnp.zeros((2 * HALO, D_B), F32)

    za = _gelu_tanh(z[HALO:HALO + tm, :2 * D_A])
    u = za[:, :D_A]
    vv = za[:, D_A:]
    vc = vv - jnp.mean(vv, axis=-1, keepdims=True)
    v = (vc * lax.rsqrt(jnp.mean(vc * vc, axis=-1, keepdims=True) + EPS) * lnv_ref[...]).astype(BF16)

    n_blocks = tm // CHUNK
    for g in range(A_GROUPS):
        cs = slice(g * GROUP_DIM, (g + 1) * GROUP_DIM)
        v_cat = jnp.concatenate([v[n * CHUNK:(n + 1) * CHUNK, cs] for n in range(n_blocks)], axis=1)
        mixed = jnp.dot(ws_ref[g], v_cat, preferred_element_type=F32)
        for n in range(n_blocks):
            rs = slice(n * CHUNK, (n + 1) * CHUNK)
            y_ref[rs, cs] = (u[rs, cs] * (mixed[:, n * CHUNK:(n + 1) * CHUNK] + bs_ref[g])).astype(BF16)

    a2_ref[...] = p_ref[0:n_p + HALO, GROUP_DIM:] + p_ref[1:n_p + HALO + 1, GROUP_DIM:]
    a4_ref[...] = a2_ref[0:n_p, :] + a2_ref[2:n_p + 2, :]
    a8 = a4_ref[0:tm + HALO, 2 * GROUP_DIM:] + a4_ref[4:tm + HALO + 4, 2 * GROUP_DIM:]
    segs = (
        p_ref[HALO - 1:HALO - 1 + tm, :GROUP_DIM] + p_ref[HALO:HALO + tm, :GROUP_DIM],
        a4_ref[HALO - 2:HALO - 2 + tm, :GROUP_DIM],
        a4_ref[HALO - 4:HALO - 4 + tm, GROUP_DIM:2 * GROUP_DIM] + a4_ref[HALO:HALO + tm, GROUP_DIM:2 * GROUP_DIM],
        a8[0:tm] + a8[HALO:HALO + tm],
    )

    def window_count(first_row, half):
        pos = pos0 + first_row + lax.broadcasted_iota(jnp.int32, (HALO, GROUP_DIM), 0)
        return (jnp.minimum(pos + half, seq_len) - jnp.maximum(pos - half, 0)).astype(F32)

    for g, w in enumerate(POOL_WINDOWS):
        half = w // 2
        cs = slice(g * GROUP_DIM, (g + 1) * GROUP_DIM)
        seg = segs[g]
        pooled = jnp.concatenate([seg[:HALO] / window_count(0, half),
                                  seg[HALO:tm - HALO] * (1.0 / w),
                                  seg[tm - HALO:] / window_count(tm - HALO, half)], axis=0)
        diff = pooled - p_ref[HALO:HALO + tm, cs]
        yb = jnp.dot(diff.astype(BF16), wp_ref[g], preferred_element_type=F32)
        y_ref[:, D_A + g * GROUP_DIM:D_A + (g + 1) * GROUP_DIM] = (yb * ps_ref[:, cs]).astype(BF16)


def _ab_mixer(x_lat, x_ctx, mod3, layer, norm_mix, w_in, mixer_layer, ln_v, w_s, b_s, w_pool,
              pool_scale, ffn_cast):
    cast_in_specs, cast_inputs, cast_out_specs, cast_out_shape = ffn_cast
    tm = TM_AB
    assert N_TOT // tm >= CAST_STEPS and AB_SUB == CTX_LEN and SEQ % tm == 0
    sub = tm // HALO
    last_halo_block = x_lat.shape[0] // HALO - 1
    split = x_ctx is not x_lat
    lat_spec, ctx_spec = _stream_specs(tm, split)
    return pl.pallas_call(
        functools.partial(_ab_kernel, split=split),
        grid=(N_TOT // tm,),
        in_specs=[
            lat_spec, ctx_spec,
            pl.BlockSpec((HALO, D_MODEL), lambda t: (jnp.clip(t * sub - 1, 0, last_halo_block), 0)),
            pl.BlockSpec((HALO, D_MODEL), lambda t: (jnp.minimum((t + 1) * sub, last_halo_block), 0)),
            _mod_spec(layer, 0, tm), _mod_spec(layer, 1, tm),
            _const_spec((1, D_MODEL)),
            _layer_spec((D_MODEL, 2 * D_A + D_B), mixer_layer),
            _const_spec((1, D_A)),
            _const_spec((A_GROUPS, CHUNK, CHUNK)),
            _const_spec((A_GROUPS, CHUNK, GROUP_DIM)),
            _const_spec((A_GROUPS, GROUP_DIM, GROUP_DIM)),
            _const_spec((1, D_B)),
            *cast_in_specs,
        ],
        out_specs=[pl.BlockSpec((tm, D_MODEL), lambda t: (t, 0)), *cast_out_specs],
        out_shape=[jax.ShapeDtypeStruct((N_TOT, D_MODEL), BF16), *cast_out_shape],
        scratch_shapes=[pltpu.VMEM((tm // AB_SUB, AB_SUB + 4 * HALO, D_B), F32),
                        pltpu.VMEM((tm // AB_SUB, AB_SUB + 3 * HALO, D_B - GROUP_DIM), F32),
                        pltpu.VMEM((tm // AB_SUB, AB_SUB + 2 * HALO, D_B - GROUP_DIM), F32)],
        compiler_params=pltpu.CompilerParams(
            dimension_semantics=("arbitrary",), vmem_limit_bytes=VMEM_LIMIT_MIB["mixer"] << 20),
        name="ab_mixer",
    )(x_lat, x_ctx, x_lat, x_lat, mod3, mod3, norm_mix, w_in, ln_v, w_s, b_s, w_pool, pool_scale,
      *cast_inputs)


def _qkv_kernel(x_ref, sh_ref, sc_ref, nrm_ref, w_ref, *rest):
    q_ref, k_ref, v_ref = rest[N_CAST:N_CAST + 3]
    _cast_chunks(rest[:N_CAST], rest[N_CAST + 3:])
    h = _adaln_norm(x_ref[...], nrm_ref[...], sc_ref[0], sh_ref[0])
    hb = h.astype(BF16)
    scale = HEAD_DIM ** -0.5 * LOG2E
    q_ref[...] = (jnp.dot(hb, w_ref[:, :D_MODEL], preferred_element_type=F32) * scale).astype(BF16)
    k_ref[...] = jnp.dot(hb, w_ref[:, D_MODEL:2 * D_MODEL], preferred_element_type=F32).astype(BF16)
    v_ref[...] = jnp.dot(hb, w_ref[:, 2 * D_MODEL:], preferred_element_type=F32).astype(BF16)


def _qkv(xs, mod3, layer, norm_mix, w_qkv, mixer_layer, ffn_cast):
    cast_in_specs, cast_inputs, cast_out_specs, cast_out_shape = ffn_cast
    tm = TM_QKV
    assert N_TOT // tm >= CAST_STEPS
    row_spec = pl.BlockSpec((tm, D_MODEL), lambda t: (t, 0))
    out = jax.ShapeDtypeStruct((N_TOT, D_MODEL), BF16)
    return pl.pallas_call(
        _qkv_kernel,
        grid=(N_TOT // tm,),
        in_specs=[row_spec, _mod_spec(layer, 0, tm), _mod_spec(layer, 1, tm),
                  _const_spec((1, D_MODEL)), _layer_spec((D_MODEL, 3 * D_MODEL), mixer_layer),
                  *cast_in_specs],
        out_specs=[row_spec, row_spec, row_spec, *cast_out_specs],
        out_shape=[out, out, out, *cast_out_shape],
        compiler_params=pltpu.CompilerParams(
            dimension_semantics=("arbitrary",), vmem_limit_bytes=VMEM_LIMIT_MIB["mixer"] << 20),
        name="qkv_proj",
    )(xs, mod3, mod3, norm_mix, w_qkv, *cast_inputs)


_NT_DIMS = (((1,), (1,)), ((), ()))


def _head_masks(rows):
    lane = lax.broadcasted_iota(jnp.int32, (rows, HEAD_PAIR), 1)
    return lane < HEAD_DIM


N_DR_PAIRS = 2 * NA_ROWS - 2


def _bias_rows(rpb):
    lo, hi = rpb[:, :-1], rpb[:, 1:]
    gap = jnp.zeros((N_HEADS, N_DR_PAIRS, GRID_W - 2 * NA_COLS + 1), F32)
    rows = jnp.concatenate([lo[..., NA_COLS - 1:], gap, hi, gap, lo[..., :NA_COLS - 1]], axis=-1)
    return rows[:, :, None, :]


def _build_bias_pairs(rows_ref, bias_ref):
    shape = (GRID_W, HEAD_PAIR)
    c = lax.broadcasted_iota(jnp.int32, shape, 0)
    kc = lax.broadcasted_iota(jnp.int32, shape, 1) % GRID_W
    col_start = jnp.clip(c - NA_COLS // 2, 0, GRID_W - NA_COLS)
    live = (kc >= col_start) & (kc < col_start + NA_COLS)
    for d in range(N_DR_PAIRS):
        for hh in range(2):
            t = pltpu.roll(jnp.broadcast_to(rows_ref[hh, d], shape), 0, 1, stride=1, stride_axis=0)
            bias_ref[d, hh * GRID_W:(hh + 1) * GRID_W, :] = jnp.where(live, t, NEG_INF) * LOG2E


N_KEYS = NB_KEYS + CTX_LEN
ROWS_PER_STEP = 32


IMAGES_PER_STEP = 2


def _na_kernel(rows_ref, q_ref, k_ref, v_ref, kc_ref, vc_ref, o_ref,
               bias_ref, vaug, s_scr, m_scr):
    @pl.when(pl.program_id(1) == 0)
    def _():
        _build_bias_pairs(rows_ref, bias_ref)

    for img in range(IMAGES_PER_STEP):
        vaug[img, :SEQ, :HEAD_PAIR] = v_ref[img * SEQ:(img + 1) * SEQ, :]
        vaug[img, SEQ:, :HEAD_PAIR] = vc_ref[img * CTX_LEN:(img + 1) * CTX_LEN, :]
        vaug[img, :, HEAD_PAIR:] = jnp.ones((SEQ + CTX_LEN, HEAD_PAIR), BF16)

    first = _head_masks(GRID_W)
    steps_per_image = GRID_H // ROWS_PER_STEP

    def locate(i, j):
        img = i // steps_per_image
        r = (i % steps_per_image) * ROWS_PER_STEP + j
        return img, r, jnp.clip(r - NA_ROWS // 2, 0, GRID_H - NA_ROWS)

    def rows(row, n):
        return pl.ds(pl.multiple_of(row * GRID_W, GRID_W), n)

    def qk_stage(i):
        for j in range(ROWS_PER_STEP):
            img, r, start = locate(i, j)
            dr0 = start - r + (NA_ROWS - 1)
            q_r = q_ref[rows(img * GRID_H + r, GRID_W), :]
            zero = jnp.zeros_like(q_r)
            qs = jnp.concatenate([jnp.where(first, q_r, zero), jnp.where(first, zero, q_r)], axis=0)
            bias = jnp.concatenate([bias_ref[dr0 + 2 * t] for t in range(NA_ROWS // 2)], axis=1)
            k_w = k_ref[rows(img * GRID_H + start, NB_KEYS), :]
            kc = kc_ref[pl.ds(pl.multiple_of(img * CTX_LEN, CTX_LEN), CTX_LEN), :]
            s_nb = lax.dot_general(qs, k_w, _NT_DIMS, preferred_element_type=F32) + bias
            s_cx = lax.dot_general(qs, kc, _NT_DIMS, preferred_element_type=F32)
            s_scr[i % 2, j, :, :NB_KEYS] = s_nb
            s_scr[i % 2, j, :, NB_KEYS:] = s_cx
            groups = [s_nb[:, t * HEAD_PAIR:(t + 1) * HEAD_PAIR] for t in range(NB_KEYS // HEAD_PAIR)]
            groups += [s_cx[:, t * HEAD_PAIR:(t + 1) * HEAD_PAIR] for t in range(CTX_LEN // HEAD_PAIR)]
            m_scr[i % 2, j] = functools.reduce(jnp.maximum, groups)

    def softmax_pv_stage(i):
        for j in range(ROWS_PER_STEP):
            img, r, start = locate(i, j)
            m = jnp.max(m_scr[i % 2, j], axis=-1, keepdims=True)
            p = jnp.exp2((s_scr[i % 2, j] - m).astype(BF16))
            oa = (jnp.dot(p[:, :NB_KEYS], vaug[img, rows(start, NB_KEYS), :], preferred_element_type=F32)
                  + jnp.dot(p[:, NB_KEYS:], vaug[img, SEQ:, :], preferred_element_type=F32))
            o2 = oa[:, :HEAD_PAIR] / oa[:, HEAD_PAIR:]
            o_ref[rows(img * GRID_H + r, GRID_W), :] = (
                jnp.where(first, o2[:GRID_W], o2[GRID_W:]).astype(BF16))

    n_steps = IMAGES_PER_STEP * steps_per_image
    qk_stage(0)

    def step(i, carry):
        softmax_pv_stage(i - 1)
        qk_stage(i)
        return carry

    lax.fori_loop(1, n_steps, step, 0)
    softmax_pv_stage(n_steps - 1)


def _na_attention(q, k, v, rpb):
    lat_rows = IMAGES_PER_STEP * SEQ
    ctx_rows = IMAGES_PER_STEP * CTX_LEN
    lat_spec = pl.BlockSpec((lat_rows, HEAD_PAIR), lambda hp, g: (g, hp))
    ctx_spec = pl.BlockSpec((ctx_rows, HEAD_PAIR), lambda hp, g: (N_LAT // ctx_rows + g, hp))
    return pl.pallas_call(
        _na_kernel,
        grid=(N_HEAD_PAIRS, BATCH // IMAGES_PER_STEP),
        in_specs=[pl.BlockSpec((2, N_DR_PAIRS, 1, HEAD_PAIR), lambda hp, g: (hp, 0, 0, 0)),
                  lat_spec, lat_spec, lat_spec, ctx_spec, ctx_spec],
        out_specs=lat_spec,
        out_shape=jax.ShapeDtypeStruct((N_TOT, D_MODEL), BF16),
        scratch_shapes=[
            pltpu.VMEM((N_DR_PAIRS, HEAD_PAIR, HEAD_PAIR), F32),
            pltpu.VMEM((IMAGES_PER_STEP, SEQ + CTX_LEN, 2 * HEAD_PAIR), BF16),
            pltpu.VMEM((2, ROWS_PER_STEP, HEAD_PAIR, N_KEYS), F32),
            pltpu.VMEM((2, ROWS_PER_STEP, HEAD_PAIR, HEAD_PAIR), F32),
        ],
        compiler_params=pltpu.CompilerParams(
            dimension_semantics=("arbitrary", "arbitrary"),
            vmem_limit_bytes=VMEM_LIMIT_MIB["attention"] << 20),
        name="na_attention",
    )(_bias_rows(rpb), q, k, v, k, v)


def _ctx_attn_kernel(q_ref, k_ref, v_ref, o_in_ref, o_ref):
    del o_in_ref
    first = _head_masks(CTX_LEN)
    for hp in range(N_HEAD_PAIRS):
        cols = slice(hp * HEAD_PAIR, (hp + 1) * HEAD_PAIR)
        q = q_ref[:, cols]
        k = k_ref[:, cols]
        v = v_ref[:, cols]
        outs = []
        for hh in range(2):
            sel = first if hh == 0 else jnp.logical_not(first)
            qm = jnp.where(sel, q, jnp.zeros_like(q))
            s = lax.dot_general(qm, k, _NT_DIMS, preferred_element_type=F32)
            p = jnp.exp2(s - jnp.max(s, axis=-1, keepdims=True))
            denom = jnp.sum(p, axis=-1, keepdims=True)
            outs.append(jnp.dot(p.astype(BF16), v, preferred_element_type=F32) / denom)
        o_ref[:, cols] = jnp.where(first, outs[0], outs[1]).astype(BF16)


def _ctx_attention(q, k, v, o):
    ctx_spec = pl.BlockSpec((CTX_LEN, D_MODEL), lambda b: (N_LAT // CTX_LEN + b, 0))
    return pl.pallas_call(
        _ctx_attn_kernel,
        grid=(BATCH,),
        in_specs=[ctx_spec, ctx_spec, ctx_spec, pl.BlockSpec(memory_space=pl.ANY)],
        out_specs=ctx_spec,
        out_shape=jax.ShapeDtypeStruct((N_TOT, D_MODEL), BF16),
        input_output_aliases={3: 0},
        compiler_params=pltpu.CompilerParams(dimension_semantics=("arbitrary",)),
        name="ctx_attention",
    )(q, k, v, o)


def kernel(x, c, ctx, c_ctx, w_mod, b_mod, norm_mix, norm_ffn, w_in_ab, ln_v, w_spatial,
           b_spatial, w_pool, pool_scale, w_out_ab, w_qkv, rpb, w_out_na, w_ffn_in,
           w_ffn_out, norm_final):
    cond = jnp.zeros((MOD_ROWS, D_MODEL), F32).at[:BATCH].set(c).at[CTX_MOD_ROW].set(c_ctx)
    mod, w_qkv, w_in_ab = _adaln(cond, w_mod, b_mod, w_qkv, w_in_ab)
    mod3 = mod.reshape(DEPTH * MOD_ROWS * N_MOD, 1, D_MODEL)

    x_lat, x_ctx = x.reshape(N_LAT, D_MODEL), ctx.reshape(N_CTX, D_MODEL)
    nf = norm_final.reshape(1, D_MODEL)
    for i in range(DEPTH):
        last = i == DEPTH - 1
        j = i // 2
        nm = norm_mix[i].reshape(1, D_MODEL)
        ffn_cast = _ffn_cast_io(w_out_na if i % 2 == 1 else w_out_ab, j, w_ffn_in, w_ffn_out, i)
        if i % 2 == 1:
            q, k, v, w_o, w_in, w_out = _qkv(x_lat, mod3, i, nm, w_qkv, j, ffn_cast)
            y = _na_attention(q, k, v, rpb[j])
            if not last:
                y = _ctx_attention(q, k, v, y)
        else:
            b_s = jnp.broadcast_to(b_spatial[j][:, :, None], (A_GROUPS, CHUNK, GROUP_DIM))
            y, w_o, w_in, w_out = _ab_mixer(
                x_lat, x_ctx, mod3, i, nm, w_in_ab, j, ln_v[j].reshape(1, D_A),
                w_spatial[j].astype(BF16), b_s, w_pool[j].astype(BF16),
                pool_scale[j].reshape(1, D_B), ffn_cast)
        x_lat = x_ctx = _ffn(x_lat, x_ctx, y, mod3, i, norm_ffn[i].reshape(1, D_MODEL), w_o,
                             w_in, w_out, nf, n_rows=N_LAT if last else N_TOT, final=last)
    return x_lat.reshape(BATCH, SEQ, D_MODEL)
```
